```python
import jax, jax.numpy as jnp
from jax import lax
import numpy as np

D_MODEL = 2048
BATCH = 8
SEQ = 2048
DEPTH = 1

HEAD_DIM_A = 64
N_Q_A = 16
N_KV_A = 4
WINDOW = 128
HEAD_DIM_B = 128
N_H_B = 8
MOBA_BLOCK = 256
MOBA_TOPK = 3
MOBA_Q_CHUNK = 16
D_FF = ((-((-8 * D_MODEL) // 3) + 255) // 256) * 256
RMS_EPS = 1e-6

WQ_A = N_Q_A * HEAD_DIM_A
WKV_A = N_KV_A * HEAD_DIM_A
W_B = N_H_B * HEAD_DIM_B
IN_COLS = WQ_A + 2 * WKV_A + 3 * W_B + 2 * D_MODEL

kernel_name = "hybrid_gated_swa_sink_moba_swiglu"


def rms_norm(x, g):
    xf = x.astype(jnp.float32)
    y = xf * lax.rsqrt(jnp.mean(xf * xf, axis=-1, keepdims=True) + RMS_EPS)
    return (y * g.astype(jnp.float32)).astype(x.dtype)


def alibi_slopes(n):
    return jnp.exp2(-8.0 * jnp.arange(1, n + 1, dtype=jnp.float32) / n)


def sliding_window_sink_attention(q, k, v, sinks, slopes):
    B, S, Hq, dh = q.shape
    Hkv = k.shape[2]
    R = Hq // Hkv
    L = WINDOW
    nb = S // L
    qb = (q * (dh ** -0.5)).reshape(B, nb, L, Hkv, R, dh)
    kb = k.reshape(B, nb, L, Hkv, dh)
    vb = v.reshape(B, nb, L, Hkv, dh)
    pad = ((0, 0), (1, 0), (0, 0), (0, 0), (0, 0))
    kwin = jnp.concatenate([jnp.pad(kb, pad)[:, :-1], kb], axis=2)
    vwin = jnp.concatenate([jnp.pad(vb, pad)[:, :-1], vb], axis=2)
    s = jnp.einsum('bnqgrd,bnkgd->bgrnqk', qb, kwin).astype(jnp.float32)
    qi = jnp.arange(L)[:, None]
    kj = jnp.arange(2 * L)[None, :]
    dist = L + qi - kj
    kpos = (jnp.arange(nb)[:, None, None] - 1) * L + kj[None]
    valid = (dist >= 0) & (dist < WINDOW) & (kpos >= 0)
    s = s - slopes.reshape(Hkv, R, 1, 1, 1) * dist.astype(jnp.float32)
    s = jnp.where(valid, s, -jnp.inf)
    sink = sinks.astype(jnp.float32).reshape(Hkv, R, 1, 1, 1)
    m = jnp.maximum(s.max(axis=-1, keepdims=True), sink)
    e = jnp.exp(s - m)
    p = e / (e.sum(axis=-1, keepdims=True) + jnp.exp(sink - m))
    o = jnp.einsum('bgrnqk,bnkgd->bnqgrd', p.astype(v.dtype), vwin)
    return o.reshape(B, S, Hq * dh)


def moba_attention(q, k, v, slopes):
    B, S, H, dh = q.shape
    nblk = -(-S // MOBA_BLOCK)
    Sp = nblk * MOBA_BLOCK
    padw = ((0, 0), (0, Sp - S), (0, 0), (0, 0))
    qh = jnp.pad(q * (dh ** -0.5), padw).transpose(0, 2, 1, 3)
    kh = jnp.pad(k, padw).transpose(0, 2, 1, 3)
    vh = jnp.pad(v, padw).transpose(0, 2, 1, 3)
    kblk = kh.reshape(B, H, nblk, MOBA_BLOCK, dh)
    vblk = vh.reshape(B, H, nblk, MOBA_BLOCK, dh)
    kmean = kblk.astype(jnp.float32).mean(axis=3)
    gate = jnp.einsum('bhsd,bhnd->bhsn', qh.astype(jnp.float32), kmean)
    cur = jnp.arange(Sp) // MOBA_BLOCK
    past = jnp.arange(nblk)[None, :] < cur[:, None]
    gate = jnp.where(past, gate, -jnp.inf)
    n_sel = min(MOBA_TOPK, nblk)
    _, sel = lax.top_k(gate, n_sel)
    sel_valid = sel < cur[:, None]
    bi = jnp.arange(B)[:, None, None, None]
    hi = jnp.arange(H)[None, :, None, None]
    slope_g = slopes[None, :, None, None, None]
    slope_o = slopes[None, :, None, None]
    offs = jnp.arange(MOBA_BLOCK)
    n_keys_sel = n_sel * MOBA_BLOCK

    def chunk(c):
        start = c * MOBA_Q_CHUNK
        qc = lax.dynamic_slice_in_dim(qh, start, MOBA_Q_CHUNK, axis=2)
        selc = lax.dynamic_slice_in_dim(sel, start, MOBA_Q_CHUNK, axis=2)
        valc = lax.dynamic_slice_in_dim(sel_valid, start, MOBA_Q_CHUNK, axis=2)
        own0 = (start // MOBA_BLOCK) * MOBA_BLOCK
        k_own = lax.dynamic_slice_in_dim(kh, own0, MOBA_BLOCK, axis=2)
        v_own = lax.dynamic_slice_in_dim(vh, own0, MOBA_BLOCK, axis=2)
        k_sel = kblk[bi, hi, selc]
        v_sel = vblk[bi, hi, selc]
        tq = start + jnp.arange(MOBA_Q_CHUNK)
        d_sel = (tq[:, None, None] - (selc[..., None] * MOBA_BLOCK + offs)).astype(jnp.float32)
        s_sel = jnp.einsum('bhqd,bhqjkd->bhqjk', qc, k_sel).astype(jnp.float32) - slope_g * d_sel
        s_sel = jnp.where(valc[..., None], s_sel, -jnp.inf)
        d_own = tq[:, None] - (own0 + offs)[None, :]
        s_own = jnp.einsum('bhqd,bhkd->bhqk', qc, k_own).astype(jnp.float32) - slope_o * d_own.astype(jnp.float32)
        s_own = jnp.where(d_own >= 0, s_own, -jnp.inf)
        logits = jnp.concatenate([s_sel.reshape(B, H, MOBA_Q_CHUNK, n_keys_sel), s_own], axis=-1)
        p = jax.nn.softmax(logits, axis=-1).astype(v.dtype)
        p_sel = p[..., :n_keys_sel].reshape(B, H, MOBA_Q_CHUNK, n_sel, MOBA_BLOCK)
        p_own = p[..., n_keys_sel:]
        return (jnp.einsum('bhqjk,bhqjkd->bhqd', p_sel, v_sel)
                + jnp.einsum('bhqk,bhkd->bhqd', p_own, v_own))

    o = lax.map(chunk, jnp.arange(Sp // MOBA_Q_CHUNK))
    o = o.transpose(1, 0, 3, 2, 4).reshape(B, Sp, H * dh)
    return o[:, :S]


def setup_inputs(seed: int = 0) -> dict:
    key = jax.random.key(seed)
    ks = jax.random.split(key, 16)
    f = jnp.float32

    def nrm(k, shape, scale):
        return jax.random.normal(k, shape, f) * scale

    return {
        "x": nrm(ks[0], (BATCH, SEQ, D_MODEL), 1.0),
        "norm1_g": 1.0 + nrm(ks[1], (DEPTH, D_MODEL), 0.02),
        "w_in": nrm(ks[2], (DEPTH, D_MODEL, IN_COLS), D_MODEL ** -0.5),
        "b_gate": nrm(ks[3], (DEPTH, 2 * D_MODEL), 0.02),
        "q_norm_a": 1.0 + nrm(ks[4], (DEPTH, HEAD_DIM_A), 0.02),
        "k_norm_a": 1.0 + nrm(ks[5], (DEPTH, HEAD_DIM_A), 0.02),
        "sinks_a": nrm(ks[6], (DEPTH, N_Q_A), 0.5),
        "q_norm_b": 1.0 + nrm(ks[7], (DEPTH, HEAD_DIM_B), 0.02),
        "k_norm_b": 1.0 + nrm(ks[8], (DEPTH, HEAD_DIM_B), 0.02),
        "w_branch_a": nrm(ks[9], (DEPTH, WQ_A, D_MODEL), WQ_A ** -0.5),
        "w_branch_b": nrm(ks[10], (DEPTH, W_B, D_MODEL), W_B ** -0.5),
        "w_o": nrm(ks[11], (DEPTH, D_MODEL, D_MODEL), D_MODEL ** -0.5),
        "norm2_g": 1.0 + nrm(ks[12], (DEPTH, D_MODEL), 0.02),
        "w_ffn_gate": nrm(ks[13], (DEPTH, D_MODEL, D_FF), D_MODEL ** -0.5),
        "w_ffn_up": nrm(ks[14], (DEPTH, D_MODEL, D_FF), D_MODEL ** -0.5),
        "w_ffn_down": nrm(ks[15], (DEPTH, D_FF, D_MODEL), D_FF ** -0.5),
    }


def reference(x, norm1_g, w_in, b_gate, q_norm_a, k_norm_a, sinks_a, q_norm_b, k_norm_b,
              w_branch_a, w_branch_b, w_o, norm2_g, w_ffn_gate, w_ffn_up, w_ffn_down):
    B, S, _ = x.shape
    slopes_a = alibi_slopes(N_Q_A)
    slopes_b = alibi_slopes(N_H_B)
    cuts = np.cumsum([WQ_A, WKV_A, WKV_A, W_B, W_B, W_B, D_MODEL]).tolist()
    h = x
    for l in range(DEPTH):
        u = rms_norm(h, norm1_g[l])
        z = u @ w_in[l]
        qa, ka, va, qb, kb, vb, ga, gb = jnp.split(z, cuts, axis=-1)
        qa = rms_norm(qa.reshape(B, S, N_Q_A, HEAD_DIM_A), q_norm_a[l])
        ka = rms_norm(ka.reshape(B, S, N_KV_A, HEAD_DIM_A), k_norm_a[l])
        va = va.reshape(B, S, N_KV_A, HEAD_DIM_A)
        o_a = sliding_window_sink_attention(qa, ka, va, sinks_a[l], slopes_a)
        qb = rms_norm(qb.reshape(B, S, N_H_B, HEAD_DIM_B), q_norm_b[l])
        kb = rms_norm(kb.reshape(B, S, N_H_B, HEAD_DIM_B), k_norm_b[l])
        vb = vb.reshape(B, S, N_H_B, HEAD_DIM_B)
        o_b = moba_attention(qb, kb, vb, slopes_b)
        gate_a = jax.nn.sigmoid(ga.astype(jnp.float32) + b_gate[l, :D_MODEL]).astype(h.dtype)
        gate_b = jax.nn.sigmoid(gb.astype(jnp.float32) + b_gate[l, D_MODEL:]).astype(h.dtype)
        mixed = gate_a * (o_a @ w_branch_a[l]) + gate_b * (o_b @ w_branch_b[l])
        h = h + mixed @ w_o[l]
        u2 = rms_norm(h, norm2_g[l])
        h = h + (jax.nn.silu(u2 @ w_ffn_gate[l]) * (u2 @ w_ffn_up[l])) @ w_ffn_down[l]
    return h
```

```python
import functools

import numpy as np
import jax
import jax.numpy as jnp
from jax import lax
from jax.experimental import pallas as pl
from jax.experimental.pallas import tpu as pltpu

F32 = jnp.float32
BF16 = jnp.bfloat16

D_MODEL = 2048
HEAD_DIM_A = 64
N_Q_A = 16
N_KV_A = 4
WINDOW = 128
HEAD_DIM_B = 128
N_H_B = 8
MOBA_BLOCK = 256
MOBA_TOPK = 3
RMS_EPS = 1e-6

WQ_A = N_Q_A * HEAD_DIM_A
WKV_A = N_KV_A * HEAD_DIM_A
W_B = N_H_B * HEAD_DIM_B
COL_QA = 0
COL_KA = COL_QA + WQ_A
COL_VA = COL_KA + WKV_A
COL_QB = COL_VA + WKV_A
COL_KB = COL_QB + W_B
COL_VB = COL_KB + W_B
COL_GA = COL_VB + W_B
COL_GB = COL_GA + D_MODEL
IN_COLS = COL_GB + D_MODEL

V7X_LANES = 128
V7X_MXU_DIM = 256
VMEM_LIMIT_BYTES = 56 * 1024 * 1024
NORM_ROW_CHUNK = 16
NEG_INF = float("-inf")


def _params(sem):
    return pltpu.CompilerParams(dimension_semantics=sem, vmem_limit_bytes=VMEM_LIMIT_BYTES)


def _dot(a, b):
    return jnp.dot(a, b, preferred_element_type=F32)


def _dot_nt(a, b):
    return lax.dot_general(a, b, (((1,), (1,)), ((), ())), preferred_element_type=F32)


def _split_bf16(x):
    hi = x.astype(BF16)
    lo = (x - hi.astype(F32)).astype(BF16)
    return hi, lo


def _rms_rows_to_bf16(x_ref, g_ref, u_ref):
    rows = x_ref.shape[0]
    g = g_ref[...]

    def body(c, carry):
        r = pl.multiple_of(c * NORM_ROW_CHUNK, NORM_ROW_CHUNK)
        x = x_ref[pl.ds(r, NORM_ROW_CHUNK), :]
        ms = jnp.mean(x * x, axis=-1, keepdims=True)
        u_ref[pl.ds(r, NORM_ROW_CHUNK), :] = (x * lax.rsqrt(ms + RMS_EPS) * g).astype(BF16)
        return carry

    lax.fori_loop(0, rows // NORM_ROW_CHUNK, body, 0)


def _in_proj_kernel(x_ref, g_ref, w_ref, o_ref, u_ref):
    @pl.when(pl.program_id(1) == 0)
    def _():
        _rms_rows_to_bf16(x_ref, g_ref, u_ref)

    o_ref[...] = _dot(u_ref[...], w_ref[...])


def _in_proj(x, g, w, *, tm=1024, tn=512):
    m, d = x.shape
    n = w.shape[1]
    return pl.pallas_call(
        _in_proj_kernel,
        grid=(m // tm, n // tn),
        in_specs=[
            pl.BlockSpec((tm, d), lambda i, j: (i, 0)),
            pl.BlockSpec((1, d), lambda i, j: (0, 0)),
            pl.BlockSpec((d, tn), lambda i, j: (0, j)),
        ],
        out_specs=pl.BlockSpec((tm, tn), lambda i, j: (i, j)),
        out_shape=jax.ShapeDtypeStruct((m, n), F32),
        scratch_shapes=[pltpu.VMEM((tm, d), BF16)],
        compiler_params=_params(("parallel", "arbitrary")),
        name="in_proj",
    )(x, g, w)


def _swa_kernel(sinks_ref, q_ref, kc_ref, kp_ref, vc_ref, vp_ref, gq_ref, gk_ref, bd_ref,
                bias_ref, o_ref):
    L = WINDOW
    first = (pl.program_id(1) == 0).astype(jnp.int32)
    bd = bd_ref[...]
    q = q_ref[0]
    k = jnp.concatenate([kp_ref[0], kc_ref[0]], axis=0)
    v = jnp.concatenate([vp_ref[0], vc_ref[0]], axis=0)

    n_qt = WQ_A // V7X_MXU_DIM
    sq = jnp.concatenate([q[:, t * 256:(t + 1) * 256] for t in range(n_qt)] + [k], axis=0)
    sq = sq * sq
    hi, lo = _split_bf16(sq)
    ms = (_dot(hi, bd) + _dot(lo, bd)) * (1.0 / HEAD_DIM_A)
    inv = lax.rsqrt(ms + RMS_EPS)
    gq = gq_ref[...] * (HEAD_DIM_A ** -0.5)
    qn = [q[:, t * 256:(t + 1) * 256] * inv[t * L:(t + 1) * L] * gq[:, t * 256:(t + 1) * 256]
          for t in range(n_qt)]
    kn = k * inv[n_qt * L:] * gk_ref[...]

    lane = lax.broadcasted_iota(jnp.int32, (1, V7X_LANES), 1)
    left = lane < HEAD_DIM_A

    def tiles(x):
        ts = [x[:, t * 128:(t + 1) * 128] for t in range(x.shape[1] // 128)]
        return ts, [pltpu.roll(t, HEAD_DIM_A, axis=1) for t in ts]

    kt, kt_sw = tiles(kn)
    vt, vt_sw = tiles(v)

    def softmax_sink(s, sink):
        m = jnp.maximum(jnp.max(s, axis=1, keepdims=True), sink)
        e = jnp.exp(s - m)
        denom = jnp.sum(e, axis=1, keepdims=True) + jnp.exp(sink - m)
        return e.astype(BF16), 1.0 / denom

    for g in range(N_KV_A):
        t, par = g // 2, g % 2
        k_l = (kt[t] if par == 0 else kt_sw[t]).astype(BF16)
        k_r = (kt_sw[t] if par == 0 else kt[t]).astype(BF16)
        v_l = jnp.where(left, vt[t] if par == 0 else vt_sw[t], 0.0).astype(BF16)
        v_r = jnp.where(left, 0.0, vt_sw[t] if par == 0 else vt[t]).astype(BF16)
        for pair in range(2):
            T = 2 * g + pair
            qt = qn[T // 2][:, (T % 2) * 128:(T % 2 + 1) * 128]
            q_l = jnp.where(left, qt, 0.0).astype(BF16)
            q_r = jnp.where(left, 0.0, qt).astype(BF16)
            h0, h1 = 2 * T, 2 * T + 1
            s0 = _dot_nt(q_l, k_l) + bias_ref[first, h0]
            s1 = _dot_nt(q_r, k_r) + bias_ref[first, h1]
            p0, i0 = softmax_sink(s0, sinks_ref[h0])
            p1, i1 = softmax_sink(s1, sinks_ref[h1])
            o = _dot(p0, v_l) * i0 + _dot(p1, v_r) * i1
            o_ref[0, :, T * 128:(T + 1) * 128] = o.astype(o_ref.dtype)


def _swa(z3, sinks, gq_t, gk_t, bd, bias):
    B, S, _ = z3.shape
    L = WINDOW
    nb = S // L
    kblk = COL_KA // WKV_A
    vblk = COL_VA // WKV_A
    prev = lambda n: jnp.maximum(n - 1, 0)
    return pl.pallas_call(
        _swa_kernel,
        grid=(B, nb),
        in_specs=[
            pl.BlockSpec(memory_space=pltpu.SMEM),
            pl.BlockSpec((1, L, WQ_A), lambda b, n: (b, n, 0)),
            pl.BlockSpec((1, L, WKV_A), lambda b, n: (b, n, kblk)),
            pl.BlockSpec((1, L, WKV_A), lambda b, n: (b, prev(n), kblk)),
            pl.BlockSpec((1, L, WKV_A), lambda b, n: (b, n, vblk)),
            pl.BlockSpec((1, L, WKV_A), lambda b, n: (b, prev(n), vblk)),
            pl.BlockSpec((1, WQ_A), lambda b, n: (0, 0)),
            pl.BlockSpec((1, WKV_A), lambda b, n: (0, 0)),
            pl.BlockSpec((V7X_MXU_DIM, V7X_MXU_DIM), lambda b, n: (0, 0)),
            pl.BlockSpec((2, N_Q_A, L, 2 * L), lambda b, n: (0, 0, 0, 0)),
        ],
        out_specs=pl.BlockSpec((1, L, WQ_A), lambda b, n: (b, n, 0)),
        out_shape=jax.ShapeDtypeStruct((B, S, WQ_A), BF16),
        compiler_params=_params(("parallel", "parallel")),
        name="swa",
    )(sinks, z3, z3, z3, z3, z3, gq_t, gk_t, bd, bias)


def _swa_bias_table():
    L = WINDOW
    slopes = np.exp2(-8.0 * np.arange(1, N_Q_A + 1, dtype=np.float32) / N_Q_A).astype(np.float32)
    qi = np.arange(L)[:, None]
    kj = np.arange(2 * L)[None, :]
    dist = L + qi - kj
    window = (dist >= 0) & (dist < WINDOW)
    table = np.empty((2, N_Q_A, L, 2 * L), np.float32)
    for first in range(2):
        valid = window & ((kj >= L) if first else True)
        table[first] = np.where(valid[None], -slopes[:, None, None] * dist[None].astype(np.float32), -np.inf)
    return jnp.asarray(table)


def _block_diag_ones(width, block):
    idx = np.arange(width) // block
    return jnp.asarray((idx[:, None] == idx[None, :]).astype(np.float32), dtype=BF16)


def _moba_kernel(slopes_ref, q_ref, k_ref, v_ref, gq_ref, gk_ref, o_ref,
                 kn_ref, vb_ref, kmean_ref, rb_ref):
    BLK = MOBA_BLOCK
    h = pl.program_id(1)
    i = pl.program_id(2)
    nblk = k_ref.shape[1] // BLK
    slope = slopes_ref[h]

    @pl.when(i == 0)
    def _():
        k = k_ref[0]
        ms = jnp.mean(k * k, axis=-1, keepdims=True)
        kn = k * lax.rsqrt(ms + RMS_EPS) * gk_ref[...]
        kn_ref[...] = kn.astype(BF16)
        vb_ref[...] = v_ref[0].astype(BF16)
        kmean_ref[...] = jnp.zeros_like(kmean_ref)
        kmean_ref[0:nblk, :] = jnp.mean(kn.reshape(nblk, BLK, HEAD_DIM_B), axis=1)

    q = q_ref[0]
    ms = jnp.mean(q * q, axis=-1, keepdims=True)
    qs = q * lax.rsqrt(ms + RMS_EPS) * (gq_ref[...] * (HEAD_DIM_B ** -0.5))
    qb = qs.astype(BF16)

    q_hi, q_lo = _split_bf16(qs)
    km_hi, km_lo = _split_bf16(kmean_ref[...])
    gate = _dot_nt(q_hi, km_hi) + _dot_nt(q_hi, km_lo) + _dot_nt(q_lo, km_hi)

    lane = lax.broadcasted_iota(jnp.int32, (1, V7X_LANES), 1)
    row = lax.broadcasted_iota(jnp.int32, (BLK, 1), 0).astype(F32)
    past = lane < i
    i_f = i.astype(F32)
    for n in range(nblk - 1):
        g_n = gate[:, n:n + 1]
        beats = (gate > g_n) | ((gate == g_n) & (lane < n))
        cnt = jnp.sum(jnp.where(beats & past, 1.0, 0.0), axis=1, keepdims=True)
        sel = (cnt < float(MOBA_TOPK)) & (i > n)
        rowbias = jnp.where(sel, -slope * ((i_f - float(n)) * float(BLK) + row), NEG_INF)
        rb_ref[n] = jnp.broadcast_to(rowbias, (BLK, V7X_LANES))

    col = lax.broadcasted_iota(jnp.int32, (1, BLK), 1).astype(F32)
    colbias = slope * col

    r0 = pl.multiple_of(i * BLK, BLK)
    kd = kn_ref[pl.ds(r0, BLK), :]
    vd = vb_ref[pl.ds(r0, BLK), :]
    rel = row - col
    s = jnp.where(rel >= 0.0, _dot_nt(qb, kd) - slope * rel, NEG_INF)
    m = jnp.max(s, axis=1, keepdims=True)
    p = jnp.exp(s - m)
    l = jnp.sum(p, axis=1, keepdims=True)
    acc = _dot(p.astype(BF16), vd)

    def body(j, carry):
        m, l, acc = carry
        rj = pl.multiple_of(j * BLK, BLK)
        kj = kn_ref[pl.ds(rj, BLK), :]
        vj = vb_ref[pl.ds(rj, BLK), :]
        rb = rb_ref[j]
        s = _dot_nt(qb, kj) + jnp.concatenate([rb, rb], axis=1) + colbias
        m_new = jnp.maximum(m, jnp.max(s, axis=1, keepdims=True))
        alpha = jnp.exp(m - m_new)
        p = jnp.exp(s - m_new)
        l = alpha * l + jnp.sum(p, axis=1, keepdims=True)
        acc = alpha * acc + _dot(p.astype(BF16), vj)
        return m_new, l, acc

    m, l, acc = lax.fori_loop(0, i, body, (m, l, acc))
    o_ref[0] = (acc / l).astype(o_ref.dtype)


def _moba(z3, slopes, gq, gk):
    B, S, _ = z3.shape
    BLK = MOBA_BLOCK
    nblk = S // BLK
    qc, kc, vc = COL_QB // HEAD_DIM_B, COL_KB // HEAD_DIM_B, COL_VB // HEAD_DIM_B
    return pl.pallas_call(
        _moba_kernel,
        grid=(B, N_H_B, nblk),
        in_specs=[
            pl.BlockSpec(memory_space=pltpu.SMEM),
            pl.BlockSpec((1, BLK, HEAD_DIM_B), lambda b, h, i: (b, i, qc + h)),
            pl.BlockSpec((1, S, HEAD_DIM_B), lambda b, h, i: (b, 0, kc + h)),
            pl.BlockSpec((1, S, HEAD_DIM_B), lambda b, h, i: (b, 0, vc + h)),
            pl.BlockSpec((1, HEAD_DIM_B), lambda b, h, i: (0, 0)),
            pl.BlockSpec((1, HEAD_DIM_B), lambda b, h, i: (0, 0)),
        ],
        out_specs=pl.BlockSpec((1, BLK, HEAD_DIM_B), lambda b, h, i: (b, i, h)),
        out_shape=jax.ShapeDtypeStruct((B, S, W_B), BF16),
        scratch_shapes=[
            pltpu.VMEM((S, HEAD_DIM_B), BF16),
            pltpu.VMEM((S, HEAD_DIM_B), BF16),
            pltpu.VMEM((V7X_LANES, HEAD_DIM_B), F32),
            pltpu.VMEM((nblk, BLK, V7X_LANES), F32),
        ],
        compiler_params=_params(("parallel", "parallel", "arbitrary")),
        name="moba",
    )(slopes, z3, z3, z3, gq, gk)


def _sigmoid(x):
    return 1.0 / (1.0 + jnp.exp(-x))


def _merge_kernel(oa_ref, ob_ref, wa_ref, wb_ref, ga_ref, gb_ref, ba_ref, bb_ref, o_ref):
    a = _dot(oa_ref[...], wa_ref[...])
    b = _dot(ob_ref[...], wb_ref[...])
    ga = _sigmoid(ga_ref[...] + ba_ref[...])
    gb = _sigmoid(gb_ref[...] + bb_ref[...])
    o_ref[...] = (ga * a + gb * b).astype(o_ref.dtype)


def _merge(oa, ob, wa, wb, z, ba, bb, *, tm=1024, tn=512):
    m, ka = oa.shape
    kb = ob.shape[1]
    n = wa.shape[1]
    ga_blk, gb_blk = COL_GA // tn, COL_GB // tn
    return pl.pallas_call(
        _merge_kernel,
        grid=(m // tm, n // tn),
        in_specs=[
            pl.BlockSpec((tm, ka), lambda i, j: (i, 0)),
            pl.BlockSpec((tm, kb), lambda i, j: (i, 0)),
            pl.BlockSpec((ka, tn), lambda i, j: (0, j)),
            pl.BlockSpec((kb, tn), lambda i, j: (0, j)),
            pl.BlockSpec((tm, tn), lambda i, j: (i, ga_blk + j)),
            pl.BlockSpec((tm, tn), lambda i, j: (i, gb_blk + j)),
            pl.BlockSpec((1, tn), lambda i, j: (0, j)),
            pl.BlockSpec((1, tn), lambda i, j: (0, j)),
        ],
        out_specs=pl.BlockSpec((tm, tn), lambda i, j: (i, j)),
        out_shape=jax.ShapeDtypeStruct((m, n), BF16),
        compiler_params=_params(("parallel", "parallel")),
        name="merge",
    )(oa, ob, wa, wb, z, z, ba, bb)


def _mm_res_kernel(a_ref, w_ref, r_ref, o_ref):
    o_ref[...] = r_ref[...] + _dot(a_ref[...], w_ref[...])


def _mm_res(a, w, res, *, tm, tn, name):
    m, k = a.shape
    n = w.shape[1]
    return pl.pallas_call(
        _mm_res_kernel,
        grid=(m // tm, n // tn),
        in_specs=[
            pl.BlockSpec((tm, k), lambda i, j: (i, 0)),
            pl.BlockSpec((k, tn), lambda i, j: (0, j)),
            pl.BlockSpec((tm, tn), lambda i, j: (i, j)),
        ],
        out_specs=pl.BlockSpec((tm, tn), lambda i, j: (i, j)),
        out_shape=jax.ShapeDtypeStruct((m, n), F32),
        compiler_params=_params(("parallel", "parallel")),
        name=name,
    )(a, w, res)


def _ffn_up_kernel(x_ref, g_ref, wg_ref, wu_ref, o_ref, u_ref):
    @pl.when(pl.program_id(1) == 0)
    def _():
        _rms_rows_to_bf16(x_ref, g_ref, u_ref)

    u = u_ref[...]
    gate = _dot(u, wg_ref[...])
    up = _dot(u, wu_ref[...])
    o_ref[...] = (gate * _sigmoid(gate) * up).astype(o_ref.dtype)


def _ffn_up(x, g, wg, wu, *, tm=1024, tn=512):
    m, d = x.shape
    n = wg.shape[1]
    return pl.pallas_call(
        _ffn_up_kernel,
        grid=(m // tm, n // tn),
        in_specs=[
            pl.BlockSpec((tm, d), lambda i, j: (i, 0)),
            pl.BlockSpec((1, d), lambda i, j: (0, 0)),
            pl.BlockSpec((d, tn), lambda i, j: (0, j)),
            pl.BlockSpec((d, tn), lambda i, j: (0, j)),
        ],
        out_specs=pl.BlockSpec((tm, tn), lambda i, j: (i, j)),
        out_shape=jax.ShapeDtypeStruct((m, n), BF16),
        scratch_shapes=[pltpu.VMEM((tm, d), BF16)],
        compiler_params=_params(("parallel", "arbitrary")),
        name="ffn_up",
    )(x, g, wg, wu)


def kernel(x, norm1_g, w_in, b_gate, q_norm_a, k_norm_a, sinks_a, q_norm_b, k_norm_b,
           w_branch_a, w_branch_b, w_o, norm2_g, w_ffn_gate, w_ffn_up, w_ffn_down):
    B, S, D = x.shape
    depth = w_in.shape[0]
    assert D == D_MODEL and w_in.shape[2] == IN_COLS
    assert S % MOBA_BLOCK == 0 and S % WINDOW == 0
    M = B * S

    slopes_b = jnp.asarray(np.exp2(-8.0 * np.arange(1, N_H_B + 1, dtype=np.float32) / N_H_B), F32)
    swa_bias = _swa_bias_table()
    bd = _block_diag_ones(V7X_MXU_DIM, HEAD_DIM_A)

    h = x.reshape(M, D)
    for l in range(depth):
        z = _in_proj(h, norm1_g[l].reshape(1, D), w_in[l].astype(BF16))
        z3 = z.reshape(B, S, IN_COLS)
        o_a = _swa(z3, sinks_a[l],
                   jnp.tile(q_norm_a[l], N_Q_A).reshape(1, WQ_A),
                   jnp.tile(k_norm_a[l], N_KV_A).reshape(1, WKV_A), bd, swa_bias)
        o_b = _moba(z3, slopes_b, q_norm_b[l].reshape(1, HEAD_DIM_B), k_norm_b[l].reshape(1, HEAD_DIM_B))
        mixed = _merge(o_a.reshape(M, WQ_A), o_b.reshape(M, W_B),
                       w_branch_a[l].astype(BF16), w_branch_b[l].astype(BF16), z,
                       b_gate[l, :D].reshape(1, D), b_gate[l, D:].reshape(1, D))
        h1 = _mm_res(mixed, w_o[l].astype(BF16), h, tm=1024, tn=1024, name="out_proj")
        act = _ffn_up(h1, norm2_g[l].reshape(1, D), w_ffn_gate[l].astype(BF16), w_ffn_up[l].astype(BF16))
        h = _mm_res(act, w_ffn_down[l].astype(BF16), h1, tm=1024, tn=512, name="ffn_down")
    return h.reshape(B, S, D)
```

```python
import functools

import numpy as np
import jax
import jax.numpy as jnp
from jax import lax
from jax.experimental import pallas as pl
from jax.experimental.pallas import tpu as pltpu

F32 = jnp.float32
BF16 = jnp.bfloat16

D_MODEL = 2048
HEAD_DIM_A = 64
N_Q_A = 16
N_KV_A = 4
WINDOW = 128
HEAD_DIM_B = 128
N_H_B = 8
MOBA_BLOCK = 256
MOBA_TOPK = 3
RMS_EPS = 1e-6

WQ_A = N_Q_A * HEAD_DIM_A
WKV_A = N_KV_A * HEAD_DIM_A
W_B = N_H_B * HEAD_DIM_B
COL_QA = 0
COL_KA = COL_QA + WQ_A
COL_VA = COL_KA + WKV_A
COL_QB = COL_VA + WKV_A
COL_KB = COL_QB + W_B
COL_VB = COL_KB + W_B
COL_GA = COL_VB + W_B
COL_GB = COL_GA + D_MODEL
IN_COLS = COL_GB + D_MODEL

V7X_LANES = 128
V7X_MXU_DIM = 256
VMEM_LIMIT_BYTES = 56 * 1024 * 1024
NORM_ROW_CHUNK = 16
MOBA_HEADS_PER_STEP = V7X_MXU_DIM // HEAD_DIM_B
MOBA_SCORE_LOOKAHEAD = 4
GATE_ROWS = 16
NEG_INF = float("-inf")
LOG2E = 1.4426950408889634


def _params(sem):
    return pltpu.CompilerParams(dimension_semantics=sem, vmem_limit_bytes=VMEM_LIMIT_BYTES)


def _dot(a, b):
    return jnp.dot(a, b, preferred_element_type=F32)


def _dot_nt(a, b):
    return lax.dot_general(a, b, (((1,), (1,)), ((), ())), preferred_element_type=F32)


def _split_bf16(x):
    hi = x.astype(BF16)
    lo = (x - hi.astype(F32)).astype(BF16)
    return hi, lo


def _rms_rows_to_bf16(x_ref, g_ref, u_ref):
    rows = x_ref.shape[0]
    g = g_ref[...]

    def body(c, carry):
        r = pl.multiple_of(c * NORM_ROW_CHUNK, NORM_ROW_CHUNK)
        x = x_ref[pl.ds(r, NORM_ROW_CHUNK), :]
        ms = jnp.mean(x * x, axis=-1, keepdims=True)
        u_ref[pl.ds(r, NORM_ROW_CHUNK), :] = (x * lax.rsqrt(ms + RMS_EPS) * g).astype(BF16)
        return carry

    lax.fori_loop(0, rows // NORM_ROW_CHUNK, body, 0)


def _in_proj_kernel(x_ref, g_ref, w_ref, o_ref, u_ref):
    @pl.when(pl.program_id(1) == 0)
    def _():
        _rms_rows_to_bf16(x_ref, g_ref, u_ref)

    o_ref[...] = _dot(u_ref[...], w_ref[...])


def _in_proj(x, g, w, *, tm=1024, tn=512):
    m, d = x.shape
    n = w.shape[1]
    return pl.pallas_call(
        _in_proj_kernel,
        grid=(m // tm, n // tn),
        in_specs=[
            pl.BlockSpec((tm, d), lambda i, j: (i, 0)),
            pl.BlockSpec((1, d), lambda i, j: (0, 0)),
            pl.BlockSpec((d, tn), lambda i, j: (0, j)),
        ],
        out_specs=pl.BlockSpec((tm, tn), lambda i, j: (i, j)),
        out_shape=jax.ShapeDtypeStruct((m, n), F32),
        scratch_shapes=[pltpu.VMEM((tm, d), BF16)],
        compiler_params=_params(("parallel", "arbitrary")),
        name="in_proj",
    )(x, g, w)


def _swa_kernel(sinks_ref, q_ref, kc_ref, kp_ref, vc_ref, vp_ref, gq_ref, gk_ref, bd_ref,
                bias_ref, o_ref):
    L = WINDOW
    first = (pl.program_id(1) == 0).astype(jnp.int32)
    bd = bd_ref[...]
    q = q_ref[0]
    k = jnp.concatenate([kp_ref[0], kc_ref[0]], axis=0)
    v = jnp.concatenate([vp_ref[0], vc_ref[0]], axis=0)

    n_qt = WQ_A // V7X_MXU_DIM
    sq = jnp.concatenate([q[:, t * 256:(t + 1) * 256] for t in range(n_qt)] + [k], axis=0)
    sq = sq * sq
    hi, lo = _split_bf16(sq)
    inv = lax.rsqrt(_dot(hi, bd) + _dot(lo, bd) + RMS_EPS)
    gq = gq_ref[...] * (HEAD_DIM_A ** -0.5 * LOG2E)
    qn = [q[:, t * 256:(t + 1) * 256] * inv[t * L:(t + 1) * L] * gq[:, t * 256:(t + 1) * 256]
          for t in range(n_qt)]
    kn = k * inv[n_qt * L:] * gk_ref[...]

    lane = lax.broadcasted_iota(jnp.int32, (1, V7X_LANES), 1)
    left = lane < HEAD_DIM_A
    lo_head = lax.broadcasted_iota(jnp.int32, (1, 2 * L), 1) < L

    kt = [kn[:, t * 128:(t + 1) * 128] for t in range(WKV_A // 128)]
    kt_sw = [pltpu.roll(x, HEAD_DIM_A, axis=1) for x in kt]
    vt = [v[:, t * 128:(t + 1) * 128].T.astype(BF16) for t in range(WKV_A // 128)]

    def q_masked(h):
        tile = h // 2
        qt = qn[tile // 2][:, (tile % 2) * 128:(tile % 2 + 1) * 128]
        keep = left if h % 2 == 0 else jnp.logical_not(left)
        return jnp.where(keep, qt, 0.0).astype(BF16)

    rep = N_Q_A // N_KV_A

    def scores(g, hpar):
        t, par = g // 2, g % 2
        ha, hb = rep * g + hpar, rep * g + hpar + 2
        k_al = kt[t] if par == hpar else kt_sw[t]
        qm = jnp.concatenate([q_masked(ha), q_masked(hb)], axis=0)
        return _dot_nt(k_al.astype(BF16), qm) + bias_ref[first, 2 * g + hpar]

    pairs = [(g, hpar) for g in range(N_KV_A) for hpar in range(2)]
    all_scores = [scores(g, hpar) for g, hpar in pairs]

    out_rows = [None] * N_Q_A
    for (g, hpar), s in zip(pairs, all_scores):
        t, par = g // 2, g % 2
        ha, hb = rep * g + hpar, rep * g + hpar + 2
        sink = jnp.where(lo_head, sinks_ref[ha], sinks_ref[hb]) * LOG2E
        m = jnp.maximum(jnp.max(s, axis=0, keepdims=True), sink)
        e = jnp.exp2(s - m)
        denom = jnp.sum(e, axis=0, keepdims=True) + jnp.exp2(sink - m)
        ot = _dot(vt[t], e.astype(BF16))
        og = ot[par * HEAD_DIM_A:(par + 1) * HEAD_DIM_A, :] * (1.0 / denom)
        out_rows[ha] = og[:, :L]
        out_rows[hb] = og[:, L:]
    o_t = jnp.concatenate(out_rows, axis=0)
    o_ref[0] = o_t.T.astype(o_ref.dtype)


def _swa(z3, sinks, gq_t, gk_t, bd, bias):
    B, S, _ = z3.shape
    L = WINDOW
    nb = S // L
    kblk = COL_KA // WKV_A
    vblk = COL_VA // WKV_A
    prev = lambda n: jnp.maximum(n - 1, 0)
    return pl.pallas_call(
        _swa_kernel,
        grid=(B, nb),
        in_specs=[
            pl.BlockSpec(memory_space=pltpu.SMEM),
            pl.BlockSpec((1, L, WQ_A), lambda b, n: (b, n, 0)),
            pl.BlockSpec((1, L, WKV_A), lambda b, n: (b, n, kblk)),
            pl.BlockSpec((1, L, WKV_A), lambda b, n: (b, prev(n), kblk)),
            pl.BlockSpec((1, L, WKV_A), lambda b, n: (b, n, vblk)),
            pl.BlockSpec((1, L, WKV_A), lambda b, n: (b, prev(n), vblk)),
            pl.BlockSpec((1, WQ_A), lambda b, n: (0, 0)),
            pl.BlockSpec((1, WKV_A), lambda b, n: (0, 0)),
            pl.BlockSpec((V7X_MXU_DIM, V7X_MXU_DIM), lambda b, n: (0, 0)),
            pl.BlockSpec((2, N_Q_A // 2, 2 * L, 2 * L), lambda b, n: (0, 0, 0, 0)),
        ],
        out_specs=pl.BlockSpec((1, L, WQ_A), lambda b, n: (b, n, 0)),
        out_shape=jax.ShapeDtypeStruct((B, S, WQ_A), BF16),
        compiler_params=_params(("parallel", "parallel")),
        name="swa",
    )(sinks, z3, z3, z3, z3, z3, gq_t, gk_t, bd, bias)


def _swa_bias_table():
    L = WINDOW
    rep = N_Q_A // N_KV_A
    slopes = np.exp2(-8.0 * np.arange(1, N_Q_A + 1, dtype=np.float32) / N_Q_A).astype(np.float32)
    kj = np.arange(2 * L)[:, None]
    qi = np.arange(L)[None, :]
    dist = L + qi - kj
    window = (dist >= 0) & (dist < WINDOW)
    table = np.empty((2, N_Q_A // 2, 2 * L, 2 * L), np.float32)
    for first in range(2):
        valid = window & ((kj >= L) if first else True)
        for g in range(N_KV_A):
            for hpar in range(2):
                for a in range(2):
                    h = rep * g + hpar + 2 * a
                    table[first, 2 * g + hpar, :, a * L:(a + 1) * L] = np.where(
                        valid, -(slopes[h] * LOG2E) * dist.astype(np.float32), -np.inf)
    return jnp.asarray(table)


def _block_diag(width, block, value):
    idx = np.arange(width) // block
    return jnp.asarray((idx[:, None] == idx[None, :]).astype(np.float32) * value, dtype=BF16)


def _moba_kernel(slopes_ref, q_ref, k_ref, v_ref, gq_ref, gk_ref, o_ref,
                 kn_ref, vt_ref, kmean_ref, tab_ref):
    BLK = MOBA_BLOCK
    HP = MOBA_HEADS_PER_STEP
    dh = HEAD_DIM_B
    hp = pl.program_id(1)
    i = pl.program_id(2)
    nblk = k_ref.shape[1] // BLK

    @pl.when(i == 0)
    def _prepare():
        kc = lax.broadcasted_iota(jnp.int32, (BLK, BLK), 0)
        qr = lax.broadcasted_iota(jnp.int32, (BLK, BLK), 1)
        rel = (qr - kc).astype(F32)
        for a in range(HP):
            k = k_ref[0, :, a * dh:(a + 1) * dh]
            ms = jnp.mean(k * k, axis=-1, keepdims=True)
            kn = k * lax.rsqrt(ms + RMS_EPS) * gk_ref[...]
            kn_ref[a] = kn.astype(BF16)
            kmean_ref[a] = jnp.zeros((GATE_ROWS, dh), F32)
            kmean_ref[a, 0:nblk, :] = jnp.mean(kn.reshape(nblk, BLK, dh), axis=1)
            vt_ref[a] = v_ref[0, :, a * dh:(a + 1) * dh].T.astype(BF16)
            slope2 = slopes_ref[hp * HP + a] * LOG2E
            tab_ref[a, 0] = -slope2 * rel
            tab_ref[a, 1] = jnp.where(rel >= 0.0, -slope2 * rel, NEG_INF)

    def attend(c):
        qb, block_max_shift = [], []
        for a in range(HP):
            slope2 = slopes_ref[hp * HP + a] * LOG2E
            q = q_ref[0, :, a * dh:(a + 1) * dh]
            ms = jnp.mean(q * q, axis=-1, keepdims=True)
            qs = q * lax.rsqrt(ms + RMS_EPS) * (gq_ref[...] * (dh ** -0.5 * LOG2E))
            qb.append(qs.astype(BF16))

            shifts = [-slope2 * float((c - n) * BLK) for n in range(c)]
            if c > MOBA_TOPK:
                q_hi, q_lo = _split_bf16(qs)
                km_hi, km_lo = _split_bf16(kmean_ref[a])
                gate = _dot_nt(km_hi, q_hi) + _dot_nt(km_lo, q_hi) + _dot_nt(km_hi, q_lo)
                blk = lax.broadcasted_iota(jnp.int32, (GATE_ROWS, 1), 0)
                past = blk < c
                for n in range(c):
                    g_n = gate[n:n + 1, :]
                    beats = ((gate > g_n) | ((gate == g_n) & (blk < n))) & past
                    rank = jnp.sum(jnp.where(beats, 1.0, 0.0), axis=0, keepdims=True)
                    shifts[n] = jnp.where(rank < float(MOBA_TOPK), shifts[n], NEG_INF)
            block_max_shift.append(shifts)

        def scores(a, j):
            s = _dot_nt(kn_ref[a, j * BLK:(j + 1) * BLK, :], qb[a]) + tab_ref[a, 1 if j == c else 0]
            return s, jnp.max(s, axis=0, keepdims=True)

        def probs(s, m):
            e = jnp.exp2(s - m)
            return e.astype(BF16), jnp.sum(e, axis=0, keepdims=True)

        def weighted_values(a, j, p):
            return _dot(vt_ref[a, :, j * BLK:(j + 1) * BLK], p)

        tasks = [(a, j) for j in range(c + 1) for a in range(HP)]
        ahead = {}
        stats = [([], [], []) for _ in range(HP)]
        for t in range(len(tasks) + MOBA_SCORE_LOOKAHEAD):
            if t < len(tasks):
                ahead[t] = scores(*tasks[t])
            d = t - MOBA_SCORE_LOOKAHEAD
            if d >= 0:
                a, j = tasks[d]
                s, m_j = ahead.pop(d)
                p, l_j = probs(s, m_j)
                ms_, ls_, os_ = stats[a]
                os_.append(weighted_values(a, j, p))
                ls_.append(l_j)
                ms_.append(m_j + block_max_shift[a][j] if j < c else m_j)

        for a in range(HP):
            ms_, ls_, os_ = stats[a]
            m = functools.reduce(jnp.maximum, ms_)
            ws = [jnp.exp2(m_j - m) for m_j in ms_]
            l = functools.reduce(jnp.add, [w * l_j for w, l_j in zip(ws, ls_)])
            ot = functools.reduce(jnp.add, [w * o_j for w, o_j in zip(ws, os_)])
            ot = ot * (1.0 / l)
            o_ref[0, :, a * dh:(a + 1) * dh] = ot.T.astype(o_ref.dtype)

    for c in range(nblk):
        pl.when(i == c)(functools.partial(attend, c))


def _moba(z3, slopes, gq, gk):
    B, S, _ = z3.shape
    BLK = MOBA_BLOCK
    HP = MOBA_HEADS_PER_STEP
    nblk = S // BLK
    w = HP * HEAD_DIM_B
    qc, kc, vc = COL_QB // w, COL_KB // w, COL_VB // w
    return pl.pallas_call(
        _moba_kernel,
        grid=(B, N_H_B // HP, nblk),
        in_specs=[
            pl.BlockSpec(memory_space=pltpu.SMEM),
            pl.BlockSpec((1, BLK, w), lambda b, h, i: (b, i, qc + h)),
            pl.BlockSpec((1, S, w), lambda b, h, i: (b, 0, kc + h)),
            pl.BlockSpec((1, S, w), lambda b, h, i: (b, 0, vc + h)),
            pl.BlockSpec((1, HEAD_DIM_B), lambda b, h, i: (0, 0)),
            pl.BlockSpec((1, HEAD_DIM_B), lambda b, h, i: (0, 0)),
        ],
        out_specs=pl.BlockSpec((1, BLK, w), lambda b, h, i: (b, i, h)),
        out_shape=jax.ShapeDtypeStruct((B, S, W_B), BF16),
        scratch_shapes=[
            pltpu.VMEM((HP, S, HEAD_DIM_B), BF16),
            pltpu.VMEM((HP, HEAD_DIM_B, S), BF16),
            pltpu.VMEM((HP, GATE_ROWS, HEAD_DIM_B), F32),
            pltpu.VMEM((HP, 2, BLK, BLK), F32),
        ],
        compiler_params=_params(("parallel", "parallel", "arbitrary")),
        name="moba",
    )(slopes, z3, z3, z3, gq, gk)


def _sigmoid(x):
    return 1.0 / (1.0 + jnp.exp(-x))


def _merge_kernel(oa_ref, ob_ref, wa_ref, wb_ref, ga_ref, gb_ref, ba_ref, bb_ref, o_ref):
    a = _dot(oa_ref[...], wa_ref[...])
    b = _dot(ob_ref[...], wb_ref[...])
    ga = _sigmoid(ga_ref[...] + ba_ref[...])
    gb = _sigmoid(gb_ref[...] + bb_ref[...])
    o_ref[...] = (ga * a + gb * b).astype(o_ref.dtype)


def _merge(oa, ob, wa, wb, z, ba, bb, *, tm=1024, tn=512):
    m, ka = oa.shape
    kb = ob.shape[1]
    n = wa.shape[1]
    ga_blk, gb_blk = COL_GA // tn, COL_GB // tn
    return pl.pallas_call(
        _merge_kernel,
        grid=(m // tm, n // tn),
        in_specs=[
            pl.BlockSpec((tm, ka), lambda i, j: (i, 0)),
            pl.BlockSpec((tm, kb), lambda i, j: (i, 0)),
            pl.BlockSpec((ka, tn), lambda i, j: (0, j)),
            pl.BlockSpec((kb, tn), lambda i, j: (0, j)),
            pl.BlockSpec((tm, tn), lambda i, j: (i, ga_blk + j)),
            pl.BlockSpec((tm, tn), lambda i, j: (i, gb_blk + j)),
            pl.BlockSpec((1, tn), lambda i, j: (0, j)),
            pl.BlockSpec((1, tn), lambda i, j: (0, j)),
        ],
        out_specs=pl.BlockSpec((tm, tn), lambda i, j: (i, j)),
        out_shape=jax.ShapeDtypeStruct((m, n), BF16),
        compiler_params=_params(("parallel", "parallel")),
        name="merge",
    )(oa, ob, wa, wb, z, z, ba, bb)


def _mm_res_kernel(a_ref, w_ref, r_ref, o_ref):
    o_ref[...] = r_ref[...] + _dot(a_ref[...], w_ref[...])


def _mm_res(a, w, res, *, tm, tn, name):
    m, k = a.shape
    n = w.shape[1]
    return pl.pallas_call(
        _mm_res_kernel,
        grid=(m // tm, n // tn),
        in_specs=[
            pl.BlockSpec((tm, k), lambda i, j: (i, 0)),
            pl.BlockSpec((k, tn), lambda i, j: (0, j)),
            pl.BlockSpec((tm, tn), lambda i, j: (i, j)),
        ],
        out_specs=pl.BlockSpec((tm, tn), lambda i, j: (i, j)),
        out_shape=jax.ShapeDtypeStruct((m, n), F32),
        compiler_params=_params(("parallel", "parallel")),
        name=name,
    )(a, w, res)


def _ffn_up_kernel(x_ref, g_ref, wg_ref, wu_ref, o_ref, u_ref):
    @pl.when(pl.program_id(1) == 0)
    def _():
        _rms_rows_to_bf16(x_ref, g_ref, u_ref)

    u = u_ref[...]
    gate = _dot(u, wg_ref[...])
    up = _dot(u, wu_ref[...])
    o_ref[...] = (gate * _sigmoid(gate) * up).astype(o_ref.dtype)


def _ffn_up(x, g, wg, wu, *, tm=1024, tn=512):
    m, d = x.shape
    n = wg.shape[1]
    return pl.pallas_call(
        _ffn_up_kernel,
        grid=(m // tm, n // tn),
        in_specs=[
            pl.BlockSpec((tm, d), lambda i, j: (i, 0)),
            pl.BlockSpec((1, d), lambda i, j: (0, 0)),
            pl.BlockSpec((d, tn), lambda i, j: (0, j)),
            pl.BlockSpec((d, tn), lambda i, j: (0, j)),
        ],
        out_specs=pl.BlockSpec((tm, tn), lambda i, j: (i, j)),
        out_shape=jax.ShapeDtypeStruct((m, n), BF16),
        scratch_shapes=[pltpu.VMEM((tm, d), BF16)],
        compiler_params=_params(("parallel", "arbitrary")),
        name="ffn_up",
    )(x, g, wg, wu)


def kernel(x, norm1_g, w_in, b_gate, q_norm_a, k_norm_a, sinks_a, q_norm_b, k_norm_b,
           w_branch_a, w_branch_b, w_o, norm2_g, w_ffn_gate, w_ffn_up, w_ffn_down):
    B, S, D = x.shape
    depth = w_in.shape[0]
    assert D == D_MODEL and w_in.shape[2] == IN_COLS
    assert S % MOBA_BLOCK == 0 and S % WINDOW == 0
    M = B * S

    slopes_b = jnp.asarray(np.exp2(-8.0 * np.arange(1, N_H_B + 1, dtype=np.float32) / N_H_B), F32)
    swa_bias = _swa_bias_table()
    bd = _block_diag(V7X_MXU_DIM, HEAD_DIM_A, 1.0 / HEAD_DIM_A)

    h = x.reshape(M, D)
    for l in range(depth):
        z = _in_proj(h, norm1_g[l].reshape(1, D), w_in[l].astype(BF16))
        z3 = z.reshape(B, S, IN_COLS)
        o_a = _swa(z3, sinks_a[l],
                   jnp.tile(q_norm_a[l], N_Q_A).reshape(1, WQ_A),
                   jnp.tile(k_norm_a[l], N_KV_A).reshape(1, WKV_A), bd, swa_bias)
        o_b = _moba(z3, slopes_b, q_norm_b[l].reshape(1, HEAD_DIM_B), k_norm_b[l].reshape(1, HEAD_DIM_B))
        mixed = _merge(o_a.reshape(M, WQ_A), o_b.reshape(M, W_B),
                       w_branch_a[l].astype(BF16), w_branch_b[l].astype(BF16), z,
                       b_gate[l, :D].reshape(1, D), b_gate[l, D:].reshape(1, D))
        h1 = _mm_res(mixed, w_o[l].astype(BF16), h, tm=1024, tn=1024, name="out_proj")
        act = _ffn_up(h1, norm2_g[l].reshape(1, D), w_ffn_gate[l].astype(BF16), w_ffn_up[l].astype(BF16))
        h = _mm_res(act, w_ffn_down[l].astype(BF16), h1, tm=1024, tn=512, name="ffn_down")
    return h.reshape(B, S, D)
```

```python
import functools

import numpy as np
import jax
import jax.numpy as jnp
from jax import lax
from jax.experimental import pallas as pl
from jax.experimental.pallas import tpu as pltpu

F32 = jnp.float32
BF16 = jnp.bfloat16

D_MODEL = 2048
HEAD_DIM_A = 64
N_Q_A = 16
N_KV_A = 4
WINDOW = 128
HEAD_DIM_B = 128
N_H_B = 8
MOBA_BLOCK = 256
MOBA_TOPK = 3
RMS_EPS = 1e-6

WQ_A = N_Q_A * HEAD_DIM_A
WKV_A = N_KV_A * HEAD_DIM_A
W_B = N_H_B * HEAD_DIM_B
COL_QA = 0
COL_KA = COL_QA + WQ_A
COL_VA = COL_KA + WKV_A
COL_QB = COL_VA + WKV_A
COL_KB = COL_QB + W_B
COL_VB = COL_KB + W_B
COL_GA = COL_VB + W_B
COL_GB = COL_GA + D_MODEL
IN_COLS = COL_GB + D_MODEL

V7X_LANES = 128
V7X_MXU_DIM = 256
VMEM_LIMIT_BYTES = 56 * 1024 * 1024
NORM_ROW_CHUNK = 64
NORM_UNROLL = 2
MOBA_HEADS_PER_STEP = V7X_MXU_DIM // HEAD_DIM_B
MOBA_SCORE_LOOKAHEAD = 4
GATE_ROWS = 16
NEG_INF = float("-inf")
LOG2E = 1.4426950408889634


def _params(sem):
    return pltpu.CompilerParams(dimension_semantics=sem, vmem_limit_bytes=VMEM_LIMIT_BYTES)


def _dot(a, b):
    return jnp.dot(a, b, preferred_element_type=F32)


def _dot_nt(a, b):
    return lax.dot_general(a, b, (((1,), (1,)), ((), ())), preferred_element_type=F32)


def _split_bf16(x):
    hi = x.astype(BF16)
    lo = (x - hi.astype(F32)).astype(BF16)
    return hi, lo


def _rms_rows_to_bf16(x_ref, g_ref, u_ref):
    rows = x_ref.shape[0]
    g = g_ref[...]

    def body(c, carry):
        r = pl.multiple_of(c * NORM_ROW_CHUNK, NORM_ROW_CHUNK)
        x = x_ref[pl.ds(r, NORM_ROW_CHUNK), :]
        ms = jnp.mean(x * x, axis=-1, keepdims=True)
        u_ref[pl.ds(r, NORM_ROW_CHUNK), :] = (x * lax.rsqrt(ms + RMS_EPS) * g).astype(BF16)
        return carry

    lax.fori_loop(0, rows // NORM_ROW_CHUNK, body, 0, unroll=NORM_UNROLL)


def _in_proj_kernel(x_ref, g_ref, w_ref, qkv_ref, gates_ref, u_ref, *, n_qkv_tiles):
    j = pl.program_id(1)

    @pl.when(j == 0)
    def _():
        _rms_rows_to_bf16(x_ref, g_ref, u_ref)

    z = _dot(u_ref[...], w_ref[...])

    @pl.when(j < n_qkv_tiles)
    def _():
        qkv_ref[...] = z.astype(qkv_ref.dtype)

    @pl.when(j >= n_qkv_tiles)
    def _():
        gates_ref[...] = z.astype(gates_ref.dtype)


def _in_proj(x, g, w, *, tm=1024, tn=512):
    m, d = x.shape
    n = w.shape[1]
    n_qkv_tiles = COL_GA // tn
    last_qkv = n_qkv_tiles - 1
    return pl.pallas_call(
        functools.partial(_in_proj_kernel, n_qkv_tiles=n_qkv_tiles),
        grid=(m // tm, n // tn),
        in_specs=[
            pl.BlockSpec((tm, d), lambda i, j: (i, 0)),
            pl.BlockSpec((1, d), lambda i, j: (0, 0)),
            pl.BlockSpec((d, tn), lambda i, j: (0, j)),
        ],
        out_specs=[
            pl.BlockSpec((tm, tn), lambda i, j: (i, jnp.minimum(j, last_qkv))),
            pl.BlockSpec((tm, tn), lambda i, j: (i, jnp.maximum(j - n_qkv_tiles, 0))),
        ],
        out_shape=[jax.ShapeDtypeStruct((m, COL_GA), BF16),
                   jax.ShapeDtypeStruct((m, n - COL_GA), BF16)],
        scratch_shapes=[pltpu.VMEM((tm, d), BF16)],
        compiler_params=_params(("parallel", "arbitrary")),
        name="in_proj",
    )(x, g, w)


def _swa_kernel(sinks_ref, q_ref, kc_ref, kp_ref, vc_ref, vp_ref, gq_ref, gk_ref, bd_ref,
                bias_ref, o_ref):
    L = WINDOW
    first = (pl.program_id(1) == 0).astype(jnp.int32)
    bd = bd_ref[...]
    q = q_ref[0].astype(F32)
    k = jnp.concatenate([kp_ref[0], kc_ref[0]], axis=0).astype(F32)
    v = jnp.concatenate([vp_ref[0], vc_ref[0]], axis=0).astype(F32)

    n_qt = WQ_A // V7X_MXU_DIM
    sq = jnp.concatenate([q[:, t * 256:(t + 1) * 256] for t in range(n_qt)] + [k], axis=0)
    sq = sq * sq
    hi, lo = _split_bf16(sq)
    inv = lax.rsqrt(_dot(hi, bd) + _dot(lo, bd) + RMS_EPS)
    gq = gq_ref[...] * (HEAD_DIM_A ** -0.5 * LOG2E)
    qn = [q[:, t * 256:(t + 1) * 256] * inv[t * L:(t + 1) * L] * gq[:, t * 256:(t + 1) * 256]
          for t in range(n_qt)]
    kn = k * inv[n_qt * L:] * gk_ref[...]

    lane = lax.broadcasted_iota(jnp.int32, (1, V7X_LANES), 1)
    left = lane < HEAD_DIM_A
    lo_head = lax.broadcasted_iota(jnp.int32, (1, 2 * L), 1) < L

    kt = [kn[:, t * 128:(t + 1) * 128] for t in range(WKV_A // 128)]
    kt_sw = [pltpu.roll(x, HEAD_DIM_A, axis=1) for x in kt]
    vt = [v[:, t * 128:(t + 1) * 128].T.astype(BF16) for t in range(WKV_A // 128)]

    def q_masked(h):
        tile = h // 2
        qt = qn[tile // 2][:, (tile % 2) * 128:(tile % 2 + 1) * 128]
        keep = left if h % 2 == 0 else jnp.logical_not(left)
        return jnp.where(keep, qt, 0.0).astype(BF16)

    rep = N_Q_A // N_KV_A

    def scores(g, hpar):
        t, par = g // 2, g % 2
        ha, hb = rep * g + hpar, rep * g + hpar + 2
        k_al = kt[t] if par == hpar else kt_sw[t]
        qm = jnp.concatenate([q_masked(ha), q_masked(hb)], axis=0)
        return _dot_nt(k_al.astype(BF16), qm) + bias_ref[first, 2 * g + hpar]

    pairs = [(g, hpar) for g in range(N_KV_A) for hpar in range(2)]
    all_scores = [scores(g, hpar) for g, hpar in pairs]

    out_rows = [None] * N_Q_A
    for (g, hpar), s in zip(pairs, all_scores):
        t, par = g // 2, g % 2
        ha, hb = rep * g + hpar, rep * g + hpar + 2
        sink = jnp.where(lo_head, sinks_ref[ha], sinks_ref[hb]) * LOG2E
        m = jnp.maximum(jnp.max(s, axis=0, keepdims=True), sink)
        e = jnp.exp2(s - m)
        denom = jnp.sum(e, axis=0, keepdims=True) + jnp.exp2(sink - m)
        ot = _dot(vt[t], e.astype(BF16))
        og = ot[par * HEAD_DIM_A:(par + 1) * HEAD_DIM_A, :] * (1.0 / denom)
        out_rows[ha] = og[:, :L]
        out_rows[hb] = og[:, L:]
    o_t = jnp.concatenate(out_rows, axis=0)
    o_ref[0] = o_t.T.astype(o_ref.dtype)


def _swa(z3, sinks, gq_t, gk_t, bd, bias):
    B, S, _ = z3.shape
    L = WINDOW
    nb = S // L
    kblk = COL_KA // WKV_A
    vblk = COL_VA // WKV_A
    prev = lambda n: jnp.maximum(n - 1, 0)
    return pl.pallas_call(
        _swa_kernel,
        grid=(B, nb),
        in_specs=[
            pl.BlockSpec(memory_space=pltpu.SMEM),
            pl.BlockSpec((1, L, WQ_A), lambda b, n: (b, n, 0)),
            pl.BlockSpec((1, L, WKV_A), lambda b, n: (b, n, kblk)),
            pl.BlockSpec((1, L, WKV_A), lambda b, n: (b, prev(n), kblk)),
            pl.BlockSpec((1, L, WKV_A), lambda b, n: (b, n, vblk)),
            pl.BlockSpec((1, L, WKV_A), lambda b, n: (b, prev(n), vblk)),
            pl.BlockSpec((1, WQ_A), lambda b, n: (0, 0)),
            pl.BlockSpec((1, WKV_A), lambda b, n: (0, 0)),
            pl.BlockSpec((V7X_MXU_DIM, V7X_MXU_DIM), lambda b, n: (0, 0)),
            pl.BlockSpec((2, N_Q_A // 2, 2 * L, 2 * L), lambda b, n: (0, 0, 0, 0)),
        ],
        out_specs=pl.BlockSpec((1, L, WQ_A), lambda b, n: (b, n, 0)),
        out_shape=jax.ShapeDtypeStruct((B, S, WQ_A), BF16),
        compiler_params=_params(("parallel", "parallel")),
        name="swa",
    )(sinks, z3, z3, z3, z3, z3, gq_t, gk_t, bd, bias)


def _swa_bias_table():
    L = WINDOW
    rep = N_Q_A // N_KV_A
    slopes = np.exp2(-8.0 * np.arange(1, N_Q_A + 1, dtype=np.float32) / N_Q_A).astype(np.float32)
    kj = np.arange(2 * L)[:, None]
    qi = np.arange(L)[None, :]
    dist = L + qi - kj
    window = (dist >= 0) & (dist < WINDOW)
    table = np.empty((2, N_Q_A // 2, 2 * L, 2 * L), np.float32)
    for first in range(2):
        valid = window & ((kj >= L) if first else True)
        for g in range(N_KV_A):
            for hpar in range(2):
                for a in range(2):
                    h = rep * g + hpar + 2 * a
                    table[first, 2 * g + hpar, :, a * L:(a + 1) * L] = np.where(
                        valid, -(slopes[h] * LOG2E) * dist.astype(np.float32), -np.inf)
    return jnp.asarray(table)


def _block_diag(width, block, value):
    idx = np.arange(width) // block
    return jnp.asarray((idx[:, None] == idx[None, :]).astype(np.float32) * value, dtype=BF16)


def _moba_kernel(slopes_ref, q_ref, k_ref, v_ref, gq_ref, gk_ref, o_ref,
                 kn_ref, vt_ref, kmean_ref, tab_ref):
    BLK = MOBA_BLOCK
    HP = MOBA_HEADS_PER_STEP
    dh = HEAD_DIM_B
    hp = pl.program_id(1)
    i = pl.program_id(2)
    nblk = k_ref.shape[1] // BLK

    @pl.when(i == 0)
    def _prepare():
        kc = lax.broadcasted_iota(jnp.int32, (BLK, BLK), 0)
        qr = lax.broadcasted_iota(jnp.int32, (BLK, BLK), 1)
        rel = (qr - kc).astype(F32)
        for a in range(HP):
            k = k_ref[0, :, a * dh:(a + 1) * dh].astype(F32)
            ms = jnp.mean(k * k, axis=-1, keepdims=True)
            kn = k * lax.rsqrt(ms + RMS_EPS) * gk_ref[...]
            kn_ref[a] = kn.astype(BF16)
            kmean_ref[a] = jnp.zeros((GATE_ROWS, dh), F32)
            kmean_ref[a, 0:nblk, :] = jnp.mean(kn.reshape(nblk, BLK, dh), axis=1)
            vt_ref[a] = v_ref[0, :, a * dh:(a + 1) * dh].astype(F32).T.astype(BF16)
            slope2 = slopes_ref[hp * HP + a] * LOG2E
            tab_ref[a, 0] = -slope2 * rel
            tab_ref[a, 1] = jnp.where(rel >= 0.0, -slope2 * rel, NEG_INF)

    def attend(c):
        qb, block_max_shift = [], []
        for a in range(HP):
            slope2 = slopes_ref[hp * HP + a] * LOG2E
            q = q_ref[0, :, a * dh:(a + 1) * dh].astype(F32)
            ms = jnp.mean(q * q, axis=-1, keepdims=True)
            qs = q * lax.rsqrt(ms + RMS_EPS) * (gq_ref[...] * (dh ** -0.5 * LOG2E))
            qb.append(qs.astype(BF16))

            shifts = [-slope2 * float((c - n) * BLK) for n in range(c)]
            if c > MOBA_TOPK:
                q_hi, q_lo = _split_bf16(qs)
                km_hi, km_lo = _split_bf16(kmean_ref[a])
                gate = _dot_nt(km_hi, q_hi) + _dot_nt(km_lo, q_hi) + _dot_nt(km_hi, q_lo)
                blk = lax.broadcasted_iota(jnp.int32, (GATE_ROWS, 1), 0)
                past = blk < c
                for n in range(c):
                    g_n = gate[n:n + 1, :]
                    beats = ((gate > g_n) | ((gate == g_n) & (blk < n))) & past
                    rank = jnp.sum(jnp.where(beats, 1.0, 0.0), axis=0, keepdims=True)
                    shifts[n] = jnp.where(rank < float(MOBA_TOPK), shifts[n], NEG_INF)
            block_max_shift.append(shifts)

        def scores(a, j):
            s = _dot_nt(kn_ref[a, j * BLK:(j + 1) * BLK, :], qb[a]) + tab_ref[a, 1 if j == c else 0]
            return s, jnp.max(s, axis=0, keepdims=True)

        def probs(s, m):
            e = jnp.exp2(s - m)
            return e.astype(BF16), jnp.sum(e, axis=0, keepdims=True)

        def weighted_values(a, j, p):
            return _dot(vt_ref[a, :, j * BLK:(j + 1) * BLK], p)

        tasks = [(a, j) for j in range(c + 1) for a in range(HP)]
        ahead = {}
        stats = [([], [], []) for _ in range(HP)]
        for t in range(len(tasks) + MOBA_SCORE_LOOKAHEAD):
            if t < len(tasks):
                ahead[t] = scores(*tasks[t])
            d = t - MOBA_SCORE_LOOKAHEAD
            if d >= 0:
                a, j = tasks[d]
                s, m_j = ahead.pop(d)
                p, l_j = probs(s, m_j)
                ms_, ls_, os_ = stats[a]
                os_.append(weighted_values(a, j, p))
                ls_.append(l_j)
                ms_.append(m_j + block_max_shift[a][j] if j < c else m_j)

        for a in range(HP):
            ms_, ls_, os_ = stats[a]
            m = functools.reduce(jnp.maximum, ms_)
            ws = [jnp.exp2(m_j - m) for m_j in ms_]
            l = functools.reduce(jnp.add, [w * l_j for w, l_j in zip(ws, ls_)])
            ot = functools.reduce(jnp.add, [w * o_j for w, o_j in zip(ws, os_)])
            ot = ot * (1.0 / l)
            o_ref[0, :, a * dh:(a + 1) * dh] = ot.T.astype(o_ref.dtype)

    for c in range(nblk):
        pl.when(i == c)(functools.partial(attend, c))


def _moba(z3, slopes, gq, gk):
    B, S, _ = z3.shape
    BLK = MOBA_BLOCK
    HP = MOBA_HEADS_PER_STEP
    nblk = S // BLK
    w = HP * HEAD_DIM_B
    qc, kc, vc = COL_QB // w, COL_KB // w, COL_VB // w
    return pl.pallas_call(
        _moba_kernel,
        grid=(B, N_H_B // HP, nblk),
        in_specs=[
            pl.BlockSpec(memory_space=pltpu.SMEM),
            pl.BlockSpec((1, BLK, w), lambda b, h, i: (b, i, qc + h)),
            pl.BlockSpec((1, S, w), lambda b, h, i: (b, 0, kc + h)),
            pl.BlockSpec((1, S, w), lambda b, h, i: (b, 0, vc + h)),
            pl.BlockSpec((1, HEAD_DIM_B), lambda b, h, i: (0, 0)),
            pl.BlockSpec((1, HEAD_DIM_B), lambda b, h, i: (0, 0)),
        ],
        out_specs=pl.BlockSpec((1, BLK, w), lambda b, h, i: (b, i, h)),
        out_shape=jax.ShapeDtypeStruct((B, S, W_B), BF16),
        scratch_shapes=[
            pltpu.VMEM((HP, S, HEAD_DIM_B), BF16),
            pltpu.VMEM((HP, HEAD_DIM_B, S), BF16),
            pltpu.VMEM((HP, GATE_ROWS, HEAD_DIM_B), F32),
            pltpu.VMEM((HP, 2, BLK, BLK), F32),
        ],
        compiler_params=_params(("parallel", "parallel", "arbitrary")),
        name="moba",
    )(slopes, z3, z3, z3, gq, gk)


def _sigmoid(x):
    return 1.0 / (1.0 + jnp.exp(-x))


def _merge_out_kernel(oa_ref, ob_ref, ga_ref, gb_ref, ba_ref, bb_ref, wa_ref, wb_ref, wo_ref, x_ref,
                      o_ref, mixed_ref, *, chunk):
    oa = oa_ref[...]
    ob = ob_ref[...]
    for c0 in range(0, mixed_ref.shape[1], chunk):
        cs = slice(c0, c0 + chunk)
        a = _dot(oa, wa_ref[:, cs])
        b = _dot(ob, wb_ref[:, cs])
        ga = _sigmoid(ga_ref[:, cs].astype(F32) + ba_ref[:, cs])
        gb = _sigmoid(gb_ref[:, cs].astype(F32) + bb_ref[:, cs])
        mixed_ref[:, cs] = (ga * a + gb * b).astype(mixed_ref.dtype)
    o_ref[...] = x_ref[...] + _dot(mixed_ref[...], wo_ref[...])


def _merge_out(oa, ob, gates, ba, bb, wa, wb, wo, x, *, tm=512, chunk=512):
    m, ka = oa.shape
    kb = ob.shape[1]
    d = wa.shape[1]
    resident = pl.Buffered(1)
    return pl.pallas_call(
        functools.partial(_merge_out_kernel, chunk=chunk),
        grid=(m // tm,),
        in_specs=[
            pl.BlockSpec((tm, ka), lambda i: (i, 0)),
            pl.BlockSpec((tm, kb), lambda i: (i, 0)),
            pl.BlockSpec((tm, d), lambda i: (i, 0)),
            pl.BlockSpec((tm, d), lambda i: (i, 1)),
            pl.BlockSpec((1, d), lambda i: (0, 0)),
            pl.BlockSpec((1, d), lambda i: (0, 0)),
            pl.BlockSpec((ka, d), lambda i: (0, 0), pipeline_mode=resident),
            pl.BlockSpec((kb, d), lambda i: (0, 0), pipeline_mode=resident),
            pl.BlockSpec((d, d), lambda i: (0, 0), pipeline_mode=resident),
            pl.BlockSpec((tm, d), lambda i: (i, 0)),
        ],
        out_specs=pl.BlockSpec((tm, d), lambda i: (i, 0)),
        out_shape=jax.ShapeDtypeStruct((m, d), F32),
        scratch_shapes=[pltpu.VMEM((tm, d), BF16)],
        compiler_params=_params(("parallel",)),
        name="merge_out",
    )(oa, ob, gates, gates, ba, bb, wa, wb, wo, x)


def _mm_res_kernel(a_ref, w_ref, r_ref, o_ref):
    o_ref[...] = r_ref[...] + _dot(a_ref[...], w_ref[...])


def _mm_res(a, w, res, *, tm, tn, name):
    m, k = a.shape
    n = w.shape[1]
    return pl.pallas_call(
        _mm_res_kernel,
        grid=(m // tm, n // tn),
        in_specs=[
            pl.BlockSpec((tm, k), lambda i, j: (i, 0)),
            pl.BlockSpec((k, tn), lambda i, j: (0, j)),
            pl.BlockSpec((tm, tn), lambda i, j: (i, j)),
        ],
        out_specs=pl.BlockSpec((tm, tn), lambda i, j: (i, j)),
        out_shape=jax.ShapeDtypeStruct((m, n), F32),
        compiler_params=_params(("parallel", "parallel")),
        name=name,
    )(a, w, res)


def _ffn_up_kernel(x_ref, g_ref, wg_ref, wu_ref, o_ref, u_ref):
    @pl.when(pl.program_id(1) == 0)
    def _():
        _rms_rows_to_bf16(x_ref, g_ref, u_ref)

    u = u_ref[...]
    gate = _dot(u, wg_ref[...])
    up = _dot(u, wu_ref[...])
    o_ref[...] = (gate * _sigmoid(gate) * up).astype(o_ref.dtype)


def _ffn_up(x, g, wg, wu, *, tm=1024, tn=512):
    m, d = x.shape
    n = wg.shape[1]
    return pl.pallas_call(
        _ffn_up_kernel,
        grid=(m // tm, n // tn),
        in_specs=[
            pl.BlockSpec((tm, d), lambda i, j: (i, 0)),
            pl.BlockSpec((1, d), lambda i, j: (0, 0)),
            pl.BlockSpec((d, tn), lambda i, j: (0, j)),
            pl.BlockSpec((d, tn), lambda i, j: (0, j)),
        ],
        out_specs=pl.BlockSpec((tm, tn), lambda i, j: (i, j)),
        out_shape=jax.ShapeDtypeStruct((m, n), BF16),
        scratch_shapes=[pltpu.VMEM((tm, d), BF16)],
        compiler_params=_params(("parallel", "arbitrary")),
        name="ffn_up",
    )(x, g, wg, wu)


def kernel(x, norm1_g, w_in, b_gate, q_norm_a, k_norm_a, sinks_a, q_norm_b, k_norm_b,
           w_branch_a, w_branch_b, w_o, norm2_g, w_ffn_gate, w_ffn_up, w_ffn_down):
    B, S, D = x.shape
    depth = w_in.shape[0]
    assert D == D_MODEL and w_in.shape[2] == IN_COLS
    assert S % MOBA_BLOCK == 0 and S % WINDOW == 0
    M = B * S

    slopes_b = jnp.asarray(np.exp2(-8.0 * np.arange(1, N_H_B + 1, dtype=np.float32) / N_H_B), F32)
    swa_bias = _swa_bias_table()
    bd = _block_diag(V7X_MXU_DIM, HEAD_DIM_A, 1.0 / HEAD_DIM_A)

    h = x.reshape(M, D)
    for l in range(depth):
        qkv, gates = _in_proj(h, norm1_g[l].reshape(1, D), w_in[l].astype(BF16))
        z3 = qkv.reshape(B, S, COL_GA)
        o_a = _swa(z3, sinks_a[l],
                   jnp.tile(q_norm_a[l], N_Q_A).reshape(1, WQ_A),
                   jnp.tile(k_norm_a[l], N_KV_A).reshape(1, WKV_A), bd, swa_bias)
        o_b = _moba(z3, slopes_b, q_norm_b[l].reshape(1, HEAD_DIM_B), k_norm_b[l].reshape(1, HEAD_DIM_B))
        h1 = _merge_out(o_a.reshape(M, WQ_A), o_b.reshape(M, W_B), gates,
                        b_gate[l, :D].reshape(1, D), b_gate[l, D:].reshape(1, D),
                        w_branch_a[l].astype(BF16), w_branch_b[l].astype(BF16), w_o[l].astype(BF16), h)
        act = _ffn_up(h1, norm2_g[l].reshape(1, D), w_ffn_gate[l].astype(BF16), w_ffn_up[l].astype(BF16))
        h = _mm_res(act, w_ffn_down[l].astype(BF16), h1, tm=1024, tn=512, name="ffn_down")
    return h.reshape(B, S, D)
```

```python
import functools

import numpy as np
import jax
import jax.numpy as jnp
from jax import lax
from jax.experimental import pallas as pl
from jax.experimental.pallas import tpu as pltpu

F32 = jnp.float32
BF16 = jnp.bfloat16

D_MODEL = 2048
HEAD_DIM_A = 64
N_Q_A = 16
N_KV_A = 4
WINDOW = 128
HEAD_DIM_B = 128
N_H_B = 8
MOBA_BLOCK = 256
MOBA_TOPK = 3
RMS_EPS = 1e-6

WQ_A = N_Q_A * HEAD_DIM_A
WKV_A = N_KV_A * HEAD_DIM_A
W_B = N_H_B * HEAD_DIM_B
COL_QA = 0
COL_KA = COL_QA + WQ_A
COL_VA = COL_KA + WKV_A
COL_QB = COL_VA + WKV_A
COL_KB = COL_QB + W_B
COL_VB = COL_KB + W_B
COL_GA = COL_VB + W_B
COL_GB = COL_GA + D_MODEL
IN_COLS = COL_GB + D_MODEL

V7X_LANES = 128
V7X_MXU_DIM = 256
VMEM_LIMIT_BYTES = 56 * 1024 * 1024
NORM_ROW_CHUNK = 64
NORM_UNROLL = 2
MOBA_HEADS_PER_STEP = V7X_MXU_DIM // HEAD_DIM_B
MOBA_QBLOCKS_PER_STEP = 8
MOBA_SCORE_LOOKAHEAD = 4
GATE_ROWS = 16
NEG_INF = float("-inf")
LOG2E = 1.4426950408889634


def _params(sem):
    return pltpu.CompilerParams(dimension_semantics=sem, vmem_limit_bytes=VMEM_LIMIT_BYTES)


def _dot(a, b):
    return jnp.dot(a, b, preferred_element_type=F32)


def _dot_nt(a, b):
    return lax.dot_general(a, b, (((1,), (1,)), ((), ())), preferred_element_type=F32)


def _split_bf16(x):
    hi = x.astype(BF16)
    lo = (x - hi.astype(F32)).astype(BF16)
    return hi, lo


def _rms_rows_to_bf16(x_ref, g_ref, u_ref):
    rows = x_ref.shape[0]
    g = g_ref[...]

    def body(c, carry):
        r = pl.multiple_of(c * NORM_ROW_CHUNK, NORM_ROW_CHUNK)
        x = x_ref[pl.ds(r, NORM_ROW_CHUNK), :]
        ms = jnp.mean(x * x, axis=-1, keepdims=True)
        u_ref[pl.ds(r, NORM_ROW_CHUNK), :] = (x * lax.rsqrt(ms + RMS_EPS) * g).astype(BF16)
        return carry

    lax.fori_loop(0, rows // NORM_ROW_CHUNK, body, 0, unroll=NORM_UNROLL)


def _norm_proj_kernel(x_ref, g_ref, w_ref, o_ref, u_ref):
    @pl.when(pl.program_id(1) == 0)
    def _():
        _rms_rows_to_bf16(x_ref, g_ref, u_ref)

    o_ref[...] = _dot(u_ref[...], w_ref[...]).astype(o_ref.dtype)


def _norm_proj(x, g, w, *, tm, tn):
    m, d = x.shape
    n = w.shape[1]
    return pl.pallas_call(
        _norm_proj_kernel,
        grid=(m // tm, n // tn),
        in_specs=[
            pl.BlockSpec((tm, d), lambda i, j: (i, 0)),
            pl.BlockSpec((1, d), lambda i, j: (0, 0)),
            pl.BlockSpec((d, tn), lambda i, j: (0, j)),
        ],
        out_specs=[
            pl.BlockSpec((tm, tn), lambda i, j: (i, j)),
            pl.BlockSpec((tm, d), lambda i, j: (i, 0)),
        ],
        out_shape=[jax.ShapeDtypeStruct((m, n), BF16), jax.ShapeDtypeStruct((m, d), BF16)],
        compiler_params=_params(("parallel", "arbitrary")),
        name="qkv_proj",
    )(x, g, w)


def _mm_kernel(a_ref, w_ref, o_ref):
    o_ref[...] = _dot(a_ref[...], w_ref[...]).astype(o_ref.dtype)


def _mm(a, w, *, tm, tn, name):
    m, k = a.shape
    n = w.shape[1]
    return pl.pallas_call(
        _mm_kernel,
        grid=(m // tm, n // tn),
        in_specs=[
            pl.BlockSpec((tm, k), lambda i, j: (i, 0)),
            pl.BlockSpec((k, tn), lambda i, j: (0, j)),
        ],
        out_specs=pl.BlockSpec((tm, tn), lambda i, j: (i, j)),
        out_shape=jax.ShapeDtypeStruct((m, n), BF16),
        compiler_params=_params(("parallel", "parallel")),
        name=name,
    )(a, w)


def _swa_kernel(sinks_ref, q_ref, kc_ref, kp_ref, vc_ref, vp_ref, gq_ref, gk_ref, bd_ref,
                bias_ref, o_ref):
    L = WINDOW
    first = (pl.program_id(1) == 0).astype(jnp.int32)
    bd = bd_ref[...]
    q = q_ref[0].astype(F32)
    k = jnp.concatenate([kp_ref[0], kc_ref[0]], axis=0).astype(F32)
    v = jnp.concatenate([vp_ref[0], vc_ref[0]], axis=0).astype(F32)

    n_qt = WQ_A // V7X_MXU_DIM
    sq = jnp.concatenate([q[:, t * 256:(t + 1) * 256] for t in range(n_qt)] + [k], axis=0)
    sq = sq * sq
    hi, lo = _split_bf16(sq)
    inv = lax.rsqrt(_dot(hi, bd) + _dot(lo, bd) + RMS_EPS)
    gq = gq_ref[...] * (HEAD_DIM_A ** -0.5 * LOG2E)
    qn = [q[:, t * 256:(t + 1) * 256] * inv[t * L:(t + 1) * L] * gq[:, t * 256:(t + 1) * 256]
          for t in range(n_qt)]
    kn = k * inv[n_qt * L:] * gk_ref[...]

    lane = lax.broadcasted_iota(jnp.int32, (1, V7X_LANES), 1)
    left = lane < HEAD_DIM_A
    lo_head = lax.broadcasted_iota(jnp.int32, (1, 2 * L), 1) < L

    kt = [kn[:, t * 128:(t + 1) * 128] for t in range(WKV_A // 128)]
    kt_sw = [pltpu.roll(x, HEAD_DIM_A, axis=1) for x in kt]
    vt = [v[:, t * 128:(t + 1) * 128].T.astype(BF16) for t in range(WKV_A // 128)]

    def q_masked(h):
        tile = h // 2
        qt = qn[tile // 2][:, (tile % 2) * 128:(tile % 2 + 1) * 128]
        keep = left if h % 2 == 0 else jnp.logical_not(left)
        return jnp.where(keep, qt, 0.0).astype(BF16)

    rep = N_Q_A // N_KV_A

    def scores(g, hpar):
        t, par = g // 2, g % 2
        ha, hb = rep * g + hpar, rep * g + hpar + 2
        k_al = kt[t] if par == hpar else kt_sw[t]
        qm = jnp.concatenate([q_masked(ha), q_masked(hb)], axis=0)
        return _dot_nt(k_al.astype(BF16), qm) + bias_ref[first, 2 * g + hpar]

    pairs = [(g, hpar) for g in range(N_KV_A) for hpar in range(2)]
    all_scores = [scores(g, hpar) for g, hpar in pairs]

    out_rows = [None] * N_Q_A
    for (g, hpar), s in zip(pairs, all_scores):
        t, par = g // 2, g % 2
        ha, hb = rep * g + hpar, rep * g + hpar + 2
        sink = jnp.where(lo_head, sinks_ref[ha], sinks_ref[hb]) * LOG2E
        m = jnp.maximum(jnp.max(s, axis=0, keepdims=True), sink)
        e = jnp.exp2(s - m)
        denom = jnp.sum(e, axis=0, keepdims=True) + jnp.exp2(sink - m)
        ot = _dot(vt[t], e.astype(BF16))
        og = ot[par * HEAD_DIM_A:(par + 1) * HEAD_DIM_A, :] * (1.0 / denom)
        out_rows[ha] = og[:, :L]
        out_rows[hb] = og[:, L:]
    o_t = jnp.concatenate(out_rows, axis=0)
    o_ref[0] = o_t.T.astype(o_ref.dtype)


def _swa(z3, sinks, gq_t, gk_t, bd, bias):
    B, S, _ = z3.shape
    L = WINDOW
    nb = S // L
    kblk = COL_KA // WKV_A
    vblk = COL_VA // WKV_A
    prev = lambda n: jnp.maximum(n - 1, 0)
    return pl.pallas_call(
        _swa_kernel,
        grid=(B, nb),
        in_specs=[
            pl.BlockSpec(memory_space=pltpu.SMEM),
            pl.BlockSpec((1, L, WQ_A), lambda b, n: (b, n, 0)),
            pl.BlockSpec((1, L, WKV_A), lambda b, n: (b, n, kblk)),
            pl.BlockSpec((1, L, WKV_A), lambda b, n: (b, prev(n), kblk)),
            pl.BlockSpec((1, L, WKV_A), lambda b, n: (b, n, vblk)),
            pl.BlockSpec((1, L, WKV_A), lambda b, n: (b, prev(n), vblk)),
            pl.BlockSpec((1, WQ_A), lambda b, n: (0, 0)),
            pl.BlockSpec((1, WKV_A), lambda b, n: (0, 0)),
            pl.BlockSpec((V7X_MXU_DIM, V7X_MXU_DIM), lambda b, n: (0, 0)),
            pl.BlockSpec((2, N_Q_A // 2, 2 * L, 2 * L), lambda b, n: (0, 0, 0, 0)),
        ],
        out_specs=pl.BlockSpec((1, L, WQ_A), lambda b, n: (b, n, 0)),
        out_shape=jax.ShapeDtypeStruct((B, S, WQ_A), BF16),
        compiler_params=_params(("parallel", "parallel")),
        name="swa",
    )(sinks, z3, z3, z3, z3, z3, gq_t, gk_t, bd, bias)


def _swa_bias_table():
    L = WINDOW
    rep = N_Q_A // N_KV_A
    slopes = np.exp2(-8.0 * np.arange(1, N_Q_A + 1, dtype=np.float32) / N_Q_A).astype(np.float32)
    kj = np.arange(2 * L)[:, None]
    qi = np.arange(L)[None, :]
    dist = L + qi - kj
    window = (dist >= 0) & (dist < WINDOW)
    table = np.empty((2, N_Q_A // 2, 2 * L, 2 * L), np.float32)
    for first in range(2):
        valid = window & ((kj >= L) if first else True)
        for g in range(N_KV_A):
            for hpar in range(2):
                for a in range(2):
                    h = rep * g + hpar + 2 * a
                    table[first, 2 * g + hpar, :, a * L:(a + 1) * L] = np.where(
                        valid, -(slopes[h] * LOG2E) * dist.astype(np.float32), -np.inf)
    return jnp.asarray(table)


def _block_diag(width, block, value):
    idx = np.arange(width) // block
    return jnp.asarray((idx[:, None] == idx[None, :]).astype(np.float32) * value, dtype=BF16)


def _moba_kernel(slopes_ref, q_ref, k_ref, v_ref, gq_ref, gk_ref, o_ref,
                 kn_ref, vt_ref, qb_ref, gate_ref, tab_ref):
    BLK = MOBA_BLOCK
    HP = MOBA_HEADS_PER_STEP
    G = MOBA_QBLOCKS_PER_STEP
    dh = HEAD_DIM_B
    hp = pl.program_id(1)
    grp = pl.program_id(2)
    nblk = k_ref.shape[1] // BLK

    @pl.when(grp == 0)
    def _prepare():
        kc = lax.broadcasted_iota(jnp.int32, (BLK, BLK), 0)
        qr = lax.broadcasted_iota(jnp.int32, (BLK, BLK), 1)
        rel = (qr - kc).astype(F32)
        for a in range(HP):
            k = k_ref[0, :, a * dh:(a + 1) * dh].astype(F32)
            ms = jnp.mean(k * k, axis=-1, keepdims=True)
            kn = k * lax.rsqrt(ms + RMS_EPS) * gk_ref[...]
            kn_ref[a] = kn.astype(BF16)
            kmean = jnp.concatenate(
                [jnp.mean(kn.reshape(nblk, BLK, dh), axis=1), jnp.zeros((GATE_ROWS - nblk, dh), F32)], axis=0)
            vt_ref[a] = v_ref[0, :, a * dh:(a + 1) * dh].astype(F32).T.astype(BF16)
            slope2 = slopes_ref[hp * HP + a] * LOG2E
            tab_ref[a, 0] = -slope2 * rel
            tab_ref[a, 1] = jnp.where(rel >= 0.0, -slope2 * rel, NEG_INF)

            q = q_ref[0, :, a * dh:(a + 1) * dh].astype(F32)
            ms = jnp.mean(q * q, axis=-1, keepdims=True)
            qs = q * lax.rsqrt(ms + RMS_EPS) * (gq_ref[...] * (dh ** -0.5 * LOG2E))
            q_hi, q_lo = _split_bf16(qs)
            km_hi, km_lo = _split_bf16(kmean)
            qb_ref[a] = q_hi
            gate_ref[a] = _dot_nt(km_hi, q_hi) + _dot_nt(km_lo, q_hi) + _dot_nt(km_hi, q_lo)

    def block_max_shifts(c, a):
        slope2 = slopes_ref[hp * HP + a] * LOG2E
        shifts = [-slope2 * float((c - n) * BLK) for n in range(c)]
        if c > MOBA_TOPK:
            gate = gate_ref[a, :, c * BLK:(c + 1) * BLK]
            blk = lax.broadcasted_iota(jnp.int32, (GATE_ROWS, 1), 0)
            past = blk < c
            for n in range(c):
                g_n = gate[n:n + 1, :]
                beats = ((gate > g_n) | ((gate == g_n) & (blk < n))) & past
                rank = jnp.sum(jnp.where(beats, 1.0, 0.0), axis=0, keepdims=True)
                shifts[n] = jnp.where(rank < float(MOBA_TOPK), shifts[n], NEG_INF)
        return shifts

    def scores(c, a, j):
        s = _dot_nt(kn_ref[a, j * BLK:(j + 1) * BLK, :], qb_ref[a, c * BLK:(c + 1) * BLK, :])
        s = s + tab_ref[a, 1 if j == c else 0]
        return s, jnp.max(s, axis=0, keepdims=True)

    def probs(s, m):
        e = jnp.exp2(s - m)
        return e.astype(BF16), jnp.sum(e, axis=0, keepdims=True)

    def weighted_values(a, j, p):
        return _dot(vt_ref[a, :, j * BLK:(j + 1) * BLK], p)

    def merge_blocks(c, a, ms_, ls_, os_, row0):
        shifts = block_max_shifts(c, a)
        ms_ = [m_j + sh for m_j, sh in zip(ms_, shifts)] + ms_[c:]
        m = functools.reduce(jnp.maximum, ms_)
        ws = [jnp.exp2(m_j - m) for m_j in ms_]
        l = functools.reduce(jnp.add, [w * l_j for w, l_j in zip(ws, ls_)])
        ot = functools.reduce(jnp.add, [w * o_j for w, o_j in zip(ws, os_)])
        ot = ot * (1.0 / l)
        o_ref[0, row0:row0 + BLK, a * dh:(a + 1) * dh] = ot.T.astype(o_ref.dtype)

    def attend(first_block):
        tasks = [(c, a, j) for c in range(first_block, first_block + G) for j in range(c + 1) for a in range(HP)]
        ahead = {}
        stats = {}
        for t in range(len(tasks) + MOBA_SCORE_LOOKAHEAD):
            if t < len(tasks):
                ahead[t] = scores(*tasks[t])
            d = t - MOBA_SCORE_LOOKAHEAD
            if d >= 0:
                c, a, j = tasks[d]
                s, m_j = ahead.pop(d)
                p, l_j = probs(s, m_j)
                ms_, ls_, os_ = stats.setdefault((c, a), ([], [], []))
                os_.append(weighted_values(a, j, p))
                ls_.append(l_j)
                ms_.append(m_j)
                if j == c:
                    merge_blocks(c, a, *stats.pop((c, a)), (c - first_block) * BLK)

    if G == nblk:
        attend(0)
    else:
        for g in range(nblk // G):
            pl.when(grp == g)(functools.partial(attend, g * G))


def _moba(z3, slopes, gq, gk):
    B, S, _ = z3.shape
    BLK = MOBA_BLOCK
    HP = MOBA_HEADS_PER_STEP
    G = MOBA_QBLOCKS_PER_STEP
    nblk = S // BLK
    w = HP * HEAD_DIM_B
    qc, kc, vc = COL_QB // w, COL_KB // w, COL_VB // w
    return pl.pallas_call(
        _moba_kernel,
        grid=(B, N_H_B // HP, nblk // G),
        in_specs=[
            pl.BlockSpec(memory_space=pltpu.SMEM),
            pl.BlockSpec((1, S, w), lambda b, h, g: (b, 0, qc + h)),
            pl.BlockSpec((1, S, w), lambda b, h, g: (b, 0, kc + h)),
            pl.BlockSpec((1, S, w), lambda b, h, g: (b, 0, vc + h)),
            pl.BlockSpec((1, HEAD_DIM_B), lambda b, h, g: (0, 0)),
            pl.BlockSpec((1, HEAD_DIM_B), lambda b, h, g: (0, 0)),
        ],
        out_specs=pl.BlockSpec((1, G * BLK, w), lambda b, h, g: (b, g, h)),
        out_shape=jax.ShapeDtypeStruct((B, S, W_B), BF16),
        scratch_shapes=[
            pltpu.VMEM((HP, S, HEAD_DIM_B), BF16),
            pltpu.VMEM((HP, HEAD_DIM_B, S), BF16),
            pltpu.VMEM((HP, S, HEAD_DIM_B), BF16),
            pltpu.VMEM((HP, GATE_ROWS, S), F32),
            pltpu.VMEM((HP, 2, BLK, BLK), F32),
        ],
        compiler_params=_params(("parallel", "parallel", "arbitrary")),
        name="moba",
    )(slopes, z3, z3, z3, gq, gk)


def _sigmoid(x):
    return 1.0 / (1.0 + jnp.exp(-x))


def _merge_out_kernel(oa_ref, ob_ref, ga_ref, gb_ref, ba_ref, bb_ref, wa_ref, wb_ref, wo_ref, x_ref,
                      o_ref, mixed_ref, *, chunk):
    oa = oa_ref[...]
    ob = ob_ref[...]
    for c0 in range(0, mixed_ref.shape[1], chunk):
        cs = slice(c0, c0 + chunk)
        a = _dot(oa, wa_ref[:, cs])
        b = _dot(ob, wb_ref[:, cs])
        ga = _sigmoid(ga_ref[:, cs].astype(F32) + ba_ref[:, cs])
        gb = _sigmoid(gb_ref[:, cs].astype(F32) + bb_ref[:, cs])
        mixed_ref[:, cs] = (ga * a + gb * b).astype(mixed_ref.dtype)
    o_ref[...] = x_ref[...] + _dot(mixed_ref[...], wo_ref[...])


def _merge_out(oa, ob, gates, ba, bb, wa, wb, wo, x, *, tm=512, chunk=512):
    m, ka = oa.shape
    kb = ob.shape[1]
    d = wa.shape[1]
    resident = pl.Buffered(1)
    return pl.pallas_call(
        functools.partial(_merge_out_kernel, chunk=chunk),
        grid=(m // tm,),
        in_specs=[
            pl.BlockSpec((tm, ka), lambda i: (i, 0)),
            pl.BlockSpec((tm, kb), lambda i: (i, 0)),
            pl.BlockSpec((tm, d), lambda i: (i, 0)),
            pl.BlockSpec((tm, d), lambda i: (i, 1)),
            pl.BlockSpec((1, d), lambda i: (0, 0)),
            pl.BlockSpec((1, d), lambda i: (0, 0)),
            pl.BlockSpec((ka, d), lambda i: (0, 0), pipeline_mode=resident),
            pl.BlockSpec((kb, d), lambda i: (0, 0), pipeline_mode=resident),
            pl.BlockSpec((d, d), lambda i: (0, 0), pipeline_mode=resident),
            pl.BlockSpec((tm, d), lambda i: (i, 0)),
        ],
        out_specs=pl.BlockSpec((tm, d), lambda i: (i, 0)),
        out_shape=jax.ShapeDtypeStruct((m, d), F32),
        scratch_shapes=[pltpu.VMEM((tm, d), BF16)],
        compiler_params=_params(("parallel",)),
        name="merge_out",
    )(oa, ob, gates, gates, ba, bb, wa, wb, wo, x)


def _mm_res_kernel(a_ref, w_ref, r_ref, o_ref):
    o_ref[...] = r_ref[...] + _dot(a_ref[...], w_ref[...])


def _mm_res(a, w, res, *, tm, tn, name):
    m, k = a.shape
    n = w.shape[1]
    return pl.pallas_call(
        _mm_res_kernel,
        grid=(m // tm, n // tn),
        in_specs=[
            pl.BlockSpec((tm, k), lambda i, j: (i, 0)),
            pl.BlockSpec((k, tn), lambda i, j: (0, j)),
            pl.BlockSpec((tm, tn), lambda i, j: (i, j)),
        ],
        out_specs=pl.BlockSpec((tm, tn), lambda i, j: (i, j)),
        out_shape=jax.ShapeDtypeStruct((m, n), F32),
        compiler_params=_params(("parallel", "parallel")),
        name=name,
    )(a, w, res)


def _ffn_up_kernel(x_ref, g_ref, wg_ref, wu_ref, o_ref, u_ref):
    @pl.when(pl.program_id(1) == 0)
    def _():
        _rms_rows_to_bf16(x_ref, g_ref, u_ref)

    u = u_ref[...]
    gate = _dot(u, wg_ref[...])
    up = _dot(u, wu_ref[...])
    o_ref[...] = (gate * _sigmoid(gate) * up).astype(o_ref.dtype)


def _ffn_up(x, g, wg, wu, *, tm=1024, tn=512):
    m, d = x.shape
    n = wg.shape[1]
    return pl.pallas_call(
        _ffn_up_kernel,
        grid=(m // tm, n // tn),
        in_specs=[
            pl.BlockSpec((tm, d), lambda i, j: (i, 0)),
            pl.BlockSpec((1, d), lambda i, j: (0, 0)),
            pl.BlockSpec((d, tn), lambda i, j: (0, j)),
            pl.BlockSpec((d, tn), lambda i, j: (0, j)),
        ],
        out_specs=pl.BlockSpec((tm, tn), lambda i, j: (i, j)),
        out_shape=jax.ShapeDtypeStruct((m, n), BF16),
        scratch_shapes=[pltpu.VMEM((tm, d), BF16)],
        compiler_params=_params(("parallel", "arbitrary")),
        name="ffn_up",
    )(x, g, wg, wu)


def kernel(x, norm1_g, w_in, b_gate, q_norm_a, k_norm_a, sinks_a, q_norm_b, k_norm_b,
           w_branch_a, w_branch_b, w_o, norm2_g, w_ffn_gate, w_ffn_up, w_ffn_down):
    B, S, D = x.shape
    depth = w_in.shape[0]
    assert D == D_MODEL and w_in.shape[2] == IN_COLS
    assert S % MOBA_BLOCK == 0 and S % WINDOW == 0
    M = B * S

    slopes_b = jnp.asarray(np.exp2(-8.0 * np.arange(1, N_H_B + 1, dtype=np.float32) / N_H_B), F32)
    swa_bias = _swa_bias_table()
    bd = _block_diag(V7X_MXU_DIM, HEAD_DIM_A, 1.0 / HEAD_DIM_A)

    h = x.reshape(M, D)
    for l in range(depth):
        qkv, u = _norm_proj(h, norm1_g[l].reshape(1, D), w_in[l, :, :COL_GA].astype(BF16),
                            tm=1024, tn=COL_GA // 2)
        gates = _mm(u, w_in[l, :, COL_GA:].astype(BF16), tm=1024, tn=D, name="gate_proj")
        z3 = qkv.reshape(B, S, COL_GA)
        o_a = _swa(z3, sinks_a[l],
                   jnp.tile(q_norm_a[l], N_Q_A).reshape(1, WQ_A),
                   jnp.tile(k_norm_a[l], N_KV_A).reshape(1, WKV_A), bd, swa_bias)
        o_b = _moba(z3, slopes_b, q_norm_b[l].reshape(1, HEAD_DIM_B), k_norm_b[l].reshape(1, HEAD_DIM_B))
        h1 = _merge_out(o_a.reshape(M, WQ_A), o_b.reshape(M, W_B), gates,
                        b_gate[l, :D].reshape(1, D), b_gate[l, D:].reshape(1, D),
                        w_branch_a[l].astype(BF16), w_branch_b[l].astype(BF16), w_o[l].astype(BF16), h)
        act = _ffn_up(h1, norm2_g[l].reshape(1, D), w_ffn_gate[l].astype(BF16), w_ffn_up[l].astype(BF16))
        h = _mm_res(act, w_ffn_down[l].astype(BF16), h1, tm=1024, tn=512, name="ffn_down")
    return h.reshape(B, S, D)
```

```python
import functools

import numpy as np
import jax
import jax.numpy as jnp
from jax import lax
from jax.experimental import pallas as pl
from jax.experimental.pallas import tpu as pltpu

F32 = jnp.float32
BF16 = jnp.bfloat16

D_MODEL = 2048
HEAD_DIM_A = 64
N_Q_A = 16
N_KV_A = 4
WINDOW = 128
HEAD_DIM_B = 128
N_H_B = 8
MOBA_BLOCK = 256
MOBA_TOPK = 3
RMS_EPS = 1e-6

WQ_A = N_Q_A * HEAD_DIM_A
WKV_A = N_KV_A * HEAD_DIM_A
W_B = N_H_B * HEAD_DIM_B
COL_QA = 0
COL_KA = COL_QA + WQ_A
COL_VA = COL_KA + WKV_A
COL_QB = COL_VA + WKV_A
COL_KB = COL_QB + W_B
COL_VB = COL_KB + W_B
COL_GA = COL_VB + W_B
COL_GB = COL_GA + D_MODEL
IN_COLS = COL_GB + D_MODEL

V7X_LANES = 128
V7X_MXU_DIM = 256
VMEM_LIMIT_BYTES = 56 * 1024 * 1024
NORM_ROW_CHUNK = 64
NORM_UNROLL = 2
BF16_SUBLANE_TILE = 16
MOBA_HEADS_PER_STEP = V7X_MXU_DIM // HEAD_DIM_B
MOBA_QBLOCKS_PER_STEP = 8
MOBA_SCORE_LOOKAHEAD = 4
GATE_ROWS = 16
NEG_INF = float("-inf")
LOG2E = 1.4426950408889634


def _params(sem):
    return pltpu.CompilerParams(dimension_semantics=sem, vmem_limit_bytes=VMEM_LIMIT_BYTES)


def _dot(a, b):
    return jnp.dot(a, b, preferred_element_type=F32)


def _dot_nt(a, b):
    return lax.dot_general(a, b, (((1,), (1,)), ((), ())), preferred_element_type=F32)


def _split_bf16(x):
    hi = x.astype(BF16)
    lo = (x - hi.astype(F32)).astype(BF16)
    return hi, lo


def _rms_rows_to_bf16(x_ref, g_ref, u_ref):
    rows = x_ref.shape[0]
    g = g_ref[...]

    def body(c, carry):
        r = pl.multiple_of(c * NORM_ROW_CHUNK, NORM_ROW_CHUNK)
        x = x_ref[pl.ds(r, NORM_ROW_CHUNK), :]
        ms = jnp.mean(x * x, axis=-1, keepdims=True)
        u_ref[pl.ds(r, NORM_ROW_CHUNK), :] = (x * lax.rsqrt(ms + RMS_EPS) * g).astype(BF16)
        return carry

    lax.fori_loop(0, rows // NORM_ROW_CHUNK, body, 0, unroll=NORM_UNROLL)


def _norm_proj_kernel(x_ref, g_ref, w_ref, o_ref, u_ref):
    @pl.when(pl.program_id(1) == 0)
    def _():
        _rms_rows_to_bf16(x_ref, g_ref, u_ref)

    o_ref[...] = _dot(u_ref[...], w_ref[...]).astype(o_ref.dtype)


def _norm_proj(x, g, w, *, tm, tn):
    m, d = x.shape
    n = w.shape[1]
    return pl.pallas_call(
        _norm_proj_kernel,
        grid=(m // tm, n // tn),
        in_specs=[
            pl.BlockSpec((tm, d), lambda i, j: (i, 0)),
            pl.BlockSpec((1, d), lambda i, j: (0, 0)),
            pl.BlockSpec((d, tn), lambda i, j: (0, j)),
        ],
        out_specs=[
            pl.BlockSpec((tm, tn), lambda i, j: (i, j)),
            pl.BlockSpec((tm, d), lambda i, j: (i, 0)),
        ],
        out_shape=[jax.ShapeDtypeStruct((m, n), BF16), jax.ShapeDtypeStruct((m, d), BF16)],
        compiler_params=_params(("parallel", "arbitrary")),
        name="qkv_proj",
    )(x, g, w)


def _side_cast_specs(arrays, n_steps, step_index):
    specs, shapes = [], []
    for w in arrays:
        rows, cols = w.shape
        chunk = rows // n_steps
        assert chunk * n_steps == rows and chunk % BF16_SUBLANE_TILE == 0, (w.shape, n_steps)
        specs.append(pl.BlockSpec((chunk, cols), lambda *g: (step_index(*g), 0)))
        shapes.append(jax.ShapeDtypeStruct(w.shape, BF16))
    return specs, shapes


def _side_cast(refs):
    n = len(refs) // 2
    for src, dst in zip(refs[:n], refs[n:]):
        dst[...] = src[...].astype(dst.dtype)


def _gate_proj_kernel(u_ref, w_ref, *refs):
    n_side = (len(refs) - 1) // 2
    o_ref = refs[n_side]
    o_ref[...] = _dot(u_ref[...], w_ref[...].astype(BF16)).astype(o_ref.dtype)
    _side_cast(refs[:n_side] + refs[n_side + 1:])


def _gate_proj(u, w_all, col0, n, side, *, tm=2048, tn=512):
    m, k = u.shape
    gi, gj = m // tm, n // tn
    assert col0 % tn == 0
    side_specs, side_shapes = _side_cast_specs(side, gi * gj, lambda i, j: i * gj + j)
    outs = pl.pallas_call(
        _gate_proj_kernel,
        grid=(gi, gj),
        in_specs=[
            pl.BlockSpec((tm, k), lambda i, j: (i, 0)),
            pl.BlockSpec((k, tn), lambda i, j: (0, col0 // tn + j)),
        ] + side_specs,
        out_specs=[pl.BlockSpec((tm, tn), lambda i, j: (i, j))] + side_specs,
        out_shape=[jax.ShapeDtypeStruct((m, n), BF16)] + side_shapes,
        compiler_params=_params(("arbitrary", "arbitrary")),
        name="gate_proj",
    )(u, w_all, *side)
    return outs[0], outs[1:]


def _swa_kernel(sinks_ref, q_ref, kc_ref, kp_ref, vc_ref, vp_ref, gq_ref, gk_ref, bd_ref,
                bias_ref, *refs):
    n_side = (len(refs) - 1) // 2
    o_ref = refs[n_side]
    _side_cast(refs[:n_side] + refs[n_side + 1:])
    L = WINDOW
    first = (pl.program_id(1) == 0).astype(jnp.int32)
    bd = bd_ref[...]
    q = q_ref[0].astype(F32)
    k = jnp.concatenate([kp_ref[0], kc_ref[0]], axis=0).astype(F32)
    v = jnp.concatenate([vp_ref[0], vc_ref[0]], axis=0).astype(F32)

    n_qt = WQ_A // V7X_MXU_DIM
    sq = jnp.concatenate([q[:, t * 256:(t + 1) * 256] for t in range(n_qt)] + [k], axis=0)
    sq = sq * sq
    hi, lo = _split_bf16(sq)
    inv = lax.rsqrt(_dot(hi, bd) + _dot(lo, bd) + RMS_EPS)
    gq = gq_ref[...] * (HEAD_DIM_A ** -0.5 * LOG2E)
    qn = [q[:, t * 256:(t + 1) * 256] * inv[t * L:(t + 1) * L] * gq[:, t * 256:(t + 1) * 256]
          for t in range(n_qt)]
    kn = k * inv[n_qt * L:] * gk_ref[...]

    lane = lax.broadcasted_iota(jnp.int32, (1, V7X_LANES), 1)
    left = lane < HEAD_DIM_A
    lo_head = lax.broadcasted_iota(jnp.int32, (1, 2 * L), 1) < L

    kt = [kn[:, t * 128:(t + 1) * 128] for t in range(WKV_A // 128)]
    kt_sw = [pltpu.roll(x, HEAD_DIM_A, axis=1) for x in kt]
    vt = [v[:, t * 128:(t + 1) * 128].T.astype(BF16) for t in range(WKV_A // 128)]

    def q_masked(h):
        tile = h // 2
        qt = qn[tile // 2][:, (tile % 2) * 128:(tile % 2 + 1) * 128]
        keep = left if h % 2 == 0 else jnp.logical_not(left)
        return jnp.where(keep, qt, 0.0).astype(BF16)

    rep = N_Q_A // N_KV_A

    def scores(g, hpar):
        t, par = g // 2, g % 2
        ha, hb = rep * g + hpar, rep * g + hpar + 2
        k_al = kt[t] if par == hpar else kt_sw[t]
        qm = jnp.concatenate([q_masked(ha), q_masked(hb)], axis=0)
        return _dot_nt(k_al.astype(BF16), qm) + bias_ref[first, 2 * g + hpar]

    pairs = [(g, hpar) for g in range(N_KV_A) for hpar in range(2)]
    all_scores = [scores(g, hpar) for g, hpar in pairs]

    out_rows = [None] * N_Q_A
    for (g, hpar), s in zip(pairs, all_scores):
        t, par = g // 2, g % 2
        ha, hb = rep * g + hpar, rep * g + hpar + 2
        sink = jnp.where(lo_head, sinks_ref[ha], sinks_ref[hb]) * LOG2E
        m = jnp.maximum(jnp.max(s, axis=0, keepdims=True), sink)
        e = jnp.exp2(s - m)
        denom = jnp.sum(e, axis=0, keepdims=True) + jnp.exp2(sink - m)
        ot = _dot(vt[t], e.astype(BF16))
        og = ot[par * HEAD_DIM_A:(par + 1) * HEAD_DIM_A, :] * (1.0 / denom)
        out_rows[ha] = og[:, :L]
        out_rows[hb] = og[:, L:]
    o_t = jnp.concatenate(out_rows, axis=0)
    o_ref[0] = o_t.T.astype(o_ref.dtype)


def _swa(z3, sinks, gq_t, gk_t, bd, bias, side):
    B, S, _ = z3.shape
    L = WINDOW
    nb = S // L
    kblk = COL_KA // WKV_A
    vblk = COL_VA // WKV_A
    prev = lambda n: jnp.maximum(n - 1, 0)
    side_specs, side_shapes = _side_cast_specs(side, B * nb, lambda b, n: b * nb + n)
    outs = pl.pallas_call(
        _swa_kernel,
        grid=(B, nb),
        in_specs=[
            pl.BlockSpec(memory_space=pltpu.SMEM),
            pl.BlockSpec((1, L, WQ_A), lambda b, n: (b, n, 0)),
            pl.BlockSpec((1, L, WKV_A), lambda b, n: (b, n, kblk)),
            pl.BlockSpec((1, L, WKV_A), lambda b, n: (b, prev(n), kblk)),
            pl.BlockSpec((1, L, WKV_A), lambda b, n: (b, n, vblk)),
            pl.BlockSpec((1, L, WKV_A), lambda b, n: (b, prev(n), vblk)),
            pl.BlockSpec((1, WQ_A), lambda b, n: (0, 0)),
            pl.BlockSpec((1, WKV_A), lambda b, n: (0, 0)),
            pl.BlockSpec((V7X_MXU_DIM, V7X_MXU_DIM), lambda b, n: (0, 0)),
            pl.BlockSpec((2, N_Q_A // 2, 2 * L, 2 * L), lambda b, n: (0, 0, 0, 0)),
        ] + side_specs,
        out_specs=[pl.BlockSpec((1, L, WQ_A), lambda b, n: (b, n, 0))] + side_specs,
        out_shape=[jax.ShapeDtypeStruct((B, S, WQ_A), BF16)] + side_shapes,
        compiler_params=_params(("arbitrary", "arbitrary")),
        name="swa",
    )(sinks, z3, z3, z3, z3, z3, gq_t, gk_t, bd, bias, *side)
    return outs[0], outs[1:]


def _swa_bias_table():
    L = WINDOW
    rep = N_Q_A // N_KV_A
    slopes = np.exp2(-8.0 * np.arange(1, N_Q_A + 1, dtype=np.float32) / N_Q_A).astype(np.float32)
    kj = np.arange(2 * L)[:, None]
    qi = np.arange(L)[None, :]
    dist = L + qi - kj
    window = (dist >= 0) & (dist < WINDOW)
    table = np.empty((2, N_Q_A // 2, 2 * L, 2 * L), np.float32)
    for first in range(2):
        valid = window & ((kj >= L) if first else True)
        for g in range(N_KV_A):
            for hpar in range(2):
                for a in range(2):
                    h = rep * g + hpar + 2 * a
                    table[first, 2 * g + hpar, :, a * L:(a + 1) * L] = np.where(
                        valid, -(slopes[h] * LOG2E) * dist.astype(np.float32), -np.inf)
    return jnp.asarray(table)


def _block_diag(width, block, value):
    idx = np.arange(width) // block
    return jnp.asarray((idx[:, None] == idx[None, :]).astype(np.float32) * value, dtype=BF16)


def _moba_kernel(slopes_ref, q_ref, k_ref, v_ref, gq_ref, gk_ref, *refs):
    n_side = (len(refs) - 6) // 2
    o_ref = refs[n_side]
    kn_ref, vt_ref, qb_ref, gate_ref, tab_ref = refs[2 * n_side + 1:]
    _side_cast(refs[:n_side] + refs[n_side + 1:2 * n_side + 1])
    BLK = MOBA_BLOCK
    HP = MOBA_HEADS_PER_STEP
    G = MOBA_QBLOCKS_PER_STEP
    dh = HEAD_DIM_B
    hp = pl.program_id(1)
    grp = pl.program_id(2)
    nblk = k_ref.shape[1] // BLK

    @pl.when(grp == 0)
    def _prepare():
        kc = lax.broadcasted_iota(jnp.int32, (BLK, BLK), 0)
        qr = lax.broadcasted_iota(jnp.int32, (BLK, BLK), 1)
        rel = (qr - kc).astype(F32)
        for a in range(HP):
            k = k_ref[0, :, a * dh:(a + 1) * dh].astype(F32)
            ms = jnp.mean(k * k, axis=-1, keepdims=True)
            kn = k * lax.rsqrt(ms + RMS_EPS) * gk_ref[...]
            kn_ref[a] = kn.astype(BF16)
            kmean = jnp.concatenate(
                [jnp.mean(kn.reshape(nblk, BLK, dh), axis=1), jnp.zeros((GATE_ROWS - nblk, dh), F32)], axis=0)
            vt_ref[a] = v_ref[0, :, a * dh:(a + 1) * dh].astype(F32).T.astype(BF16)
            slope2 = slopes_ref[hp * HP + a] * LOG2E
            tab_ref[a, 0] = -slope2 * rel
            tab_ref[a, 1] = jnp.where(rel >= 0.0, -slope2 * rel, NEG_INF)

            q = q_ref[0, :, a * dh:(a + 1) * dh].astype(F32)
            ms = jnp.mean(q * q, axis=-1, keepdims=True)
            qs = q * lax.rsqrt(ms + RMS_EPS) * (gq_ref[...] * (dh ** -0.5 * LOG2E))
            q_hi, q_lo = _split_bf16(qs)
            km_hi, km_lo = _split_bf16(kmean)
            qb_ref[a] = q_hi
            gate_ref[a] = _dot_nt(km_hi, q_hi) + _dot_nt(km_lo, q_hi) + _dot_nt(km_hi, q_lo)

    def block_max_shifts(c, a):
        slope2 = slopes_ref[hp * HP + a] * LOG2E
        shifts = [-slope2 * float((c - n) * BLK) for n in range(c)]
        if c > MOBA_TOPK:
            gate = gate_ref[a, :, c * BLK:(c + 1) * BLK]
            blk = lax.broadcasted_iota(jnp.int32, (GATE_ROWS, 1), 0)
            past = blk < c
            for n in range(c):
                g_n = gate[n:n + 1, :]
                beats = ((gate > g_n) | ((gate == g_n) & (blk < n))) & past
                rank = jnp.sum(jnp.where(beats, 1.0, 0.0), axis=0, keepdims=True)
                shifts[n] = jnp.where(rank < float(MOBA_TOPK), shifts[n], NEG_INF)
        return shifts

    def scores(c, a, j):
        s = _dot_nt(kn_ref[a, j * BLK:(j + 1) * BLK, :], qb_ref[a, c * BLK:(c + 1) * BLK, :])
        s = s + tab_ref[a, 1 if j == c else 0]
        return s, jnp.max(s, axis=0, keepdims=True)

    def probs(s, m):
        e = jnp.exp2(s - m)
        return e.astype(BF16), jnp.sum(e, axis=0, keepdims=True)

    def weighted_values(a, j, p):
        return _dot(vt_ref[a, :, j * BLK:(j + 1) * BLK], p)

    def merge_blocks(c, a, ms_, ls_, os_, row0):
        shifts = block_max_shifts(c, a)
        ms_ = [m_j + sh for m_j, sh in zip(ms_, shifts)] + ms_[c:]
        m = functools.reduce(jnp.maximum, ms_)
        ws = [jnp.exp2(m_j - m) for m_j in ms_]
        l = functools.reduce(jnp.add, [w * l_j for w, l_j in zip(ws, ls_)])
        ot = functools.reduce(jnp.add, [w * o_j for w, o_j in zip(ws, os_)])
        ot = ot * (1.0 / l)
        o_ref[0, row0:row0 + BLK, a * dh:(a + 1) * dh] = ot.T.astype(o_ref.dtype)

    def attend(first_block):
        tasks = [(c, a, j) for c in range(first_block, first_block + G) for j in range(c + 1) for a in range(HP)]
        ahead = {}
        stats = {}
        for t in range(len(tasks) + MOBA_SCORE_LOOKAHEAD):
            if t < len(tasks):
                ahead[t] = scores(*tasks[t])
            d = t - MOBA_SCORE_LOOKAHEAD
            if d >= 0:
                c, a, j = tasks[d]
                s, m_j = ahead.pop(d)
                p, l_j = probs(s, m_j)
                ms_, ls_, os_ = stats.setdefault((c, a), ([], [], []))
                os_.append(weighted_values(a, j, p))
                ls_.append(l_j)
                ms_.append(m_j)
                if j == c:
                    merge_blocks(c, a, *stats.pop((c, a)), (c - first_block) * BLK)

    if G == nblk:
        attend(0)
    else:
        for g in range(nblk // G):
            pl.when(grp == g)(functools.partial(attend, g * G))


def _moba(z3, slopes, gq, gk, side):
    B, S, _ = z3.shape
    BLK = MOBA_BLOCK
    HP = MOBA_HEADS_PER_STEP
    G = MOBA_QBLOCKS_PER_STEP
    nblk = S // BLK
    w = HP * HEAD_DIM_B
    qc, kc, vc = COL_QB // w, COL_KB // w, COL_VB // w
    n_hp, n_grp = N_H_B // HP, nblk // G
    side_specs, side_shapes = _side_cast_specs(
        side, B * n_hp * n_grp, lambda b, h, g: (b * n_hp + h) * n_grp + g)
    outs = pl.pallas_call(
        _moba_kernel,
        grid=(B, n_hp, n_grp),
        in_specs=[
            pl.BlockSpec(memory_space=pltpu.SMEM),
            pl.BlockSpec((1, S, w), lambda b, h, g: (b, 0, qc + h)),
            pl.BlockSpec((1, S, w), lambda b, h, g: (b, 0, kc + h)),
            pl.BlockSpec((1, S, w), lambda b, h, g: (b, 0, vc + h)),
            pl.BlockSpec((1, HEAD_DIM_B), lambda b, h, g: (0, 0)),
            pl.BlockSpec((1, HEAD_DIM_B), lambda b, h, g: (0, 0)),
        ] + side_specs,
        out_specs=[pl.BlockSpec((1, G * BLK, w), lambda b, h, g: (b, g, h))] + side_specs,
        out_shape=[jax.ShapeDtypeStruct((B, S, W_B), BF16)] + side_shapes,
        scratch_shapes=[
            pltpu.VMEM((HP, S, HEAD_DIM_B), BF16),
            pltpu.VMEM((HP, HEAD_DIM_B, S), BF16),
            pltpu.VMEM((HP, S, HEAD_DIM_B), BF16),
            pltpu.VMEM((HP, GATE_ROWS, S), F32),
            pltpu.VMEM((HP, 2, BLK, BLK), F32),
        ],
        compiler_params=_params(("arbitrary", "arbitrary", "arbitrary")),
        name="moba",
    )(slopes, z3, z3, z3, gq, gk, *side)
    return outs[0], outs[1:]


def _sigmoid(x):
    return 1.0 / (1.0 + jnp.exp(-x))


def _merge_out_kernel(oa_ref, ob_ref, ga_ref, gb_ref, ba_ref, bb_ref, wa_ref, wb_ref, wo_ref, x_ref,
                      g2_ref, o_ref, u2_ref, mixed_ref, *, chunk):
    oa = oa_ref[...]
    ob = ob_ref[...]
    for c0 in range(0, mixed_ref.shape[1], chunk):
        cs = slice(c0, c0 + chunk)
        a = _dot(oa, wa_ref[:, cs])
        b = _dot(ob, wb_ref[:, cs])
        ga = _sigmoid(ga_ref[:, cs].astype(F32) + ba_ref[:, cs])
        gb = _sigmoid(gb_ref[:, cs].astype(F32) + bb_ref[:, cs])
        mixed_ref[:, cs] = (ga * a + gb * b).astype(mixed_ref.dtype)
    o_ref[...] = x_ref[...] + _dot(mixed_ref[...], wo_ref[...])
    _rms_rows_to_bf16(o_ref, g2_ref, u2_ref)


def _merge_out(oa, ob, gates, ba, bb, wa, wb, wo, x, g2, *, tm=512, chunk=512):
    m, ka = oa.shape
    kb = ob.shape[1]
    d = wa.shape[1]
    resident = pl.Buffered(1)
    return pl.pallas_call(
        functools.partial(_merge_out_kernel, chunk=chunk),
        grid=(m // tm,),
        in_specs=[
            pl.BlockSpec((tm, ka), lambda i: (i, 0)),
            pl.BlockSpec((tm, kb), lambda i: (i, 0)),
            pl.BlockSpec((tm, d), lambda i: (i, 0)),
            pl.BlockSpec((tm, d), lambda i: (i, 1)),
            pl.BlockSpec((1, d), lambda i: (0, 0)),
            pl.BlockSpec((1, d), lambda i: (0, 0)),
            pl.BlockSpec((ka, d), lambda i: (0, 0), pipeline_mode=resident),
            pl.BlockSpec((kb, d), lambda i: (0, 0), pipeline_mode=resident),
            pl.BlockSpec((d, d), lambda i: (0, 0), pipeline_mode=resident),
            pl.BlockSpec((tm, d), lambda i: (i, 0)),
            pl.BlockSpec((1, d), lambda i: (0, 0)),
        ],
        out_specs=[pl.BlockSpec((tm, d), lambda i: (i, 0)), pl.BlockSpec((tm, d), lambda i: (i, 0))],
        out_shape=[jax.ShapeDtypeStruct((m, d), F32), jax.ShapeDtypeStruct((m, d), BF16)],
        scratch_shapes=[pltpu.VMEM((tm, d), BF16)],
        compiler_params=_params(("parallel",)),
        name="merge_out",
    )(oa, ob, gates, gates, ba, bb, wa, wb, wo, x, g2)


def _mm_res_kernel(a_ref, w_ref, r_ref, o_ref):
    o_ref[...] = r_ref[...] + _dot(a_ref[...], w_ref[...])


def _mm_res(a, w, res, *, tm, tn, name):
    m, k = a.shape
    n = w.shape[1]
    return pl.pallas_call(
        _mm_res_kernel,
        grid=(m // tm, n // tn),
        in_specs=[
            pl.BlockSpec((tm, k), lambda i, j: (i, 0)),
            pl.BlockSpec((k, tn), lambda i, j: (0, j)),
            pl.BlockSpec((tm, tn), lambda i, j: (i, j)),
        ],
        out_specs=pl.BlockSpec((tm, tn), lambda i, j: (i, j)),
        out_shape=jax.ShapeDtypeStruct((m, n), F32),
        compiler_params=_params(("parallel", "parallel")),
        name=name,
    )(a, w, res)


def _ffn_up_kernel(u_ref, wg_ref, wu_ref, o_ref):
    u = u_ref[...]
    for c0 in range(0, o_ref.shape[1], V7X_MXU_DIM):
        cs = slice(c0, c0 + V7X_MXU_DIM)
        gate = _dot(u, wg_ref[:, cs])
        up = _dot(u, wu_ref[:, cs])
        o_ref[:, cs] = (gate * _sigmoid(gate) * up).astype(o_ref.dtype)


def _ffn_up(u, wg, wu, *, tm=2048, tn=512):
    m, d = u.shape
    n = wg.shape[1]
    return pl.pallas_call(
        _ffn_up_kernel,
        grid=(m // tm, n // tn),
        in_specs=[
            pl.BlockSpec((tm, d), lambda i, j: (i, 0)),
            pl.BlockSpec((d, tn), lambda i, j: (0, j)),
            pl.BlockSpec((d, tn), lambda i, j: (0, j)),
        ],
        out_specs=pl.BlockSpec((tm, tn), lambda i, j: (i, j)),
        out_shape=jax.ShapeDtypeStruct((m, n), BF16),
        compiler_params=_params(("parallel", "parallel")),
        name="ffn_up",
    )(u, wg, wu)


def kernel(x, norm1_g, w_in, b_gate, q_norm_a, k_norm_a, sinks_a, q_norm_b, k_norm_b,
           w_branch_a, w_branch_b, w_o, norm2_g, w_ffn_gate, w_ffn_up, w_ffn_down):
    B, S, D = x.shape
    depth = w_in.shape[0]
    assert D == D_MODEL and w_in.shape[2] == IN_COLS
    assert S % MOBA_BLOCK == 0 and S % WINDOW == 0
    M = B * S

    slopes_b = jnp.asarray(np.exp2(-8.0 * np.arange(1, N_H_B + 1, dtype=np.float32) / N_H_B), F32)
    swa_bias = _swa_bias_table()
    bd = _block_diag(V7X_MXU_DIM, HEAD_DIM_A, 1.0 / HEAD_DIM_A)

    h = x.reshape(M, D)
    for l in range(depth):
        qkv, u = _norm_proj(h, norm1_g[l].reshape(1, D), w_in[l, :, :COL_GA].astype(BF16),
                            tm=1024, tn=COL_GA // 2)
        gates, (wo_b, wa_b, wb_b) = _gate_proj(u, w_in[l], COL_GA, 2 * D,
                                               [w_o[l], w_branch_a[l], w_branch_b[l]])
        z3 = qkv.reshape(B, S, COL_GA)
        o_a, (wg_b, wu_b) = _swa(z3, sinks_a[l],
                                 jnp.tile(q_norm_a[l], N_Q_A).reshape(1, WQ_A),
                                 jnp.tile(k_norm_a[l], N_KV_A).reshape(1, WKV_A), bd, swa_bias,
                                 [w_ffn_gate[l], w_ffn_up[l]])
        o_b, (wd_b,) = _moba(z3, slopes_b, q_norm_b[l].reshape(1, HEAD_DIM_B),
                             k_norm_b[l].reshape(1, HEAD_DIM_B), [w_ffn_down[l]])
        h1, u2 = _merge_out(o_a.reshape(M, WQ_A), o_b.reshape(M, W_B), gates,
                            b_gate[l, :D].reshape(1, D), b_gate[l, D:].reshape(1, D),
                            wa_b, wb_b, wo_b, h, norm2_g[l].reshape(1, D))
        act = _ffn_up(u2, wg_b, wu_b)
        h = _mm_res(act, wd_b, h1, tm=1024, tn=512, name="ffn_down")
    return h.reshape(B, S, D)
```

```python
import functools

import numpy as np
import jax
import jax.numpy as jnp
from jax import lax
from jax.experimental import pallas as pl
from jax.experimental.pallas import tpu as pltpu

F32 = jnp.float32
BF16 = jnp.bfloat16

D_MODEL = 2048
HEAD_DIM_A = 64
N_Q_A = 16
N_KV_A = 4
WINDOW = 128
HEAD_DIM_B = 128
N_H_B = 8
MOBA_BLOCK = 256
MOBA_TOPK = 3
RMS_EPS = 1e-6

WQ_A = N_Q_A * HEAD_DIM_A
WKV_A = N_KV_A * HEAD_DIM_A
W_B = N_H_B * HEAD_DIM_B
COL_QA = 0
COL_KA = COL_QA + WQ_A
COL_VA = COL_KA + WKV_A
COL_QB = COL_VA + WKV_A
COL_KB = COL_QB + W_B
COL_VB = COL_KB + W_B
COL_GA = COL_VB + W_B
COL_GB = COL_GA + D_MODEL
IN_COLS = COL_GB + D_MODEL

V7X_LANES = 128
V7X_MXU_DIM = 256
VMEM_LIMIT_BYTES = 56 * 1024 * 1024
NORM_ROW_CHUNK = 64
NORM_UNROLL = 2
BF16_SUBLANE_TILE = 16
MOBA_HEADS_PER_STEP = V7X_MXU_DIM // HEAD_DIM_B
MOBA_QBLOCKS_PER_STEP = 8
MOBA_SCORE_LOOKAHEAD = 4
GATE_ROWS = 16
NEG_INF = float("-inf")
LOG2E = 1.4426950408889634


def _params(sem):
    return pltpu.CompilerParams(dimension_semantics=sem, vmem_limit_bytes=VMEM_LIMIT_BYTES)


def _dot(a, b):
    return jnp.dot(a, b, preferred_element_type=F32)


def _dot_nt(a, b):
    return lax.dot_general(a, b, (((1,), (1,)), ((), ())), preferred_element_type=F32)


def _split_bf16(x):
    hi = x.astype(BF16)
    lo = (x - hi.astype(F32)).astype(BF16)
    return hi, lo


def _rms_rows_to_bf16(x_ref, g_ref, u_ref):
    rows = x_ref.shape[0]
    g = g_ref[...]

    def body(c, carry):
        r = pl.multiple_of(c * NORM_ROW_CHUNK, NORM_ROW_CHUNK)
        x = x_ref[pl.ds(r, NORM_ROW_CHUNK), :]
        ms = jnp.mean(x * x, axis=-1, keepdims=True)
        u_ref[pl.ds(r, NORM_ROW_CHUNK), :] = (x * lax.rsqrt(ms + RMS_EPS) * g).astype(BF16)
        return carry

    lax.fori_loop(0, rows // NORM_ROW_CHUNK, body, 0, unroll=NORM_UNROLL)


def _head_norm_kind(col):
    if col < COL_VA:
        return "norm64"
    if COL_QB <= col < COL_VB:
        return "norm128"
    return "copy"


def _qkv_proj_kernel(x_ref, g_ref, w_ref, hg_ref, bd_ref, o_ref, u_ref, *, n_col_tiles):
    j = pl.program_id(1)

    @pl.when(j == 0)
    def _():
        _rms_rows_to_bf16(x_ref, g_ref, u_ref)

    tn = o_ref.shape[1]
    CH = V7X_MXU_DIM

    def finish(z, cs, kind):
        if kind == "norm64":
            inv = lax.rsqrt(_dot((z * z).astype(BF16), bd_ref[...]) + RMS_EPS)
            z = z * inv * hg_ref[:, cs]
        elif kind == "norm128":
            halves = []
            for h0 in range(0, CH, HEAD_DIM_B):
                zh = z[:, h0:h0 + HEAD_DIM_B]
                ms = jnp.mean(zh * zh, axis=-1, keepdims=True)
                halves.append(zh * lax.rsqrt(ms + RMS_EPS))
            z = jnp.concatenate(halves, axis=1) * hg_ref[:, cs]
        o_ref[:, cs] = z.astype(o_ref.dtype)

    def column_tile(jt):
        u = u_ref[...]
        chunks = [slice(c0, c0 + CH) for c0 in range(0, tn, CH)]
        z_next = _dot(u, w_ref[:, chunks[0]])
        for idx, cs in enumerate(chunks):
            z = z_next
            if idx + 1 < len(chunks):
                z_next = _dot(u, w_ref[:, chunks[idx + 1]])
            finish(z, cs, _head_norm_kind(jt * tn + cs.start))

    for jt in range(n_col_tiles):
        pl.when(j == jt)(functools.partial(column_tile, jt))


def _qkv_proj(x, g, w, head_gain, bd, *, tm, tn):
    m, d = x.shape
    n = w.shape[1]
    assert n % tn == 0 and tn % V7X_MXU_DIM == 0
    return pl.pallas_call(
        functools.partial(_qkv_proj_kernel, n_col_tiles=n // tn),
        grid=(m // tm, n // tn),
        in_specs=[
            pl.BlockSpec((tm, d), lambda i, j: (i, 0)),
            pl.BlockSpec((1, d), lambda i, j: (0, 0)),
            pl.BlockSpec((d, tn), lambda i, j: (0, j)),
            pl.BlockSpec((1, tn), lambda i, j: (0, j)),
            pl.BlockSpec((V7X_MXU_DIM, V7X_MXU_DIM), lambda i, j: (0, 0)),
        ],
        out_specs=[
            pl.BlockSpec((tm, tn), lambda i, j: (i, j)),
            pl.BlockSpec((tm, d), lambda i, j: (i, 0)),
        ],
        out_shape=[jax.ShapeDtypeStruct((m, n), BF16), jax.ShapeDtypeStruct((m, d), BF16)],
        compiler_params=_params(("parallel", "arbitrary")),
        name="qkv_proj",
    )(x, g, w, head_gain, bd)


def _side_cast_specs(arrays, n_steps, step_index):
    specs, shapes = [], []
    for w in arrays:
        rows, cols = w.shape
        chunk = rows // n_steps
        assert chunk * n_steps == rows and chunk % BF16_SUBLANE_TILE == 0, (w.shape, n_steps)
        specs.append(pl.BlockSpec((chunk, cols), lambda *g: (step_index(*g), 0)))
        shapes.append(jax.ShapeDtypeStruct(w.shape, BF16))
    return specs, shapes


def _side_cast(refs):
    n = len(refs) // 2
    for src, dst in zip(refs[:n], refs[n:]):
        dst[...] = src[...].astype(dst.dtype)


def _gate_proj_kernel(u_ref, w_ref, *refs):
    n_side = (len(refs) - 1) // 2
    o_ref = refs[n_side]
    o_ref[...] = _dot(u_ref[...], w_ref[...].astype(BF16)).astype(o_ref.dtype)
    _side_cast(refs[:n_side] + refs[n_side + 1:])


def _gate_proj(u, w_all, col0, n, side, *, tm=2048, tn=512):
    m, k = u.shape
    gi, gj = m // tm, n // tn
    assert col0 % tn == 0
    side_specs, side_shapes = _side_cast_specs(side, gi * gj, lambda i, j: i * gj + j)
    outs = pl.pallas_call(
        _gate_proj_kernel,
        grid=(gi, gj),
        in_specs=[
            pl.BlockSpec((tm, k), lambda i, j: (i, 0)),
            pl.BlockSpec((k, tn), lambda i, j: (0, col0 // tn + j)),
        ] + side_specs,
        out_specs=[pl.BlockSpec((tm, tn), lambda i, j: (i, j))] + side_specs,
        out_shape=[jax.ShapeDtypeStruct((m, n), BF16)] + side_shapes,
        compiler_params=_params(("arbitrary", "arbitrary")),
        name="gate_proj",
    )(u, w_all, *side)
    return outs[0], outs[1:]


def _swa_kernel(sinks_ref, q_ref, kc_ref, kp_ref, vc_ref, vp_ref, bias_ref, *refs):
    n_side = (len(refs) - 1) // 2
    o_ref = refs[n_side]
    _side_cast(refs[:n_side] + refs[n_side + 1:])
    L = WINDOW
    first = (pl.program_id(1) == 0).astype(jnp.int32)
    q = q_ref[0]
    k = jnp.concatenate([kp_ref[0], kc_ref[0]], axis=0).astype(F32)
    v = jnp.concatenate([vp_ref[0], vc_ref[0]], axis=0).astype(F32)

    lane = lax.broadcasted_iota(jnp.int32, (1, V7X_LANES), 1)
    left = lane < HEAD_DIM_A
    lo_head = lax.broadcasted_iota(jnp.int32, (1, 2 * L), 1) < L

    kt = [k[:, t * 128:(t + 1) * 128] for t in range(WKV_A // 128)]
    kt_sw = [pltpu.roll(x, HEAD_DIM_A, axis=1) for x in kt]
    vt = [v[:, t * 128:(t + 1) * 128].T.astype(BF16) for t in range(WKV_A // 128)]

    def q_masked(h):
        tile = h // 2
        qt = q[:, tile * 128:(tile + 1) * 128]
        keep = left if h % 2 == 0 else jnp.logical_not(left)
        return jnp.where(keep, qt, jnp.zeros_like(qt))

    rep = N_Q_A // N_KV_A

    def scores(g, hpar):
        t, par = g // 2, g % 2
        ha, hb = rep * g + hpar, rep * g + hpar + 2
        k_al = kt[t] if par == hpar else kt_sw[t]
        qm = jnp.concatenate([q_masked(ha), q_masked(hb)], axis=0)
        return _dot_nt(k_al.astype(BF16), qm) + bias_ref[first, 2 * g + hpar]

    pairs = [(g, hpar) for g in range(N_KV_A) for hpar in range(2)]
    all_scores = [scores(g, hpar) for g, hpar in pairs]

    out_rows = [None] * N_Q_A
    for (g, hpar), s in zip(pairs, all_scores):
        t, par = g // 2, g % 2
        ha, hb = rep * g + hpar, rep * g + hpar + 2
        sink = jnp.where(lo_head, sinks_ref[ha], sinks_ref[hb]) * LOG2E
        m = jnp.maximum(jnp.max(s, axis=0, keepdims=True), sink)
        e = jnp.exp2(s - m)
        denom = jnp.sum(e, axis=0, keepdims=True) + jnp.exp2(sink - m)
        ot = _dot(vt[t], e.astype(BF16))
        og = ot[par * HEAD_DIM_A:(par + 1) * HEAD_DIM_A, :] * (1.0 / denom)
        out_rows[ha] = og[:, :L]
        out_rows[hb] = og[:, L:]
    o_t = jnp.concatenate(out_rows, axis=0)
    o_ref[0] = o_t.T.astype(o_ref.dtype)


def _swa(z3, sinks, bias, side):
    B, S, _ = z3.shape
    L = WINDOW
    nb = S // L
    kblk = COL_KA // WKV_A
    vblk = COL_VA // WKV_A
    prev = lambda n: jnp.maximum(n - 1, 0)
    side_specs, side_shapes = _side_cast_specs(side, B * nb, lambda b, n: b * nb + n)
    outs = pl.pallas_call(
        _swa_kernel,
        grid=(B, nb),
        in_specs=[
            pl.BlockSpec(memory_space=pltpu.SMEM),
            pl.BlockSpec((1, L, WQ_A), lambda b, n: (b, n, 0)),
            pl.BlockSpec((1, L, WKV_A), lambda b, n: (b, n, kblk)),
            pl.BlockSpec((1, L, WKV_A), lambda b, n: (b, prev(n), kblk)),
            pl.BlockSpec((1, L, WKV_A), lambda b, n: (b, n, vblk)),
            pl.BlockSpec((1, L, WKV_A), lambda b, n: (b, prev(n), vblk)),
            pl.BlockSpec((2, N_Q_A // 2, 2 * L, 2 * L), lambda b, n: (0, 0, 0, 0)),
        ] + side_specs,
        out_specs=[pl.BlockSpec((1, L, WQ_A), lambda b, n: (b, n, 0))] + side_specs,
        out_shape=[jax.ShapeDtypeStruct((B, S, WQ_A), BF16)] + side_shapes,
        compiler_params=_params(("arbitrary", "arbitrary")),
        name="swa",
    )(sinks, z3, z3, z3, z3, z3, bias, *side)
    return outs[0], outs[1:]


def _swa_bias_table():
    L = WINDOW
    rep = N_Q_A // N_KV_A
    slopes = np.exp2(-8.0 * np.arange(1, N_Q_A + 1, dtype=np.float32) / N_Q_A).astype(np.float32)
    kj = np.arange(2 * L)[:, None]
    qi = np.arange(L)[None, :]
    dist = L + qi - kj
    window = (dist >= 0) & (dist < WINDOW)
    table = np.empty((2, N_Q_A // 2, 2 * L, 2 * L), np.float32)
    for first in range(2):
        valid = window & ((kj >= L) if first else True)
        for g in range(N_KV_A):
            for hpar in range(2):
                for a in range(2):
                    h = rep * g + hpar + 2 * a
                    table[first, 2 * g + hpar, :, a * L:(a + 1) * L] = np.where(
                        valid, -(slopes[h] * LOG2E) * dist.astype(np.float32), -np.inf)
    return jnp.asarray(table)


def _block_diag(width, block, value):
    idx = np.arange(width) // block
    return jnp.asarray((idx[:, None] == idx[None, :]).astype(np.float32) * value, dtype=BF16)


def _moba_kernel(slopes_ref, q_ref, k_ref, v_ref, *refs):
    n_side = (len(refs) - 4) // 2
    o_ref = refs[n_side]
    vt_ref, gate_ref, tab_ref = refs[2 * n_side + 1:]
    _side_cast(refs[:n_side] + refs[n_side + 1:2 * n_side + 1])
    BLK = MOBA_BLOCK
    HP = MOBA_HEADS_PER_STEP
    G = MOBA_QBLOCKS_PER_STEP
    dh = HEAD_DIM_B
    hp = pl.program_id(1)
    grp = pl.program_id(2)
    nblk = k_ref.shape[1] // BLK

    @pl.when(grp == 0)
    def _prepare():
        kc = lax.broadcasted_iota(jnp.int32, (BLK, BLK), 0)
        qr = lax.broadcasted_iota(jnp.int32, (BLK, BLK), 1)
        rel = (qr - kc).astype(F32)
        eye = (lax.broadcasted_iota(jnp.int32, (dh, dh), 0)
               == lax.broadcasted_iota(jnp.int32, (dh, dh), 1)).astype(BF16)
        for a in range(HP):
            kn = k_ref[0, :, a * dh:(a + 1) * dh]
            kmean = jnp.concatenate(
                [jnp.mean(kn.astype(F32).reshape(nblk, BLK, dh), axis=1),
                 jnp.zeros((GATE_ROWS - nblk, dh), F32)], axis=0)
            vt_ref[a] = _dot_nt(eye, v_ref[0, :, a * dh:(a + 1) * dh]).astype(BF16)
            slope2 = slopes_ref[hp * HP + a] * LOG2E
            tab_ref[a, 0] = -slope2 * rel
            tab_ref[a, 1] = jnp.where(rel >= 0.0, -slope2 * rel, NEG_INF)
            km_hi, km_lo = _split_bf16(kmean)
            qn = q_ref[0, :, a * dh:(a + 1) * dh]
            gate_ref[a] = _dot_nt(km_hi, qn) + _dot_nt(km_lo, qn)

    def block_max_shifts(c, a):
        slope2 = slopes_ref[hp * HP + a] * LOG2E
        shifts = [-slope2 * float((c - n) * BLK) for n in range(c)]
        if c > MOBA_TOPK:
            gate = gate_ref[a, :, c * BLK:(c + 1) * BLK]
            blk = lax.broadcasted_iota(jnp.int32, (GATE_ROWS, 1), 0)
            past = blk < c
            for n in range(c):
                g_n = gate[n:n + 1, :]
                beats = ((gate > g_n) | ((gate == g_n) & (blk < n))) & past
                rank = jnp.sum(jnp.where(beats, 1.0, 0.0), axis=0, keepdims=True)
                shifts[n] = jnp.where(rank < float(MOBA_TOPK), shifts[n], NEG_INF)
        return shifts

    def scores(c, a, j):
        s = _dot_nt(k_ref[0, j * BLK:(j + 1) * BLK, a * dh:(a + 1) * dh],
                    q_ref[0, c * BLK:(c + 1) * BLK, a * dh:(a + 1) * dh])
        s = s + tab_ref[a, 1 if j == c else 0]
        return s, jnp.max(s, axis=0, keepdims=True)

    def probs(s, m):
        e = jnp.exp2(s - m)
        return e.astype(BF16), jnp.sum(e, axis=0, keepdims=True)

    def weighted_values(a, j, p):
        return _dot(vt_ref[a, :, j * BLK:(j + 1) * BLK], p)

    def merge_blocks(c, a, ms_, ls_, os_, row0):
        shifts = block_max_shifts(c, a)
        ms_ = [m_j + sh for m_j, sh in zip(ms_, shifts)] + ms_[c:]
        m = functools.reduce(jnp.maximum, ms_)
        ws = [jnp.exp2(m_j - m) for m_j in ms_]
        l = functools.reduce(jnp.add, [w * l_j for w, l_j in zip(ws, ls_)])
        ot = functools.reduce(jnp.add, [w * o_j for w, o_j in zip(ws, os_)])
        ot = ot * (1.0 / l)
        o_ref[0, row0:row0 + BLK, a * dh:(a + 1) * dh] = ot.T.astype(o_ref.dtype)

    def attend(first_block):
        tasks = [(c, a, j) for c in range(first_block, first_block + G) for j in range(c + 1) for a in range(HP)]
        ahead = {}
        stats = {}
        for t in range(len(tasks) + MOBA_SCORE_LOOKAHEAD):
            if t < len(tasks):
                ahead[t] = scores(*tasks[t])
            d = t - MOBA_SCORE_LOOKAHEAD
            if d >= 0:
                c, a, j = tasks[d]
                s, m_j = ahead.pop(d)
                p, l_j = probs(s, m_j)
                ms_, ls_, os_ = stats.setdefault((c, a), ([], [], []))
                os_.append(weighted_values(a, j, p))
                ls_.append(l_j)
                ms_.append(m_j)
                if j == c:
                    merge_blocks(c, a, *stats.pop((c, a)), (c - first_block) * BLK)

    if G == nblk:
        attend(0)
    else:
        for g in range(nblk // G):
            pl.when(grp == g)(functools.partial(attend, g * G))


def _moba(z3, slopes, side):
    B, S, _ = z3.shape
    BLK = MOBA_BLOCK
    HP = MOBA_HEADS_PER_STEP
    G = MOBA_QBLOCKS_PER_STEP
    nblk = S // BLK
    w = HP * HEAD_DIM_B
    qc, kc, vc = COL_QB // w, COL_KB // w, COL_VB // w
    n_hp, n_grp = N_H_B // HP, nblk // G
    side_specs, side_shapes = _side_cast_specs(
        side, B * n_hp * n_grp, lambda b, h, g: (b * n_hp + h) * n_grp + g)
    outs = pl.pallas_call(
        _moba_kernel,
        grid=(B, n_hp, n_grp),
        in_specs=[
            pl.BlockSpec(memory_space=pltpu.SMEM),
            pl.BlockSpec((1, S, w), lambda b, h, g: (b, 0, qc + h)),
            pl.BlockSpec((1, S, w), lambda b, h, g: (b, 0, kc + h)),
            pl.BlockSpec((1, S, w), lambda b, h, g: (b, 0, vc + h)),
        ] + side_specs,
        out_specs=[pl.BlockSpec((1, G * BLK, w), lambda b, h, g: (b, g, h))] + side_specs,
        out_shape=[jax.ShapeDtypeStruct((B, S, W_B), BF16)] + side_shapes,
        scratch_shapes=[
            pltpu.VMEM((HP, HEAD_DIM_B, S), BF16),
            pltpu.VMEM((HP, GATE_ROWS, S), F32),
            pltpu.VMEM((HP, 2, BLK, BLK), F32),
        ],
        compiler_params=_params(("arbitrary", "arbitrary", "arbitrary")),
        name="moba",
    )(slopes, z3, z3, z3, *side)
    return outs[0], outs[1:]


def _sigmoid(x):
    return 1.0 / (1.0 + jnp.exp(-x))


def _merge_out_kernel(oa_ref, ob_ref, ga_ref, gb_ref, ba_ref, bb_ref, wa_ref, wb_ref, wo_ref, x_ref,
                      g2_ref, o_ref, u2_ref, mixed_ref, *, chunk):
    oa = oa_ref[...]
    ob = ob_ref[...]
    for c0 in range(0, mixed_ref.shape[1], chunk):
        cs = slice(c0, c0 + chunk)
        a = _dot(oa, wa_ref[:, cs])
        b = _dot(ob, wb_ref[:, cs])
        ga = _sigmoid(ga_ref[:, cs].astype(F32) + ba_ref[:, cs])
        gb = _sigmoid(gb_ref[:, cs].astype(F32) + bb_ref[:, cs])
        mixed_ref[:, cs] = (ga * a + gb * b).astype(mixed_ref.dtype)
    o_ref[...] = x_ref[...] + _dot(mixed_ref[...], wo_ref[...])
    _rms_rows_to_bf16(o_ref, g2_ref, u2_ref)


def _merge_out(oa, ob, gates, ba, bb, wa, wb, wo, x, g2, *, tm=512, chunk=512):
    m, ka = oa.shape
    kb = ob.shape[1]
    d = wa.shape[1]
    resident = pl.Buffered(1)
    return pl.pallas_call(
        functools.partial(_merge_out_kernel, chunk=chunk),
        grid=(m // tm,),
        in_specs=[
            pl.BlockSpec((tm, ka), lambda i: (i, 0)),
            pl.BlockSpec((tm, kb), lambda i: (i, 0)),
            pl.BlockSpec((tm, d), lambda i: (i, 0)),
            pl.BlockSpec((tm, d), lambda i: (i, 1)),
            pl.BlockSpec((1, d), lambda i: (0, 0)),
            pl.BlockSpec((1, d), lambda i: (0, 0)),
            pl.BlockSpec((ka, d), lambda i: (0, 0), pipeline_mode=resident),
            pl.BlockSpec((kb, d), lambda i: (0, 0), pipeline_mode=resident),
            pl.BlockSpec((d, d), lambda i: (0, 0), pipeline_mode=resident),
            pl.BlockSpec((tm, d), lambda i: (i, 0)),
            pl.BlockSpec((1, d), lambda i: (0, 0)),
        ],
        out_specs=[pl.BlockSpec((tm, d), lambda i: (i, 0)), pl.BlockSpec((tm, d), lambda i: (i, 0))],
        out_shape=[jax.ShapeDtypeStruct((m, d), F32), jax.ShapeDtypeStruct((m, d), BF16)],
        scratch_shapes=[pltpu.VMEM((tm, d), BF16)],
        compiler_params=_params(("parallel",)),
        name="merge_out",
    )(oa, ob, gates, gates, ba, bb, wa, wb, wo, x, g2)


def _mm_res_kernel(a_ref, w_ref, r_ref, o_ref):
    o_ref[...] = r_ref[...] + _dot(a_ref[...], w_ref[...])


def _mm_res(a, w, res, *, tm, tn, name):
    m, k = a.shape
    n = w.shape[1]
    return pl.pallas_call(
        _mm_res_kernel,
        grid=(m // tm, n // tn),
        in_specs=[
            pl.BlockSpec((tm, k), lambda i, j: (i, 0)),
            pl.BlockSpec((k, tn), lambda i, j: (0, j)),
            pl.BlockSpec((tm, tn), lambda i, j: (i, j)),
        ],
        out_specs=pl.BlockSpec((tm, tn), lambda i, j: (i, j)),
        out_shape=jax.ShapeDtypeStruct((m, n), F32),
        compiler_params=_params(("parallel", "parallel")),
        name=name,
    )(a, w, res)


def _ffn_up_kernel(u_ref, wg_ref, wu_ref, o_ref):
    u = u_ref[...]
    for c0 in range(0, o_ref.shape[1], V7X_MXU_DIM):
        cs = slice(c0, c0 + V7X_MXU_DIM)
        gate = _dot(u, wg_ref[:, cs])
        up = _dot(u, wu_ref[:, cs])
        o_ref[:, cs] = (gate * _sigmoid(gate) * up).astype(o_ref.dtype)


def _ffn_up(u, wg, wu, *, tm=2048, tn=512):
    m, d = u.shape
    n = wg.shape[1]
    return pl.pallas_call(
        _ffn_up_kernel,
        grid=(m // tm, n // tn),
        in_specs=[
            pl.BlockSpec((tm, d), lambda i, j: (i, 0)),
            pl.BlockSpec((d, tn), lambda i, j: (0, j)),
            pl.BlockSpec((d, tn), lambda i, j: (0, j)),
        ],
        out_specs=pl.BlockSpec((tm, tn), lambda i, j: (i, j)),
        out_shape=jax.ShapeDtypeStruct((m, n), BF16),
        compiler_params=_params(("parallel", "parallel")),
        name="ffn_up",
    )(u, wg, wu)


def kernel(x, norm1_g, w_in, b_gate, q_norm_a, k_norm_a, sinks_a, q_norm_b, k_norm_b,
           w_branch_a, w_branch_b, w_o, norm2_g, w_ffn_gate, w_ffn_up, w_ffn_down):
    B, S, D = x.shape
    depth = w_in.shape[0]
    assert D == D_MODEL and w_in.shape[2] == IN_COLS
    assert S % MOBA_BLOCK == 0 and S % WINDOW == 0
    M = B * S

    slopes_b = jnp.asarray(np.exp2(-8.0 * np.arange(1, N_H_B + 1, dtype=np.float32) / N_H_B), F32)
    swa_bias = _swa_bias_table()
    bd = _block_diag(V7X_MXU_DIM, HEAD_DIM_A, 1.0 / HEAD_DIM_A)

    h = x.reshape(M, D)
    for l in range(depth):
        head_gain = jnp.concatenate([
            jnp.tile(q_norm_a[l], N_Q_A) * (HEAD_DIM_A ** -0.5 * LOG2E),
            jnp.tile(k_norm_a[l], N_KV_A),
            jnp.ones((WKV_A,), F32),
            jnp.tile(q_norm_b[l], N_H_B) * (HEAD_DIM_B ** -0.5 * LOG2E),
            jnp.tile(k_norm_b[l], N_H_B),
            jnp.ones((W_B,), F32),
        ]).reshape(1, COL_GA)
        qkv, u = _qkv_proj(h, norm1_g[l].reshape(1, D), w_in[l, :, :COL_GA].astype(BF16), head_gain, bd,
                           tm=1024, tn=COL_GA // 3)
        gates, (wo_b, wa_b, wb_b) = _gate_proj(u, w_in[l], COL_GA, 2 * D,
                                               [w_o[l], w_branch_a[l], w_branch_b[l]])
        z3 = qkv.reshape(B, S, COL_GA)
        o_a, (wg_b, wu_b) = _swa(z3, sinks_a[l], swa_bias, [w_ffn_gate[l], w_ffn_up[l]])
        o_b, (wd_b,) = _moba(z3, slopes_b, [w_ffn_down[l]])
        h1, u2 = _merge_out(o_a.reshape(M, WQ_A), o_b.reshape(M, W_B), gates,
                            b_gate[l, :D].reshape(1, D), b_gate[l, D:].reshape(1, D),
                            wa_b, wb_b, wo_b, h, norm2_g[l].reshape(1, D))
        act = _ffn_up(u2, wg_b, wu_b)
        h = _mm_res(act, wd_b, h1, tm=1024, tn=512, name="ffn_down")
    return h.reshape(B, S, D)
```

```python
import functools

import numpy as np
import jax
import jax.numpy as jnp
from jax import lax
from jax.experimental import pallas as pl
from jax.experimental.pallas import tpu as pltpu

F32 = jnp.float32
BF16 = jnp.bfloat16

D_MODEL = 2048
HEAD_DIM_A = 64
N_Q_A = 16
N_KV_A = 4
WINDOW = 128
HEAD_DIM_B = 128
N_H_B = 8
MOBA_BLOCK = 256
MOBA_TOPK = 3
RMS_EPS = 1e-6

WQ_A = N_Q_A * HEAD_DIM_A
WKV_A = N_KV_A * HEAD_DIM_A
W_B = N_H_B * HEAD_DIM_B
COL_QA = 0
COL_KA = COL_QA + WQ_A
COL_VA = COL_KA + WKV_A
COL_QB = COL_VA + WKV_A
COL_KB = COL_QB + W_B
COL_VB = COL_KB + W_B
COL_GA = COL_VB + W_B
COL_GB = COL_GA + D_MODEL
IN_COLS = COL_GB + D_MODEL

V7X_LANES = 128
V7X_MXU_DIM = 256
VMEM_LIMIT_BYTES = 56 * 1024 * 1024
NORM_ROW_CHUNK = 64
NORM_UNROLL = 2
BF16_SUBLANE_TILE = 16
SWA_QBLOCKS_PER_STEP = 8
SWA_SCORE_LOOKAHEAD = 8
MOBA_HEADS_PER_STEP = V7X_MXU_DIM // HEAD_DIM_B
MOBA_QBLOCKS_PER_STEP = 8
MOBA_TASK_LOOKAHEAD = 4
GATE_ROWS = 16
NEG_INF = float("-inf")
LOG2E = 1.4426950408889634


def _params(sem):
    return pltpu.CompilerParams(dimension_semantics=sem, vmem_limit_bytes=VMEM_LIMIT_BYTES)


def _dot(a, b):
    return jnp.dot(a, b, preferred_element_type=F32)


def _dot_nt(a, b):
    return lax.dot_general(a, b, (((1,), (1,)), ((), ())), preferred_element_type=F32)


def _split_bf16(x):
    hi = x.astype(BF16)
    lo = (x - hi.astype(F32)).astype(BF16)
    return hi, lo


def _rms_rows_to_bf16(x_ref, g_ref, u_ref):
    rows = x_ref.shape[0]
    g = g_ref[...]

    def body(c, carry):
        r = pl.multiple_of(c * NORM_ROW_CHUNK, NORM_ROW_CHUNK)
        x = x_ref[pl.ds(r, NORM_ROW_CHUNK), :]
        ms = jnp.mean(x * x, axis=-1, keepdims=True)
        u_ref[pl.ds(r, NORM_ROW_CHUNK), :] = (x * lax.rsqrt(ms + RMS_EPS) * g).astype(BF16)
        return carry

    lax.fori_loop(0, rows // NORM_ROW_CHUNK, body, 0, unroll=NORM_UNROLL)


def _head_norm_kind(col):
    if col < COL_VA:
        return "norm64"
    if COL_QB <= col < COL_VB:
        return "norm128"
    return "copy"


def _qkv_proj_kernel(x_ref, g_ref, w_ref, hg_ref, bd_ref, o_ref, u_ref, *, n_col_tiles):
    j = pl.program_id(1)

    @pl.when(j == 0)
    def _():
        _rms_rows_to_bf16(x_ref, g_ref, u_ref)

    tn = o_ref.shape[1]
    CH = V7X_MXU_DIM

    def finish(z, cs, kind):
        if kind == "norm64":
            inv = lax.rsqrt(_dot((z * z).astype(BF16), bd_ref[...]) + RMS_EPS)
            z = z * inv * hg_ref[:, cs]
        elif kind == "norm128":
            halves = []
            for h0 in range(0, CH, HEAD_DIM_B):
                zh = z[:, h0:h0 + HEAD_DIM_B]
                ms = jnp.mean(zh * zh, axis=-1, keepdims=True)
                halves.append(zh * lax.rsqrt(ms + RMS_EPS))
            z = jnp.concatenate(halves, axis=1) * hg_ref[:, cs]
        o_ref[:, cs] = z.astype(o_ref.dtype)

    def column_tile(jt):
        u = u_ref[...]
        chunks = [slice(c0, c0 + CH) for c0 in range(0, tn, CH)]
        z_next = _dot(u, w_ref[:, chunks[0]])
        for idx, cs in enumerate(chunks):
            z = z_next
            if idx + 1 < len(chunks):
                z_next = _dot(u, w_ref[:, chunks[idx + 1]])
            finish(z, cs, _head_norm_kind(jt * tn + cs.start))

    for jt in range(n_col_tiles):
        pl.when(j == jt)(functools.partial(column_tile, jt))


def _qkv_proj(x, g, w, head_gain, bd, *, tm, tn):
    m, d = x.shape
    n = w.shape[1]
    assert n % tn == 0 and tn % V7X_MXU_DIM == 0
    return pl.pallas_call(
        functools.partial(_qkv_proj_kernel, n_col_tiles=n // tn),
        grid=(m // tm, n // tn),
        in_specs=[
            pl.BlockSpec((tm, d), lambda i, j: (i, 0)),
            pl.BlockSpec((1, d), lambda i, j: (0, 0)),
            pl.BlockSpec((d, tn), lambda i, j: (0, j)),
            pl.BlockSpec((1, tn), lambda i, j: (0, j)),
            pl.BlockSpec((V7X_MXU_DIM, V7X_MXU_DIM), lambda i, j: (0, 0)),
        ],
        out_specs=[
            pl.BlockSpec((tm, tn), lambda i, j: (i, j)),
            pl.BlockSpec((tm, d), lambda i, j: (i, 0)),
        ],
        out_shape=[jax.ShapeDtypeStruct((m, n), BF16), jax.ShapeDtypeStruct((m, d), BF16)],
        compiler_params=_params(("parallel", "arbitrary")),
        name="qkv_proj",
    )(x, g, w, head_gain, bd)


def _side_cast_specs(arrays, n_steps, step_index):
    specs, shapes = [], []
    for w in arrays:
        rows, cols = w.shape
        chunk = rows // n_steps
        assert chunk * n_steps == rows and chunk % BF16_SUBLANE_TILE == 0, (w.shape, n_steps)
        specs.append(pl.BlockSpec((chunk, cols), lambda *g: (step_index(*g), 0)))
        shapes.append(jax.ShapeDtypeStruct(w.shape, BF16))
    return specs, shapes


def _side_cast(refs):
    n = len(refs) // 2
    for src, dst in zip(refs[:n], refs[n:]):
        dst[...] = src[...].astype(dst.dtype)


def _gate_proj_kernel(u_ref, w_ref, *refs):
    n_side = (len(refs) - 1) // 2
    o_ref = refs[n_side]
    o_ref[...] = _dot(u_ref[...], w_ref[...].astype(BF16)).astype(o_ref.dtype)
    _side_cast(refs[:n_side] + refs[n_side + 1:])


def _gate_proj(u, w_all, col0, n, side, *, tm=2048, tn=512):
    m, k = u.shape
    gi, gj = m // tm, n // tn
    assert col0 % tn == 0
    side_specs, side_shapes = _side_cast_specs(side, gi * gj, lambda i, j: i * gj + j)
    outs = pl.pallas_call(
        _gate_proj_kernel,
        grid=(gi, gj),
        in_specs=[
            pl.BlockSpec((tm, k), lambda i, j: (i, 0)),
            pl.BlockSpec((k, tn), lambda i, j: (0, col0 // tn + j)),
        ] + side_specs,
        out_specs=[pl.BlockSpec((tm, tn), lambda i, j: (i, j))] + side_specs,
        out_shape=[jax.ShapeDtypeStruct((m, n), BF16)] + side_shapes,
        compiler_params=_params(("arbitrary", "arbitrary")),
        name="gate_proj",
    )(u, w_all, *side)
    return outs[0], outs[1:]


def _swa_kernel(sinks_ref, q_ref, kc_ref, kp_ref, vc_ref, vp_ref, bias_ref, *refs):
    n_side = (len(refs) - 1) // 2
    o_ref = refs[n_side]
    _side_cast(refs[:n_side] + refs[n_side + 1:])
    L = WINDOW
    QB = SWA_QBLOCKS_PER_STEP
    first_step = (pl.program_id(1) == 0).astype(jnp.int32)
    k = jnp.concatenate([kp_ref[0], kc_ref[0]], axis=0).astype(F32)
    v = jnp.concatenate([vp_ref[0], vc_ref[0]], axis=0).astype(F32)

    lane = lax.broadcasted_iota(jnp.int32, (1, V7X_LANES), 1)
    left = lane < HEAD_DIM_A
    lo_head = lax.broadcasted_iota(jnp.int32, (1, 2 * L), 1) < L

    n_kt = WKV_A // V7X_LANES
    kt = [k[:, t * 128:(t + 1) * 128] for t in range(n_kt)]
    kt_sw = [pltpu.roll(x, HEAD_DIM_A, axis=1).astype(BF16) for x in kt]
    kt = [x.astype(BF16) for x in kt]
    pad_row = lax.broadcasted_iota(jnp.int32, (BF16_SUBLANE_TILE, k.shape[0]), 0)
    ones_row = jnp.where(pad_row == 0, 1.0, 0.0)
    vt = [jnp.concatenate([v[:, t * 128:(t + 1) * 128].T, ones_row], axis=0).astype(BF16) for t in range(n_kt)]

    def q_masked(qb, h):
        tile = h // 2
        qt = q_ref[0, qb * L:(qb + 1) * L, tile * 128:(tile + 1) * 128]
        keep = left if h % 2 == 0 else jnp.logical_not(left)
        return jnp.where(keep, qt, jnp.zeros_like(qt))

    rep = N_Q_A // N_KV_A

    def scores(qb, g, hpar):
        t, par = g // 2, g % 2
        ha, hb = rep * g + hpar, rep * g + hpar + 2
        k_al = (kt if par == hpar else kt_sw)[t][qb * L:(qb + 2) * L]
        qm = jnp.concatenate([q_masked(qb, ha), q_masked(qb, hb)], axis=0)
        first = first_step if qb == 0 else 0
        return _dot_nt(k_al, qm) + bias_ref[first, 2 * g + hpar]

    def weighted_values(qb, g, hpar, s, out_rows):
        t, par = g // 2, g % 2
        ha, hb = rep * g + hpar, rep * g + hpar + 2
        sink = jnp.where(lo_head, sinks_ref[ha], sinks_ref[hb]) * LOG2E
        m = jnp.maximum(jnp.max(s, axis=0, keepdims=True), sink)
        e = jnp.exp2(s - m).astype(BF16)
        ot = _dot(vt[t][:, qb * L:(qb + 2) * L], e)
        denom = ot[V7X_LANES:V7X_LANES + 1, :] + jnp.exp2(sink - m)
        og = ot[par * HEAD_DIM_A:(par + 1) * HEAD_DIM_A, :] * (1.0 / denom)
        out_rows[ha] = og[:, :L]
        out_rows[hb] = og[:, L:]

    tasks = [(qb, g, hpar) for qb in range(QB) for g in range(N_KV_A) for hpar in range(2)]
    ahead = {}
    out_rows = {}
    for t in range(len(tasks) + SWA_SCORE_LOOKAHEAD):
        if t < len(tasks):
            ahead[t] = scores(*tasks[t])
        d = t - SWA_SCORE_LOOKAHEAD
        if d >= 0:
            qb, g, hpar = tasks[d]
            rows = out_rows.setdefault(qb, [None] * N_Q_A)
            weighted_values(qb, g, hpar, ahead.pop(d), rows)
            if (g, hpar) == (N_KV_A - 1, 1):
                o_t = jnp.concatenate(out_rows.pop(qb), axis=0)
                o_ref[0, qb * L:(qb + 1) * L, :] = o_t.T.astype(o_ref.dtype)


def _swa(z3, sinks, bias, side):
    B, S, _ = z3.shape
    L = WINDOW
    QB = SWA_QBLOCKS_PER_STEP
    nb = S // (QB * L)
    kblk = COL_KA // WKV_A
    vblk = COL_VA // WKV_A
    prev = lambda n: jnp.maximum(n * QB - 1, 0)
    side_specs, side_shapes = _side_cast_specs(side, B * nb, lambda b, n: b * nb + n)
    outs = pl.pallas_call(
        _swa_kernel,
        grid=(B, nb),
        in_specs=[
            pl.BlockSpec(memory_space=pltpu.SMEM),
            pl.BlockSpec((1, QB * L, WQ_A), lambda b, n: (b, n, 0)),
            pl.BlockSpec((1, QB * L, WKV_A), lambda b, n: (b, n, kblk)),
            pl.BlockSpec((1, L, WKV_A), lambda b, n: (b, prev(n), kblk)),
            pl.BlockSpec((1, QB * L, WKV_A), lambda b, n: (b, n, vblk)),
            pl.BlockSpec((1, L, WKV_A), lambda b, n: (b, prev(n), vblk)),
            pl.BlockSpec((2, N_Q_A // 2, 2 * L, 2 * L), lambda b, n: (0, 0, 0, 0)),
        ] + side_specs,
        out_specs=[pl.BlockSpec((1, QB * L, WQ_A), lambda b, n: (b, n, 0))] + side_specs,
        out_shape=[jax.ShapeDtypeStruct((B, S, WQ_A), BF16)] + side_shapes,
        compiler_params=_params(("arbitrary", "arbitrary")),
        name="swa",
    )(sinks, z3, z3, z3, z3, z3, bias, *side)
    return outs[0], outs[1:]


def _swa_bias_table():
    L = WINDOW
    rep = N_Q_A // N_KV_A
    slopes = np.exp2(-8.0 * np.arange(1, N_Q_A + 1, dtype=np.float32) / N_Q_A).astype(np.float32)
    kj = np.arange(2 * L)[:, None]
    qi = np.arange(L)[None, :]
    dist = L + qi - kj
    window = (dist >= 0) & (dist < WINDOW)
    table = np.empty((2, N_Q_A // 2, 2 * L, 2 * L), np.float32)
    for first in range(2):
        valid = window & ((kj >= L) if first else True)
        for g in range(N_KV_A):
            for hpar in range(2):
                for a in range(2):
                    h = rep * g + hpar + 2 * a
                    table[first, 2 * g + hpar, :, a * L:(a + 1) * L] = np.where(
                        valid, -(slopes[h] * LOG2E) * dist.astype(np.float32), -np.inf)
    return jnp.asarray(table)


def _block_diag(width, block, value):
    idx = np.arange(width) // block
    return jnp.asarray((idx[:, None] == idx[None, :]).astype(np.float32) * value, dtype=BF16)


def _moba_kernel(slopes_ref, q_ref, k_ref, v_ref, *refs):
    n_side = (len(refs) - 4) // 2
    o_ref = refs[n_side]
    vt_ref, gate_ref, tab_ref = refs[2 * n_side + 1:]
    _side_cast(refs[:n_side] + refs[n_side + 1:2 * n_side + 1])
    BLK = MOBA_BLOCK
    HP = MOBA_HEADS_PER_STEP
    G = MOBA_QBLOCKS_PER_STEP
    dh = HEAD_DIM_B
    hp = pl.program_id(1)
    grp = pl.program_id(2)
    S = k_ref.shape[1]
    nblk = S // BLK

    @pl.when(grp == 0)
    def _prepare():
        kc = lax.broadcasted_iota(jnp.int32, (BLK, BLK), 0)
        qr = lax.broadcasted_iota(jnp.int32, (BLK, BLK), 1)
        rel = (qr - kc).astype(F32)
        eye = (lax.broadcasted_iota(jnp.int32, (dh, dh), 0)
               == lax.broadcasted_iota(jnp.int32, (dh, dh), 1)).astype(BF16)
        for a in range(HP):
            kn = k_ref[0, :, a * dh:(a + 1) * dh]
            kmean = jnp.concatenate(
                [jnp.mean(kn.astype(F32).reshape(nblk, BLK, dh), axis=1),
                 jnp.zeros((GATE_ROWS - nblk, dh), F32)], axis=0)
            vt_ref[a, 0:dh, :] = _dot_nt(eye, v_ref[0, :, a * dh:(a + 1) * dh]).astype(BF16)
            pad_row = lax.broadcasted_iota(jnp.int32, (BF16_SUBLANE_TILE, S), 0)
            vt_ref[a, dh:, :] = jnp.where(pad_row == 0, 1.0, 0.0).astype(BF16)
            slope2 = slopes_ref[hp * HP + a] * LOG2E
            tab_ref[a, 0] = -slope2 * rel
            tab_ref[a, 1] = jnp.where(rel >= 0.0, -slope2 * rel, NEG_INF)
            km_hi, km_lo = _split_bf16(kmean)
            qn = q_ref[0, :, a * dh:(a + 1) * dh]
            gate_ref[a] = _dot_nt(km_hi, qn) + _dot_nt(km_lo, qn)

    def block_max_shifts(c, a):
        slope2 = slopes_ref[hp * HP + a] * LOG2E
        shifts = [-slope2 * float((c - n) * BLK) for n in range(c)]
        if c > MOBA_TOPK:
            gate = gate_ref[a, :, c * BLK:(c + 1) * BLK]
            blk = lax.broadcasted_iota(jnp.int32, (GATE_ROWS, 1), 0)
            past = blk < c
            for n in range(c):
                g_n = gate[n:n + 1, :]
                beats = ((gate > g_n) | ((gate == g_n) & (blk < n))) & past
                rank = jnp.sum(jnp.where(beats, 1.0, 0.0), axis=0, keepdims=True)
                shifts[n] = jnp.where(rank < float(MOBA_TOPK), shifts[n], NEG_INF)
        return shifts

    def scores(c, a):
        ss, ms = [], []
        for j in range(c + 1):
            s = _dot_nt(k_ref[0, j * BLK:(j + 1) * BLK, a * dh:(a + 1) * dh],
                        q_ref[0, c * BLK:(c + 1) * BLK, a * dh:(a + 1) * dh])
            s = s + tab_ref[a, 1 if j == c else 0]
            ss.append(s)
            ms.append(jnp.max(s, axis=0, keepdims=True))
        return ss, ms

    def weighted_values(c, a, ss, ms, row0):
        shifts = block_max_shifts(c, a)
        m = functools.reduce(jnp.maximum, [m_j + sh for m_j, sh in zip(ms, shifts)] + ms[c:])
        acc = None
        for j in range(c + 1):
            p = jnp.exp2(ss[j] - (m - shifts[j] if j < c else m)).astype(BF16)
            part = _dot(vt_ref[a, :, j * BLK:(j + 1) * BLK], p)
            acc = part if acc is None else acc + part
        ot = acc[0:dh] * (1.0 / acc[dh:dh + 1])
        o_ref[0, row0:row0 + BLK, a * dh:(a + 1) * dh] = ot.T.astype(o_ref.dtype)

    def attend(first_block):
        tasks = [(c, a) for c in range(first_block, first_block + G) for a in range(HP)]
        ahead = {}
        for t in range(len(tasks) + MOBA_TASK_LOOKAHEAD):
            if t < len(tasks):
                ahead[t] = scores(*tasks[t])
            d = t - MOBA_TASK_LOOKAHEAD
            if d >= 0:
                c, a = tasks[d]
                weighted_values(c, a, *ahead.pop(d), (c - first_block) * BLK)

    if G == nblk:
        attend(0)
    else:
        for g in range(nblk // G):
            pl.when(grp == g)(functools.partial(attend, g * G))


def _moba(z3, slopes, side):
    B, S, _ = z3.shape
    BLK = MOBA_BLOCK
    HP = MOBA_HEADS_PER_STEP
    G = MOBA_QBLOCKS_PER_STEP
    nblk = S // BLK
    w = HP * HEAD_DIM_B
    qc, kc, vc = COL_QB // w, COL_KB // w, COL_VB // w
    n_hp, n_grp = N_H_B // HP, nblk // G
    side_specs, side_shapes = _side_cast_specs(
        side, B * n_hp * n_grp, lambda b, h, g: (b * n_hp + h) * n_grp + g)
    outs = pl.pallas_call(
        _moba_kernel,
        grid=(B, n_hp, n_grp),
        in_specs=[
            pl.BlockSpec(memory_space=pltpu.SMEM),
            pl.BlockSpec((1, S, w), lambda b, h, g: (b, 0, qc + h)),
            pl.BlockSpec((1, S, w), lambda b, h, g: (b, 0, kc + h)),
            pl.BlockSpec((1, S, w), lambda b, h, g: (b, 0, vc + h)),
        ] + side_specs,
        out_specs=[pl.BlockSpec((1, G * BLK, w), lambda b, h, g: (b, g, h))] + side_specs,
        out_shape=[jax.ShapeDtypeStruct((B, S, W_B), BF16)] + side_shapes,
        scratch_shapes=[
            pltpu.VMEM((HP, HEAD_DIM_B + BF16_SUBLANE_TILE, S), BF16),
            pltpu.VMEM((HP, GATE_ROWS, S), F32),
            pltpu.VMEM((HP, 2, BLK, BLK), F32),
        ],
        compiler_params=_params(("arbitrary", "arbitrary", "arbitrary")),
        name="moba",
    )(slopes, z3, z3, z3, *side)
    return outs[0], outs[1:]


def _sigmoid(x):
    return 1.0 / (1.0 + jnp.exp(-x))


def _merge_out_kernel(oa_ref, ob_ref, ga_ref, gb_ref, ba_ref, bb_ref, wa_ref, wb_ref, wo_ref, x_ref,
                      g2_ref, o_ref, u2_ref, mixed_ref, *, chunk):
    oa = oa_ref[...]
    ob = ob_ref[...]
    for c0 in range(0, mixed_ref.shape[1], chunk):
        cs = slice(c0, c0 + chunk)
        a = _dot(oa, wa_ref[:, cs])
        b = _dot(ob, wb_ref[:, cs])
        ga = _sigmoid(ga_ref[:, cs].astype(F32) + ba_ref[:, cs])
        gb = _sigmoid(gb_ref[:, cs].astype(F32) + bb_ref[:, cs])
        mixed_ref[:, cs] = (ga * a + gb * b).astype(mixed_ref.dtype)
    o_ref[...] = x_ref[...] + _dot(mixed_ref[...], wo_ref[...])
    _rms_rows_to_bf16(o_ref, g2_ref, u2_ref)


def _merge_out(oa, ob, gates, ba, bb, wa, wb, wo, x, g2, *, tm=512, chunk=512):
    m, ka = oa.shape
    kb = ob.shape[1]
    d = wa.shape[1]
    resident = pl.Buffered(1)
    return pl.pallas_call(
        functools.partial(_merge_out_kernel, chunk=chunk),
        grid=(m // tm,),
        in_specs=[
            pl.BlockSpec((tm, ka), lambda i: (i, 0)),
            pl.BlockSpec((tm, kb), lambda i: (i, 0)),
            pl.BlockSpec((tm, d), lambda i: (i, 0)),
            pl.BlockSpec((tm, d), lambda i: (i, 1)),
            pl.BlockSpec((1, d), lambda i: (0, 0)),
            pl.BlockSpec((1, d), lambda i: (0, 0)),
            pl.BlockSpec((ka, d), lambda i: (0, 0), pipeline_mode=resident),
            pl.BlockSpec((kb, d), lambda i: (0, 0), pipeline_mode=resident),
            pl.BlockSpec((d, d), lambda i: (0, 0), pipeline_mode=resident),
            pl.BlockSpec((tm, d), lambda i: (i, 0)),
            pl.BlockSpec((1, d), lambda i: (0, 0)),
        ],
        out_specs=[pl.BlockSpec((tm, d), lambda i: (i, 0)), pl.BlockSpec((tm, d), lambda i: (i, 0))],
        out_shape=[jax.ShapeDtypeStruct((m, d), F32), jax.ShapeDtypeStruct((m, d), BF16)],
        scratch_shapes=[pltpu.VMEM((tm, d), BF16)],
        compiler_params=_params(("parallel",)),
        name="merge_out",
    )(oa, ob, gates, gates, ba, bb, wa, wb, wo, x, g2)


def _mm_res_kernel(a_ref, w_ref, r_ref, o_ref):
    o_ref[...] = r_ref[...] + _dot(a_ref[...], w_ref[...])


def _mm_res(a, w, res, *, tm, tn, name):
    m, k = a.shape
    n = w.shape[1]
    return pl.pallas_call(
        _mm_res_kernel,
        grid=(m // tm, n // tn),
        in_specs=[
            pl.BlockSpec((tm, k), lambda i, j: (i, 0)),
            pl.BlockSpec((k, tn), lambda i, j: (0, j)),
            pl.BlockSpec((tm, tn), lambda i, j: (i, j)),
        ],
        out_specs=pl.BlockSpec((tm, tn), lambda i, j: (i, j)),
        out_shape=jax.ShapeDtypeStruct((m, n), F32),
        compiler_params=_params(("parallel", "parallel")),
        name=name,
    )(a, w, res)


def _ffn_up_kernel(u_ref, wg_ref, wu_ref, o_ref):
    u = u_ref[...]
    for c0 in range(0, o_ref.shape[1], V7X_MXU_DIM):
        cs = slice(c0, c0 + V7X_MXU_DIM)
        gate = _dot(u, wg_ref[:, cs])
        up = _dot(u, wu_ref[:, cs])
        o_ref[:, cs] = (gate * _sigmoid(gate) * up).astype(o_ref.dtype)


def _ffn_up(u, wg, wu, *, tm=2048, tn=512):
    m, d = u.shape
    n = wg.shape[1]
    return pl.pallas_call(
        _ffn_up_kernel,
        grid=(m // tm, n // tn),
        in_specs=[
            pl.BlockSpec((tm, d), lambda i, j: (i, 0)),
            pl.BlockSpec((d, tn), lambda i, j: (0, j)),
            pl.BlockSpec((d, tn), lambda i, j: (0, j)),
        ],
        out_specs=pl.BlockSpec((tm, tn), lambda i, j: (i, j)),
        out_shape=jax.ShapeDtypeStruct((m, n), BF16),
        compiler_params=_params(("parallel", "parallel")),
        name="ffn_up",
    )(u, wg, wu)


def kernel(x, norm1_g, w_in, b_gate, q_norm_a, k_norm_a, sinks_a, q_norm_b, k_norm_b,
           w_branch_a, w_branch_b, w_o, norm2_g, w_ffn_gate, w_ffn_up, w_ffn_down):
    B, S, D = x.shape
    depth = w_in.shape[0]
    assert D == D_MODEL and w_in.shape[2] == IN_COLS
    assert S % MOBA_BLOCK == 0 and S % WINDOW == 0
    M = B * S

    slopes_b = jnp.asarray(np.exp2(-8.0 * np.arange(1, N_H_B + 1, dtype=np.float32) / N_H_B), F32)
    swa_bias = _swa_bias_table()
    bd = _block_diag(V7X_MXU_DIM, HEAD_DIM_A, 1.0 / HEAD_DIM_A)

    h = x.reshape(M, D)
    for l in range(depth):
        head_gain = jnp.concatenate([
            jnp.tile(q_norm_a[l], N_Q_A) * (HEAD_DIM_A ** -0.5 * LOG2E),
            jnp.tile(k_norm_a[l], N_KV_A),
            jnp.ones((WKV_A,), F32),
            jnp.tile(q_norm_b[l], N_H_B) * (HEAD_DIM_B ** -0.5 * LOG2E),
            jnp.tile(k_norm_b[l], N_H_B),
            jnp.ones((W_B,), F32),
        ]).reshape(1, COL_GA)
        qkv, u = _qkv_proj(h, norm1_g[l].reshape(1, D), w_in[l, :, :COL_GA].astype(BF16), head_gain, bd,
                           tm=1024, tn=COL_GA // 3)
        gates, (wo_b, wa_b, wb_b) = _gate_proj(u, w_in[l], COL_GA, 2 * D,
                                               [w_o[l], w_branch_a[l], w_branch_b[l]])
        z3 = qkv.reshape(B, S, COL_GA)
        o_a, (wg_b, wu_b) = _swa(z3, sinks_a[l], swa_bias, [w_ffn_gate[l], w_ffn_up[l]])
        o_b, (wd_b,) = _moba(z3, slopes_b, [w_ffn_down[l]])
        h1, u2 = _merge_out(o_a.reshape(M, WQ_A), o_b.reshape(M, W_B), gates,
                            b_gate[l, :D].reshape(1, D), b_gate[l, D:].reshape(1, D),
                            wa_b, wb_b, wo_b, h, norm2_g[l].reshape(1, D))
        act = _ffn_up(u2, wg_b, wu_b)
        h = _mm_res(act, wd_b, h1, tm=1024, tn=512, name="ffn_down")
    return h.reshape(B, S, D)
```

```python
import functools

import numpy as np
import jax
import jax.numpy as jnp
from jax import lax
from jax.experimental import pallas as pl
from jax.experimental.pallas import tpu as pltpu

F32 = jnp.float32
BF16 = jnp.bfloat16

D_MODEL = 2048
HEAD_DIM_A = 64
N_Q_A = 16
N_KV_A = 4
WINDOW = 128
HEAD_DIM_B = 128
N_H_B = 8
MOBA_BLOCK = 256
MOBA_TOPK = 3
RMS_EPS = 1e-6

WQ_A = N_Q_A * HEAD_DIM_A
WKV_A = N_KV_A * HEAD_DIM_A
W_B = N_H_B * HEAD_DIM_B
COL_QA = 0
COL_KA = COL_QA + WQ_A
COL_VA = COL_KA + WKV_A
COL_QB = COL_VA + WKV_A
COL_KB = COL_QB + W_B
COL_VB = COL_KB + W_B
COL_GA = COL_VB + W_B
COL_GB = COL_GA + D_MODEL
IN_COLS = COL_GB + D_MODEL

V7X_LANES = 128
V7X_MXU_DIM = 256
VMEM_LIMIT_BYTES = 56 * 1024 * 1024
NORM_ROW_CHUNK = 64
NORM_UNROLL = 2
BF16_SUBLANE_TILE = 16
SWA_QBLOCKS_PER_STEP = 8
SWA_SCORE_LOOKAHEAD = 8
MOBA_HEADS_PER_STEP = V7X_MXU_DIM // HEAD_DIM_B
MOBA_TASK_LOOKAHEAD = 4
GATE_ROWS = 16
NEG_INF = float("-inf")
LOG2E = 1.4426950408889634


def _params(sem):
    return pltpu.CompilerParams(dimension_semantics=sem, vmem_limit_bytes=VMEM_LIMIT_BYTES)


def _dot(a, b):
    return jnp.dot(a, b, preferred_element_type=F32)


def _dot_nt(a, b):
    return lax.dot_general(a, b, (((1,), (1,)), ((), ())), preferred_element_type=F32)


def _split_bf16(x):
    hi = x.astype(BF16)
    lo = (x - hi.astype(F32)).astype(BF16)
    return hi, lo


def _rms_rows_to_bf16(x_ref, g_ref, u_ref):
    rows = x_ref.shape[0]
    g = g_ref[...]

    def body(c, carry):
        r = pl.multiple_of(c * NORM_ROW_CHUNK, NORM_ROW_CHUNK)
        x = x_ref[pl.ds(r, NORM_ROW_CHUNK), :]
        ms = jnp.mean(x * x, axis=-1, keepdims=True)
        u_ref[pl.ds(r, NORM_ROW_CHUNK), :] = (x * lax.rsqrt(ms + RMS_EPS) * g).astype(BF16)
        return carry

    lax.fori_loop(0, rows // NORM_ROW_CHUNK, body, 0, unroll=NORM_UNROLL)


def _head_norm_kind(col):
    if col < COL_VA:
        return "norm64"
    if COL_QB <= col < COL_VB:
        return "norm128"
    return "copy"


def _qkv_proj_kernel(x_ref, g_ref, w_ref, hg_ref, bd_ref, o_ref, u_ref, *, n_col_tiles):
    j = pl.program_id(1)

    @pl.when(j == 0)
    def _():
        _rms_rows_to_bf16(x_ref, g_ref, u_ref)

    tn = o_ref.shape[1]
    CH = V7X_MXU_DIM

    def finish(z, cs, kind):
        if kind == "norm64":
            inv = lax.rsqrt(_dot((z * z).astype(BF16), bd_ref[...]) + RMS_EPS)
            z = z * inv * hg_ref[:, cs]
        elif kind == "norm128":
            halves = []
            for h0 in range(0, CH, HEAD_DIM_B):
                zh = z[:, h0:h0 + HEAD_DIM_B]
                ms = jnp.mean(zh * zh, axis=-1, keepdims=True)
                halves.append(zh * lax.rsqrt(ms + RMS_EPS))
            z = jnp.concatenate(halves, axis=1) * hg_ref[:, cs]
        o_ref[:, cs] = z.astype(o_ref.dtype)

    def column_tile(jt):
        u = u_ref[...]
        chunks = [slice(c0, c0 + CH) for c0 in range(0, tn, CH)]
        z_next = _dot(u, w_ref[:, chunks[0]])
        for idx, cs in enumerate(chunks):
            z = z_next
            if idx + 1 < len(chunks):
                z_next = _dot(u, w_ref[:, chunks[idx + 1]])
            finish(z, cs, _head_norm_kind(jt * tn + cs.start))

    for jt in range(n_col_tiles):
        pl.when(j == jt)(functools.partial(column_tile, jt))


def _qkv_proj(x, g, w, head_gain, bd, *, tm, tn):
    m, d = x.shape
    n = w.shape[1]
    assert n % tn == 0 and tn % V7X_MXU_DIM == 0
    return pl.pallas_call(
        functools.partial(_qkv_proj_kernel, n_col_tiles=n // tn),
        grid=(m // tm, n // tn),
        in_specs=[
            pl.BlockSpec((tm, d), lambda i, j: (i, 0)),
            pl.BlockSpec((1, d), lambda i, j: (0, 0)),
            pl.BlockSpec((d, tn), lambda i, j: (0, j)),
            pl.BlockSpec((1, tn), lambda i, j: (0, j)),
            pl.BlockSpec((V7X_MXU_DIM, V7X_MXU_DIM), lambda i, j: (0, 0)),
        ],
        out_specs=[
            pl.BlockSpec((tm, tn), lambda i, j: (i, j)),
            pl.BlockSpec((tm, d), lambda i, j: (i, 0)),
        ],
        out_shape=[jax.ShapeDtypeStruct((m, n), BF16), jax.ShapeDtypeStruct((m, d), BF16)],
        compiler_params=_params(("parallel", "arbitrary")),
        name="qkv_proj",
    )(x, g, w, head_gain, bd)


def _side_cast_specs(arrays, n_steps, step_index):
    specs, shapes = [], []
    for w in arrays:
        rows, cols = w.shape
        chunk = rows // n_steps
        assert chunk * n_steps == rows and chunk % BF16_SUBLANE_TILE == 0, (w.shape, n_steps)
        specs.append(pl.BlockSpec((chunk, cols), lambda *g: (step_index(*g), 0)))
        shapes.append(jax.ShapeDtypeStruct(w.shape, BF16))
    return specs, shapes


def _side_cast(refs):
    n = len(refs) // 2
    for src, dst in zip(refs[:n], refs[n:]):
        dst[...] = src[...].astype(dst.dtype)


def _gate_proj_kernel(u_ref, w_ref, *refs):
    n_side = (len(refs) - 1) // 2
    o_ref = refs[n_side]
    o_ref[...] = _dot(u_ref[...], w_ref[...].astype(BF16)).astype(o_ref.dtype)
    _side_cast(refs[:n_side] + refs[n_side + 1:])


def _gate_proj(u, w_all, col0, n, side, *, tm=2048, tn=512):
    m, k = u.shape
    gi, gj = m // tm, n // tn
    assert col0 % tn == 0
    side_specs, side_shapes = _side_cast_specs(side, gi * gj, lambda i, j: i * gj + j)
    outs = pl.pallas_call(
        _gate_proj_kernel,
        grid=(gi, gj),
        in_specs=[
            pl.BlockSpec((tm, k), lambda i, j: (i, 0)),
            pl.BlockSpec((k, tn), lambda i, j: (0, col0 // tn + j)),
        ] + side_specs,
        out_specs=[pl.BlockSpec((tm, tn), lambda i, j: (i, j))] + side_specs,
        out_shape=[jax.ShapeDtypeStruct((m, n), BF16)] + side_shapes,
        compiler_params=_params(("arbitrary", "arbitrary")),
        name="gate_proj",
    )(u, w_all, *side)
    return outs[0], outs[1:]


def _swa_kernel(sinks_ref, q_ref, kc_ref, kp_ref, vc_ref, vp_ref, bias_ref, *refs):
    n_side = (len(refs) - 1) // 2
    o_ref = refs[n_side]
    _side_cast(refs[:n_side] + refs[n_side + 1:])
    L = WINDOW
    QB = SWA_QBLOCKS_PER_STEP
    first_step = (pl.program_id(1) == 0).astype(jnp.int32)
    k = jnp.concatenate([kp_ref[0], kc_ref[0]], axis=0).astype(F32)
    v = jnp.concatenate([vp_ref[0], vc_ref[0]], axis=0).astype(F32)

    lane = lax.broadcasted_iota(jnp.int32, (1, V7X_LANES), 1)
    left = lane < HEAD_DIM_A
    lo_head = lax.broadcasted_iota(jnp.int32, (1, 2 * L), 1) < L

    n_kt = WKV_A // V7X_LANES
    kt = [k[:, t * 128:(t + 1) * 128] for t in range(n_kt)]
    kt_sw = [pltpu.roll(x, HEAD_DIM_A, axis=1).astype(BF16) for x in kt]
    kt = [x.astype(BF16) for x in kt]
    pad_row = lax.broadcasted_iota(jnp.int32, (BF16_SUBLANE_TILE, k.shape[0]), 0)
    ones_row = jnp.where(pad_row == 0, 1.0, 0.0)
    vt = [jnp.concatenate([v[:, t * 128:(t + 1) * 128].T, ones_row], axis=0).astype(BF16) for t in range(n_kt)]

    def q_masked(qb, h):
        tile = h // 2
        qt = q_ref[0, qb * L:(qb + 1) * L, tile * 128:(tile + 1) * 128]
        keep = left if h % 2 == 0 else jnp.logical_not(left)
        return jnp.where(keep, qt, jnp.zeros_like(qt))

    rep = N_Q_A // N_KV_A

    def scores(qb, g, hpar):
        t, par = g // 2, g % 2
        ha, hb = rep * g + hpar, rep * g + hpar + 2
        k_al = (kt if par == hpar else kt_sw)[t][qb * L:(qb + 2) * L]
        qm = jnp.concatenate([q_masked(qb, ha), q_masked(qb, hb)], axis=0)
        first = first_step if qb == 0 else 0
        return _dot_nt(k_al, qm) + bias_ref[first, 2 * g + hpar]

    def weighted_values(qb, g, hpar, s, out_rows):
        t, par = g // 2, g % 2
        ha, hb = rep * g + hpar, rep * g + hpar + 2
        sink = jnp.where(lo_head, sinks_ref[ha], sinks_ref[hb]) * LOG2E
        m = jnp.maximum(jnp.max(s, axis=0, keepdims=True), sink)
        e = jnp.exp2(s - m).astype(BF16)
        ot = _dot(vt[t][:, qb * L:(qb + 2) * L], e)
        denom = ot[V7X_LANES:V7X_LANES + 1, :] + jnp.exp2(sink - m)
        og = ot[par * HEAD_DIM_A:(par + 1) * HEAD_DIM_A, :] * (1.0 / denom)
        out_rows[ha] = og[:, :L]
        out_rows[hb] = og[:, L:]

    tasks = [(qb, g, hpar) for qb in range(QB) for g in range(N_KV_A) for hpar in range(2)]
    ahead = {}
    out_rows = {}
    for t in range(len(tasks) + SWA_SCORE_LOOKAHEAD):
        if t < len(tasks):
            ahead[t] = scores(*tasks[t])
        d = t - SWA_SCORE_LOOKAHEAD
        if d >= 0:
            qb, g, hpar = tasks[d]
            rows = out_rows.setdefault(qb, [None] * N_Q_A)
            weighted_values(qb, g, hpar, ahead.pop(d), rows)
            if (g, hpar) == (N_KV_A - 1, 1):
                o_t = jnp.concatenate(out_rows.pop(qb), axis=0)
                o_ref[0, qb * L:(qb + 1) * L, :] = o_t.T.astype(o_ref.dtype)


def _swa(z3, sinks, bias, side):
    B, S, _ = z3.shape
    L = WINDOW
    QB = SWA_QBLOCKS_PER_STEP
    nb = S // (QB * L)
    kblk = COL_KA // WKV_A
    vblk = COL_VA // WKV_A
    prev = lambda n: jnp.maximum(n * QB - 1, 0)
    side_specs, side_shapes = _side_cast_specs(side, B * nb, lambda b, n: b * nb + n)
    outs = pl.pallas_call(
        _swa_kernel,
        grid=(B, nb),
        in_specs=[
            pl.BlockSpec(memory_space=pltpu.SMEM),
            pl.BlockSpec((1, QB * L, WQ_A), lambda b, n: (b, n, 0)),
            pl.BlockSpec((1, QB * L, WKV_A), lambda b, n: (b, n, kblk)),
            pl.BlockSpec((1, L, WKV_A), lambda b, n: (b, prev(n), kblk)),
            pl.BlockSpec((1, QB * L, WKV_A), lambda b, n: (b, n, vblk)),
            pl.BlockSpec((1, L, WKV_A), lambda b, n: (b, prev(n), vblk)),
            pl.BlockSpec((2, N_Q_A // 2, 2 * L, 2 * L), lambda b, n: (0, 0, 0, 0)),
        ] + side_specs,
        out_specs=[pl.BlockSpec((1, QB * L, WQ_A), lambda b, n: (b, n, 0))] + side_specs,
        out_shape=[jax.ShapeDtypeStruct((B, S, WQ_A), BF16)] + side_shapes,
        compiler_params=_params(("arbitrary", "arbitrary")),
        name="swa",
    )(sinks, z3, z3, z3, z3, z3, bias, *side)
    return outs[0], outs[1:]


def _swa_bias_table():
    L = WINDOW
    rep = N_Q_A // N_KV_A
    slopes = np.exp2(-8.0 * np.arange(1, N_Q_A + 1, dtype=np.float32) / N_Q_A).astype(np.float32)
    kj = np.arange(2 * L)[:, None]
    qi = np.arange(L)[None, :]
    dist = L + qi - kj
    window = (dist >= 0) & (dist < WINDOW)
    table = np.empty((2, N_Q_A // 2, 2 * L, 2 * L), np.float32)
    for first in range(2):
        valid = window & ((kj >= L) if first else True)
        for g in range(N_KV_A):
            for hpar in range(2):
                for a in range(2):
                    h = rep * g + hpar + 2 * a
                    table[first, 2 * g + hpar, :, a * L:(a + 1) * L] = np.where(
                        valid, -(slopes[h] * LOG2E) * dist.astype(np.float32), -np.inf)
    return jnp.asarray(table)


def _block_diag(width, block, value):
    idx = np.arange(width) // block
    return jnp.asarray((idx[:, None] == idx[None, :]).astype(np.float32) * value, dtype=BF16)


def _moba_kernel(slopes_ref, q_ref, k_ref, v_ref, *refs):
    n_side = (len(refs) - 4) // 2
    o_ref = refs[n_side]
    vt_ref, gate_ref, tab_ref = refs[2 * n_side + 1:]
    _side_cast(refs[:n_side] + refs[n_side + 1:2 * n_side + 1])
    BLK = MOBA_BLOCK
    HP = MOBA_HEADS_PER_STEP
    dh = HEAD_DIM_B
    hp = pl.program_id(1)
    S = k_ref.shape[1]
    nblk = S // BLK

    def prepare_tables():
        kc = lax.broadcasted_iota(jnp.int32, (BLK, BLK), 0)
        qr = lax.broadcasted_iota(jnp.int32, (BLK, BLK), 1)
        rel = (qr - kc).astype(F32)
        for a in range(HP):
            slope2 = slopes_ref[hp * HP + a] * LOG2E
            tab_ref[a, 0] = -slope2 * rel
            tab_ref[a, 1] = jnp.where(rel >= 0.0, -slope2 * rel, NEG_INF)

    def prepare_values():
        eye = (lax.broadcasted_iota(jnp.int32, (dh, dh), 0)
               == lax.broadcasted_iota(jnp.int32, (dh, dh), 1)).astype(BF16)
        pad_row = lax.broadcasted_iota(jnp.int32, (BF16_SUBLANE_TILE, S), 0)
        for a in range(HP):
            vt_ref[a, 0:dh, :] = _dot_nt(eye, v_ref[0, :, a * dh:(a + 1) * dh]).astype(BF16)
            vt_ref[a, dh:, :] = jnp.where(pad_row == 0, 1.0, 0.0).astype(BF16)

    def block_means():
        blk = lax.broadcasted_iota(jnp.int32, (GATE_ROWS, S), 0)
        pos = lax.broadcasted_iota(jnp.int32, (GATE_ROWS, S), 1)
        member = jnp.where((pos >= blk * BLK) & (pos < (blk + 1) * BLK), 1.0 / BLK, 0.0).astype(BF16)
        return [_split_bf16(_dot(member, k_ref[0, :, a * dh:(a + 1) * dh])) for a in range(HP)]

    def prepare_gate(kmeans):
        for a, (km_hi, km_lo) in enumerate(kmeans):
            qn = q_ref[0, :, a * dh:(a + 1) * dh]
            gate_ref[a] = _dot_nt(km_hi, qn) + _dot_nt(km_lo, qn)

    def block_max_shifts(c, a):
        slope2 = slopes_ref[hp * HP + a] * LOG2E
        shifts = [-slope2 * float((c - n) * BLK) for n in range(c)]
        if c > MOBA_TOPK:
            gate = gate_ref[a, :, c * BLK:(c + 1) * BLK]
            blk = lax.broadcasted_iota(jnp.int32, (GATE_ROWS, 1), 0)
            past = blk < c
            for n in range(c):
                g_n = gate[n:n + 1, :]
                beats = ((gate > g_n) | ((gate == g_n) & (blk < n))) & past
                rank = jnp.sum(jnp.where(beats, 1.0, 0.0), axis=0, keepdims=True)
                shifts[n] = jnp.where(rank < float(MOBA_TOPK), shifts[n], NEG_INF)
        return shifts

    def scores(c, a, ss, ms):
        for j in range(c + 1):
            s = _dot_nt(k_ref[0, j * BLK:(j + 1) * BLK, a * dh:(a + 1) * dh],
                        q_ref[0, c * BLK:(c + 1) * BLK, a * dh:(a + 1) * dh])
            s = s + tab_ref[a, 1 if j == c else 0]
            ss.append(s)
            ms.append(jnp.max(s, axis=0, keepdims=True))
            yield

    def weighted_values(c, a, ss, ms):
        shifts = block_max_shifts(c, a)
        m = functools.reduce(jnp.maximum, [m_j + sh for m_j, sh in zip(ms, shifts)] + ms[c:])
        acc = None
        for j in range(c + 1):
            p = jnp.exp2(ss[j] - (m - shifts[j] if j < c else m)).astype(BF16)
            part = _dot(vt_ref[a, :, j * BLK:(j + 1) * BLK], p)
            acc = part if acc is None else acc + part
            if j < c:
                yield
        ot = acc[0:dh] * (1.0 / acc[dh:dh + 1])
        o_ref[0, c * BLK:(c + 1) * BLK, a * dh:(a + 1) * dh] = ot.T.astype(o_ref.dtype)
        yield

    prepare_tables()
    kmeans = block_means()
    tasks = [(c, a) for c in range(nblk) for a in range(HP)]
    ahead = {}
    for t in range(len(tasks) + MOBA_TASK_LOOKAHEAD):
        running = []
        if t < len(tasks):
            ahead[t] = ([], [])
            running.append(scores(*tasks[t], *ahead[t]))
        d = t - MOBA_TASK_LOOKAHEAD
        if d >= 0:
            c, a = tasks[d]
            if d == 0:
                prepare_values()
            if (c, a) == (MOBA_TOPK, 0):
                prepare_gate(kmeans)
            running.append(weighted_values(c, a, *ahead.pop(d)))
        while running:
            running = [g for g in running if next(g, StopIteration) is not StopIteration]


def _moba(z3, slopes, side):
    B, S, _ = z3.shape
    BLK = MOBA_BLOCK
    HP = MOBA_HEADS_PER_STEP
    w = HP * HEAD_DIM_B
    qc, kc, vc = COL_QB // w, COL_KB // w, COL_VB // w
    n_hp = N_H_B // HP
    side_specs, side_shapes = _side_cast_specs(side, B * n_hp, lambda b, h: b * n_hp + h)
    outs = pl.pallas_call(
        _moba_kernel,
        grid=(B, n_hp),
        in_specs=[
            pl.BlockSpec(memory_space=pltpu.SMEM),
            pl.BlockSpec((1, S, w), lambda b, h: (b, 0, qc + h)),
            pl.BlockSpec((1, S, w), lambda b, h: (b, 0, kc + h)),
            pl.BlockSpec((1, S, w), lambda b, h: (b, 0, vc + h)),
        ] + side_specs,
        out_specs=[pl.BlockSpec((1, S, w), lambda b, h: (b, 0, h))] + side_specs,
        out_shape=[jax.ShapeDtypeStruct((B, S, W_B), BF16)] + side_shapes,
        scratch_shapes=[
            pltpu.VMEM((HP, HEAD_DIM_B + BF16_SUBLANE_TILE, S), BF16),
            pltpu.VMEM((HP, GATE_ROWS, S), F32),
            pltpu.VMEM((HP, 2, BLK, BLK), F32),
        ],
        compiler_params=_params(("arbitrary", "arbitrary")),
        name="moba",
    )(slopes, z3, z3, z3, *side)
    return outs[0], outs[1:]


def _sigmoid(x):
    return 1.0 / (1.0 + jnp.exp(-x))


def _merge_out_kernel(oa_ref, ob_ref, ga_ref, gb_ref, ba_ref, bb_ref, wa_ref, wb_ref, wo_ref, x_ref,
                      g2_ref, o_ref, u2_ref, mixed_ref, *, chunk):
    oa = oa_ref[...]
    ob = ob_ref[...]
    for c0 in range(0, mixed_ref.shape[1], chunk):
        cs = slice(c0, c0 + chunk)
        a = _dot(oa, wa_ref[:, cs])
        b = _dot(ob, wb_ref[:, cs])
        ga = _sigmoid(ga_ref[:, cs].astype(F32) + ba_ref[:, cs])
        gb = _sigmoid(gb_ref[:, cs].astype(F32) + bb_ref[:, cs])
        mixed_ref[:, cs] = (ga * a + gb * b).astype(mixed_ref.dtype)
    h1 = x_ref[...] + _dot(mixed_ref[...], wo_ref[...])
    o_ref[...] = h1
    inv = lax.rsqrt(jnp.mean(h1 * h1, axis=-1, keepdims=True) + RMS_EPS)
    u2_ref[...] = (h1 * inv * g2_ref[...]).astype(u2_ref.dtype)


def _merge_out(oa, ob, gates, ba, bb, wa, wb, wo, x, g2, *, tm=512, chunk=512):
    m, ka = oa.shape
    kb = ob.shape[1]
    d = wa.shape[1]
    resident = pl.Buffered(1)
    return pl.pallas_call(
        functools.partial(_merge_out_kernel, chunk=chunk),
        grid=(m // tm,),
        in_specs=[
            pl.BlockSpec((tm, ka), lambda i: (i, 0)),
            pl.BlockSpec((tm, kb), lambda i: (i, 0)),
            pl.BlockSpec((tm, d), lambda i: (i, 0)),
            pl.BlockSpec((tm, d), lambda i: (i, 1)),
            pl.BlockSpec((1, d), lambda i: (0, 0)),
            pl.BlockSpec((1, d), lambda i: (0, 0)),
            pl.BlockSpec((ka, d), lambda i: (0, 0), pipeline_mode=resident),
            pl.BlockSpec((kb, d), lambda i: (0, 0), pipeline_mode=resident),
            pl.BlockSpec((d, d), lambda i: (0, 0), pipeline_mode=resident),
            pl.BlockSpec((tm, d), lambda i: (i, 0)),
            pl.BlockSpec((1, d), lambda i: (0, 0)),
        ],
        out_specs=[pl.BlockSpec((tm, d), lambda i: (i, 0)), pl.BlockSpec((tm, d), lambda i: (i, 0))],
        out_shape=[jax.ShapeDtypeStruct((m, d), F32), jax.ShapeDtypeStruct((m, d), BF16)],
        scratch_shapes=[pltpu.VMEM((tm, d), BF16)],
        compiler_params=_params(("parallel",)),
        name="merge_out",
    )(oa, ob, gates, gates, ba, bb, wa, wb, wo, x, g2)


def _mm_res_kernel(a_ref, w_ref, r_ref, o_ref):
    o_ref[...] = r_ref[...] + _dot(a_ref[...], w_ref[...])


def _mm_res(a, w, res, *, tm, tn, name):
    m, k = a.shape
    n = w.shape[1]
    return pl.pallas_call(
        _mm_res_kernel,
        grid=(m // tm, n // tn),
        in_specs=[
            pl.BlockSpec((tm, k), lambda i, j: (i, 0)),
            pl.BlockSpec((k, tn), lambda i, j: (0, j)),
            pl.BlockSpec((tm, tn), lambda i, j: (i, j)),
        ],
        out_specs=pl.BlockSpec((tm, tn), lambda i, j: (i, j)),
        out_shape=jax.ShapeDtypeStruct((m, n), F32),
        compiler_params=_params(("parallel", "parallel")),
        name=name,
    )(a, w, res)


def _ffn_up_kernel(u_ref, wg_ref, wu_ref, o_ref):
    u = u_ref[...]
    for c0 in range(0, o_ref.shape[1], V7X_MXU_DIM):
        cs = slice(c0, c0 + V7X_MXU_DIM)
        gate = _dot(u, wg_ref[:, cs])
        up = _dot(u, wu_ref[:, cs])
        o_ref[:, cs] = (gate * _sigmoid(gate) * up).astype(o_ref.dtype)


def _ffn_up(u, wg, wu, *, tm=2048, tn=512):
    m, d = u.shape
    n = wg.shape[1]
    return pl.pallas_call(
        _ffn_up_kernel,
        grid=(m // tm, n // tn),
        in_specs=[
            pl.BlockSpec((tm, d), lambda i, j: (i, 0)),
            pl.BlockSpec((d, tn), lambda i, j: (0, j)),
            pl.BlockSpec((d, tn), lambda i, j: (0, j)),
        ],
        out_specs=pl.BlockSpec((tm, tn), lambda i, j: (i, j)),
        out_shape=jax.ShapeDtypeStruct((m, n), BF16),
        compiler_params=_params(("parallel", "parallel")),
        name="ffn_up",
    )(u, wg, wu)


def kernel(x, norm1_g, w_in, b_gate, q_norm_a, k_norm_a, sinks_a, q_norm_b, k_norm_b,
           w_branch_a, w_branch_b, w_o, norm2_g, w_ffn_gate, w_ffn_up, w_ffn_down):
    B, S, D = x.shape
    depth = w_in.shape[0]
    assert D == D_MODEL and w_in.shape[2] == IN_COLS
    assert S % MOBA_BLOCK == 0 and S % WINDOW == 0
    M = B * S

    slopes_b = jnp.asarray(np.exp2(-8.0 * np.arange(1, N_H_B + 1, dtype=np.float32) / N_H_B), F32)
    swa_bias = _swa_bias_table()
    bd = _block_diag(V7X_MXU_DIM, HEAD_DIM_A, 1.0 / HEAD_DIM_A)

    h = x.reshape(M, D)
    for l in range(depth):
        head_gain = jnp.concatenate([
            jnp.tile(q_norm_a[l], N_Q_A) * (HEAD_DIM_A ** -0.5 * LOG2E),
            jnp.tile(k_norm_a[l], N_KV_A),
            jnp.ones((WKV_A,), F32),
            jnp.tile(q_norm_b[l], N_H_B) * (HEAD_DIM_B ** -0.5 * LOG2E),
            jnp.tile(k_norm_b[l], N_H_B),
            jnp.ones((W_B,), F32),
        ]).reshape(1, COL_GA)
        qkv, u = _qkv_proj(h, norm1_g[l].reshape(1, D), w_in[l, :, :COL_GA].astype(BF16), head_gain, bd,
                           tm=1024, tn=COL_GA // 3)
        gates, (wo_b, wa_b, wb_b) = _gate_proj(u, w_in[l], COL_GA, 2 * D,
                                               [w_o[l], w_branch_a[l], w_branch_b[l]])
        z3 = qkv.reshape(B, S, COL_GA)
        o_a, (wg_b, wu_b) = _swa(z3, sinks_a[l], swa_bias, [w_ffn_gate[l], w_ffn_up[l]])
        o_b, (wd_b,) = _moba(z3, slopes_b, [w_ffn_down[l]])
        h1, u2 = _merge_out(o_a.reshape(M, WQ_A), o_b.reshape(M, W_B), gates,
                            b_gate[l, :D].reshape(1, D), b_gate[l, D:].reshape(1, D),
                            wa_b, wb_b, wo_b, h, norm2_g[l].reshape(1, D))
        act = _ffn_up(u2, wg_b, wu_b)
        h = _mm_res(act, wd_b, h1, tm=1024, tn=512, name="ffn_down")
    return h.reshape(B, S, D)
```

```python
import functools

import numpy as np
import jax
import jax.numpy as jnp
from jax import lax
from jax.experimental import pallas as pl
from jax.experimental.pallas import tpu as pltpu

F32 = jnp.float32
BF16 = jnp.bfloat16

D_MODEL = 2048
HEAD_DIM_A = 64
N_Q_A = 16
N_KV_A = 4
WINDOW = 128
HEAD_DIM_B = 128
N_H_B = 8
MOBA_BLOCK = 256
MOBA_TOPK = 3
RMS_EPS = 1e-6

WQ_A = N_Q_A * HEAD_DIM_A
WKV_A = N_KV_A * HEAD_DIM_A
W_B = N_H_B * HEAD_DIM_B
COL_QA = 0
COL_KA = COL_QA + WQ_A
COL_VA = COL_KA + WKV_A
COL_QB = COL_VA + WKV_A
COL_KB = COL_QB + W_B
COL_VB = COL_KB + W_B
COL_GA = COL_VB + W_B
COL_GB = COL_GA + D_MODEL
IN_COLS = COL_GB + D_MODEL

V7X_LANES = 128
V7X_MXU_DIM = 256
VMEM_LIMIT_BYTES = 56 * 1024 * 1024
NORM_ROW_CHUNK = 64
NORM_UNROLL = 2
BF16_SUBLANE_TILE = 16
SIDE_CAST_COL_TILE = 512
SWA_QBLOCKS_PER_STEP = 8
SWA_SCORE_LOOKAHEAD = 8
MOBA_HEADS_PER_STEP = 2
MOBA_TASK_LOOKAHEAD = 4
GATE_ROWS = 16
NEG_INF = float("-inf")
LOG2E = 1.4426950408889634


def _params(sem):
    return pltpu.CompilerParams(dimension_semantics=sem, vmem_limit_bytes=VMEM_LIMIT_BYTES)


def _dot(a, b):
    return jnp.dot(a, b, preferred_element_type=F32)


def _dot_nt(a, b):
    return lax.dot_general(a, b, (((1,), (1,)), ((), ())), preferred_element_type=F32)


def _split_bf16(x):
    hi = x.astype(BF16)
    lo = (x - hi.astype(F32)).astype(BF16)
    return hi, lo


def _rms_rows_to_bf16(x_ref, g_ref, u_ref):
    rows = x_ref.shape[0]
    g = g_ref[...]

    def body(c, carry):
        r = pl.multiple_of(c * NORM_ROW_CHUNK, NORM_ROW_CHUNK)
        x = x_ref[pl.ds(r, NORM_ROW_CHUNK), :]
        ms = jnp.mean(x * x, axis=-1, keepdims=True)
        u_ref[pl.ds(r, NORM_ROW_CHUNK), :] = (x * lax.rsqrt(ms + RMS_EPS) * g).astype(BF16)
        return carry

    lax.fori_loop(0, rows // NORM_ROW_CHUNK, body, 0, unroll=NORM_UNROLL)


def _head_norm_kind(col):
    if col < COL_VA:
        return "norm64"
    if COL_QB <= col < COL_VB:
        return "norm128"
    return "copy"


def _qkv_proj_kernel(x_ref, g_ref, w_ref, hg_ref, bd_ref, o_ref, u_ref):
    _rms_rows_to_bf16(x_ref, g_ref, u_ref)
    CH = V7X_MXU_DIM

    def finish(z, cs, kind):
        if kind == "norm64":
            inv = lax.rsqrt(_dot((z * z).astype(BF16), bd_ref[...]) + RMS_EPS)
            z = z * inv * hg_ref[:, cs]
        elif kind == "norm128":
            halves = []
            for h0 in range(0, CH, HEAD_DIM_B):
                zh = z[:, h0:h0 + HEAD_DIM_B]
                ms = jnp.mean(zh * zh, axis=-1, keepdims=True)
                halves.append(zh * lax.rsqrt(ms + RMS_EPS))
            z = jnp.concatenate(halves, axis=1) * hg_ref[:, cs]
        o_ref[:, cs] = z.astype(o_ref.dtype)

    u = u_ref[...]
    chunks = [slice(c0, c0 + CH) for c0 in range(0, o_ref.shape[1], CH)]
    z_next = _dot(u, w_ref[:, chunks[0]])
    for idx, cs in enumerate(chunks):
        z = z_next
        if idx + 1 < len(chunks):
            z_next = _dot(u, w_ref[:, chunks[idx + 1]])
        finish(z, cs, _head_norm_kind(cs.start))


def _qkv_proj(x, g, w, head_gain, bd, *, tm):
    m, d = x.shape
    n = w.shape[1]
    assert n % V7X_MXU_DIM == 0
    resident = pl.Buffered(1)
    return pl.pallas_call(
        _qkv_proj_kernel,
        grid=(m // tm,),
        in_specs=[
            pl.BlockSpec((tm, d), lambda i: (i, 0)),
            pl.BlockSpec((1, d), lambda i: (0, 0)),
            pl.BlockSpec((d, n), lambda i: (0, 0), pipeline_mode=resident),
            pl.BlockSpec((1, n), lambda i: (0, 0)),
            pl.BlockSpec((V7X_MXU_DIM, V7X_MXU_DIM), lambda i: (0, 0)),
        ],
        out_specs=[
            pl.BlockSpec((tm, n), lambda i: (i, 0)),
            pl.BlockSpec((tm, d), lambda i: (i, 0)),
        ],
        out_shape=[jax.ShapeDtypeStruct((m, n), BF16), jax.ShapeDtypeStruct((m, d), BF16)],
        compiler_params=_params(("parallel",)),
        name="qkv_proj",
    )(x, g, w, head_gain, bd)


def _side_cast_specs(arrays, n_steps, step_index):
    in_specs, out_specs, shapes = [], [], []
    for item in arrays:
        w, col0, ncols, col_tile = item if isinstance(item, tuple) else (item, 0, item.shape[1], item.shape[1])
        rows = w.shape[0]
        n_col = ncols // col_tile
        n_row = n_steps // n_col
        chunk = rows // n_row
        assert n_col * col_tile == ncols and n_row * n_col == n_steps and col0 % col_tile == 0, (w.shape, n_steps)
        assert chunk * n_row == rows and chunk % BF16_SUBLANE_TILE == 0, (w.shape, n_steps)

        def in_map(*g, n_col=n_col, c0=col0 // col_tile):
            s = step_index(*g)
            return (s // n_col, c0 + s % n_col)

        def out_map(*g, n_col=n_col):
            s = step_index(*g)
            return (s // n_col, s % n_col)

        in_specs.append(pl.BlockSpec((chunk, col_tile), in_map))
        out_specs.append(pl.BlockSpec((chunk, col_tile), out_map))
        shapes.append(jax.ShapeDtypeStruct((rows, ncols), BF16))
    return in_specs, out_specs, shapes


def _side_cast(refs):
    n = len(refs) // 2
    for src, dst in zip(refs[:n], refs[n:]):
        dst[...] = src[...].astype(dst.dtype)


def _gate_proj_kernel(u_ref, w_ref, *refs):
    n_side = (len(refs) - 1) // 2
    o_ref = refs[n_side]
    o_ref[...] = _dot(u_ref[...], w_ref[...]).astype(o_ref.dtype)
    _side_cast(refs[:n_side] + refs[n_side + 1:])


def _gate_proj(u, w, side, *, tm=2048, tn=1024):
    m, k = u.shape
    n = w.shape[1]
    gi, gj = m // tm, n // tn
    side_in, side_out, side_shapes = _side_cast_specs(side, gi * gj, lambda i, j: i * gj + j)
    outs = pl.pallas_call(
        _gate_proj_kernel,
        grid=(gi, gj),
        in_specs=[
            pl.BlockSpec((tm, k), lambda i, j: (i, 0)),
            pl.BlockSpec((k, tn), lambda i, j: (0, j)),
        ] + side_in,
        out_specs=[pl.BlockSpec((tm, tn), lambda i, j: (i, j))] + side_out,
        out_shape=[jax.ShapeDtypeStruct((m, n), BF16)] + side_shapes,
        compiler_params=_params(("arbitrary", "arbitrary")),
        name="gate_proj",
    )(u, w, *side)
    return outs[0], outs[1:]


def _swa_kernel(sinks_ref, q_ref, kc_ref, kp_ref, vc_ref, vp_ref, bias_ref, *refs):
    n_side = (len(refs) - 1) // 2
    o_ref = refs[n_side]
    _side_cast(refs[:n_side] + refs[n_side + 1:])
    L = WINDOW
    QB = SWA_QBLOCKS_PER_STEP
    first_step = (pl.program_id(1) == 0).astype(jnp.int32)
    k = jnp.concatenate([kp_ref[0], kc_ref[0]], axis=0).astype(F32)
    v = jnp.concatenate([vp_ref[0], vc_ref[0]], axis=0).astype(F32)

    lane = lax.broadcasted_iota(jnp.int32, (1, V7X_LANES), 1)
    left = lane < HEAD_DIM_A
    lo_head = lax.broadcasted_iota(jnp.int32, (1, 2 * L), 1) < L

    n_kt = WKV_A // V7X_LANES
    kt = [k[:, t * 128:(t + 1) * 128] for t in range(n_kt)]
    kt_sw = [pltpu.roll(x, HEAD_DIM_A, axis=1).astype(BF16) for x in kt]
    kt = [x.astype(BF16) for x in kt]
    pad_row = lax.broadcasted_iota(jnp.int32, (BF16_SUBLANE_TILE, k.shape[0]), 0)
    ones_row = jnp.where(pad_row == 0, 1.0, 0.0)
    vt = [jnp.concatenate([v[:, t * 128:(t + 1) * 128].T, ones_row], axis=0).astype(BF16) for t in range(n_kt)]

    def q_masked(qb, h):
        tile = h // 2
        qt = q_ref[0, qb * L:(qb + 1) * L, tile * 128:(tile + 1) * 128]
        keep = left if h % 2 == 0 else jnp.logical_not(left)
        return jnp.where(keep, qt, jnp.zeros_like(qt))

    rep = N_Q_A // N_KV_A

    def scores(qb, g, hpar):
        t, par = g // 2, g % 2
        ha, hb = rep * g + hpar, rep * g + hpar + 2
        k_al = (kt if par == hpar else kt_sw)[t][qb * L:(qb + 2) * L]
        qm = jnp.concatenate([q_masked(qb, ha), q_masked(qb, hb)], axis=0)
        first = first_step if qb == 0 else 0
        return _dot_nt(k_al, qm) + bias_ref[first, 2 * g + hpar]

    def weighted_values(qb, g, hpar, s, out_rows):
        t, par = g // 2, g % 2
        ha, hb = rep * g + hpar, rep * g + hpar + 2
        sink = jnp.where(lo_head, sinks_ref[ha], sinks_ref[hb]) * LOG2E
        m = jnp.maximum(jnp.max(s, axis=0, keepdims=True), sink)
        e = jnp.exp2(s - m).astype(BF16)
        ot = _dot(vt[t][:, qb * L:(qb + 2) * L], e)
        denom = ot[V7X_LANES:V7X_LANES + 1, :] + jnp.exp2(sink - m)
        og = ot[par * HEAD_DIM_A:(par + 1) * HEAD_DIM_A, :] * (1.0 / denom)
        out_rows[ha] = og[:, :L]
        out_rows[hb] = og[:, L:]

    tasks = [(qb, g, hpar) for qb in range(QB) for g in range(N_KV_A) for hpar in range(2)]
    ahead = {}
    out_rows = {}
    for t in range(len(tasks) + SWA_SCORE_LOOKAHEAD):
        if t < len(tasks):
            ahead[t] = scores(*tasks[t])
        d = t - SWA_SCORE_LOOKAHEAD
        if d >= 0:
            qb, g, hpar = tasks[d]
            rows = out_rows.setdefault(qb, [None] * N_Q_A)
            weighted_values(qb, g, hpar, ahead.pop(d), rows)
            if (g, hpar) == (N_KV_A - 1, 1):
                o_t = jnp.concatenate(out_rows.pop(qb), axis=0)
                o_ref[0, qb * L:(qb + 1) * L, :] = o_t.T.astype(o_ref.dtype)


def _swa(z3, sinks, bias, side):
    B, S, _ = z3.shape
    L = WINDOW
    QB = SWA_QBLOCKS_PER_STEP
    nb = S // (QB * L)
    kblk = COL_KA // WKV_A
    vblk = COL_VA // WKV_A
    prev = lambda n: jnp.maximum(n * QB - 1, 0)
    side_in, side_out, side_shapes = _side_cast_specs(side, B * nb, lambda b, n: b * nb + n)
    outs = pl.pallas_call(
        _swa_kernel,
        grid=(B, nb),
        in_specs=[
            pl.BlockSpec(memory_space=pltpu.SMEM),
            pl.BlockSpec((1, QB * L, WQ_A), lambda b, n: (b, n, 0)),
            pl.BlockSpec((1, QB * L, WKV_A), lambda b, n: (b, n, kblk)),
            pl.BlockSpec((1, L, WKV_A), lambda b, n: (b, prev(n), kblk)),
            pl.BlockSpec((1, QB * L, WKV_A), lambda b, n: (b, n, vblk)),
            pl.BlockSpec((1, L, WKV_A), lambda b, n: (b, prev(n), vblk)),
            pl.BlockSpec((2, N_Q_A // 2, 2 * L, 2 * L), lambda b, n: (0, 0, 0, 0)),
        ] + side_in,
        out_specs=[pl.BlockSpec((1, QB * L, WQ_A), lambda b, n: (b, n, 0))] + side_out,
        out_shape=[jax.ShapeDtypeStruct((B, S, WQ_A), BF16)] + side_shapes,
        compiler_params=_params(("arbitrary", "arbitrary")),
        name="swa",
    )(sinks, z3, z3, z3, z3, z3, bias, *side)
    return outs[0], outs[1:]


def _swa_bias_table():
    L = WINDOW
    rep = N_Q_A // N_KV_A
    slopes = np.exp2(-8.0 * np.arange(1, N_Q_A + 1, dtype=np.float32) / N_Q_A).astype(np.float32)
    kj = np.arange(2 * L)[:, None]
    qi = np.arange(L)[None, :]
    dist = L + qi - kj
    window = (dist >= 0) & (dist < WINDOW)
    table = np.empty((2, N_Q_A // 2, 2 * L, 2 * L), np.float32)
    for first in range(2):
        valid = window & ((kj >= L) if first else True)
        for g in range(N_KV_A):
            for hpar in range(2):
                for a in range(2):
                    h = rep * g + hpar + 2 * a
                    table[first, 2 * g + hpar, :, a * L:(a + 1) * L] = np.where(
                        valid, -(slopes[h] * LOG2E) * dist.astype(np.float32), -np.inf)
    return jnp.asarray(table)


def _block_diag(width, block, value):
    idx = np.arange(width) // block
    return jnp.asarray((idx[:, None] == idx[None, :]).astype(np.float32) * value, dtype=BF16)


def _moba_kernel(slopes_ref, q_ref, k_ref, v_ref, *refs):
    n_side = (len(refs) - 4) // 2
    o_ref = refs[n_side]
    vt_ref, gate_ref, tab_ref = refs[2 * n_side + 1:]
    _side_cast(refs[:n_side] + refs[n_side + 1:2 * n_side + 1])
    BLK = MOBA_BLOCK
    HP = MOBA_HEADS_PER_STEP
    dh = HEAD_DIM_B
    hp = pl.program_id(1)
    S = k_ref.shape[1]
    nblk = S // BLK

    def prepare_tables():
        kc = lax.broadcasted_iota(jnp.int32, (BLK, BLK), 0)
        qr = lax.broadcasted_iota(jnp.int32, (BLK, BLK), 1)
        rel = (qr - kc).astype(F32)
        for a in range(HP):
            slope2 = slopes_ref[hp * HP + a] * LOG2E
            tab_ref[a, 0] = -slope2 * rel
            tab_ref[a, 1] = jnp.where(rel >= 0.0, -slope2 * rel, NEG_INF)

    def prepare_values():
        eye = (lax.broadcasted_iota(jnp.int32, (dh, dh), 0)
               == lax.broadcasted_iota(jnp.int32, (dh, dh), 1)).astype(BF16)
        pad_row = lax.broadcasted_iota(jnp.int32, (BF16_SUBLANE_TILE, S), 0)
        for a in range(HP):
            vt_ref[a, 0:dh, :] = _dot_nt(eye, v_ref[0, :, a * dh:(a + 1) * dh]).astype(BF16)
            vt_ref[a, dh:, :] = jnp.where(pad_row == 0, 1.0, 0.0).astype(BF16)

    def block_means():
        blk = lax.broadcasted_iota(jnp.int32, (GATE_ROWS, S), 0)
        pos = lax.broadcasted_iota(jnp.int32, (GATE_ROWS, S), 1)
        member = jnp.where((pos >= blk * BLK) & (pos < (blk + 1) * BLK), 1.0 / BLK, 0.0).astype(BF16)
        return [_split_bf16(_dot(member, k_ref[0, :, a * dh:(a + 1) * dh])) for a in range(HP)]

    def prepare_gate(kmeans):
        for a, (km_hi, km_lo) in enumerate(kmeans):
            qn = q_ref[0, :, a * dh:(a + 1) * dh]
            gate_ref[a] = _dot_nt(km_hi, qn) + _dot_nt(km_lo, qn)

    def block_max_shifts(c, a):
        slope2 = slopes_ref[hp * HP + a] * LOG2E
        shifts = [-slope2 * float((c - n) * BLK) for n in range(c)]
        if c > MOBA_TOPK:
            gate = gate_ref[a, :, c * BLK:(c + 1) * BLK]
            blk = lax.broadcasted_iota(jnp.int32, (GATE_ROWS, 1), 0)
            past = blk < c
            for n in range(c):
                g_n = gate[n:n + 1, :]
                beats = ((gate > g_n) | ((gate == g_n) & (blk < n))) & past
                rank = jnp.sum(jnp.where(beats, 1.0, 0.0), axis=0, keepdims=True)
                shifts[n] = jnp.where(rank < float(MOBA_TOPK), shifts[n], NEG_INF)
        return shifts

    def scores(c, a, ss, ms):
        for j in range(c + 1):
            s = _dot_nt(k_ref[0, j * BLK:(j + 1) * BLK, a * dh:(a + 1) * dh],
                        q_ref[0, c * BLK:(c + 1) * BLK, a * dh:(a + 1) * dh])
            s = s + tab_ref[a, 1 if j == c else 0]
            ss.append(s)
            ms.append(jnp.max(s, axis=0, keepdims=True))
            yield

    def weighted_values(c, a, ss, ms):
        shifts = block_max_shifts(c, a)
        m = functools.reduce(jnp.maximum, [m_j + sh for m_j, sh in zip(ms, shifts)] + ms[c:])
        acc = None
        for j in range(c + 1):
            p = jnp.exp2(ss[j] - (m - shifts[j] if j < c else m)).astype(BF16)
            part = _dot(vt_ref[a, :, j * BLK:(j + 1) * BLK], p)
            acc = part if acc is None else acc + part
            if j < c:
                yield
        ot = acc[0:dh] * (1.0 / acc[dh:dh + 1])
        o_ref[0, c * BLK:(c + 1) * BLK, a * dh:(a + 1) * dh] = ot.T.astype(o_ref.dtype)
        yield

    prepare_tables()
    kmeans = block_means()
    tasks = [(c, a) for c in range(nblk) for a in range(HP)]
    ahead = {}
    for t in range(len(tasks) + MOBA_TASK_LOOKAHEAD):
        running = []
        if t < len(tasks):
            ahead[t] = ([], [])
            running.append(scores(*tasks[t], *ahead[t]))
        d = t - MOBA_TASK_LOOKAHEAD
        if d >= 0:
            c, a = tasks[d]
            if d == 0:
                prepare_values()
            if (c, a) == (MOBA_TOPK, 0):
                prepare_gate(kmeans)
            running.append(weighted_values(c, a, *ahead.pop(d)))
        while running:
            running = [g for g in running if next(g, StopIteration) is not StopIteration]


def _moba(z3, slopes, side):
    B, S, _ = z3.shape
    BLK = MOBA_BLOCK
    HP = MOBA_HEADS_PER_STEP
    w = HP * HEAD_DIM_B
    qc, kc, vc = COL_QB // w, COL_KB // w, COL_VB // w
    n_hp = N_H_B // HP
    side_in, side_out, side_shapes = _side_cast_specs(side, B * n_hp, lambda b, h: b * n_hp + h)
    outs = pl.pallas_call(
        _moba_kernel,
        grid=(B, n_hp),
        in_specs=[
            pl.BlockSpec(memory_space=pltpu.SMEM),
            pl.BlockSpec((1, S, w), lambda b, h: (b, 0, qc + h)),
            pl.BlockSpec((1, S, w), lambda b, h: (b, 0, kc + h)),
            pl.BlockSpec((1, S, w), lambda b, h: (b, 0, vc + h)),
        ] + side_in,
        out_specs=[pl.BlockSpec((1, S, w), lambda b, h: (b, 0, h))] + side_out,
        out_shape=[jax.ShapeDtypeStruct((B, S, W_B), BF16)] + side_shapes,
        scratch_shapes=[
            pltpu.VMEM((HP, HEAD_DIM_B + BF16_SUBLANE_TILE, S), BF16),
            pltpu.VMEM((HP, GATE_ROWS, S), F32),
            pltpu.VMEM((HP, 2, BLK, BLK), F32),
        ],
        compiler_params=_params(("arbitrary", "arbitrary")),
        name="moba",
    )(slopes, z3, z3, z3, *[s[0] if isinstance(s, tuple) else s for s in side])
    return outs[0], outs[1:]


def _sigmoid(x):
    return 1.0 / (1.0 + jnp.exp(-x))


def _merge_out_kernel(oa_ref, ob_ref, ga_ref, gb_ref, ba_ref, bb_ref, wa_ref, wb_ref, wo_ref, x_ref,
                      g2_ref, o_ref, u2_ref, mixed_ref, *, chunk):
    oa = oa_ref[...]
    ob = ob_ref[...]
    for c0 in range(0, mixed_ref.shape[1], chunk):
        cs = slice(c0, c0 + chunk)
        a = _dot(oa, wa_ref[:, cs])
        b = _dot(ob, wb_ref[:, cs])
        ga = _sigmoid(ga_ref[:, cs].astype(F32) + ba_ref[:, cs])
        gb = _sigmoid(gb_ref[:, cs].astype(F32) + bb_ref[:, cs])
        mixed_ref[:, cs] = (ga * a + gb * b).astype(mixed_ref.dtype)
    h1 = x_ref[...] + _dot(mixed_ref[...], wo_ref[...])
    o_ref[...] = h1
    inv = lax.rsqrt(jnp.mean(h1 * h1, axis=-1, keepdims=True) + RMS_EPS)
    u2_ref[...] = (h1 * inv * g2_ref[...]).astype(u2_ref.dtype)


def _merge_out(oa, ob, gates, ba, bb, wa, wb, wo, x, g2, *, tm=512, chunk=512):
    m, ka = oa.shape
    kb = ob.shape[1]
    d = wa.shape[1]
    resident = pl.Buffered(1)
    return pl.pallas_call(
        functools.partial(_merge_out_kernel, chunk=chunk),
        grid=(m // tm,),
        in_specs=[
            pl.BlockSpec((tm, ka), lambda i: (i, 0)),
            pl.BlockSpec((tm, kb), lambda i: (i, 0)),
            pl.BlockSpec((tm, d), lambda i: (i, 0)),
            pl.BlockSpec((tm, d), lambda i: (i, 1)),
            pl.BlockSpec((1, d), lambda i: (0, 0)),
            pl.BlockSpec((1, d), lambda i: (0, 0)),
            pl.BlockSpec((ka, d), lambda i: (0, 0), pipeline_mode=resident),
            pl.BlockSpec((kb, d), lambda i: (0, 0), pipeline_mode=resident),
            pl.BlockSpec((d, d), lambda i: (0, 0), pipeline_mode=resident),
            pl.BlockSpec((tm, d), lambda i: (i, 0)),
            pl.BlockSpec((1, d), lambda i: (0, 0)),
        ],
        out_specs=[pl.BlockSpec((tm, d), lambda i: (i, 0)), pl.BlockSpec((tm, d), lambda i: (i, 0))],
        out_shape=[jax.ShapeDtypeStruct((m, d), F32), jax.ShapeDtypeStruct((m, d), BF16)],
        scratch_shapes=[pltpu.VMEM((tm, d), BF16)],
        compiler_params=_params(("parallel",)),
        name="merge_out",
    )(oa, ob, gates, gates, ba, bb, wa, wb, wo, x, g2)


def _mm_res_kernel(a_ref, w_ref, r_ref, o_ref):
    o_ref[...] = r_ref[...] + _dot(a_ref[...], w_ref[...])


def _mm_res(a, w, res, *, tm, tn, name):
    m, k = a.shape
    n = w.shape[1]
    return pl.pallas_call(
        _mm_res_kernel,
        grid=(m // tm, n // tn),
        in_specs=[
            pl.BlockSpec((tm, k), lambda i, j: (i, 0)),
            pl.BlockSpec((k, tn), lambda i, j: (0, j)),
            pl.BlockSpec((tm, tn), lambda i, j: (i, j)),
        ],
        out_specs=pl.BlockSpec((tm, tn), lambda i, j: (i, j)),
        out_shape=jax.ShapeDtypeStruct((m, n), F32),
        compiler_params=_params(("parallel", "parallel")),
        name=name,
    )(a, w, res)


def _ffn_up_kernel(u_ref, wg_ref, wu_ref, o_ref):
    u = u_ref[...]
    for c0 in range(0, o_ref.shape[1], V7X_MXU_DIM):
        cs = slice(c0, c0 + V7X_MXU_DIM)
        gate = _dot(u, wg_ref[:, cs])
        up = _dot(u, wu_ref[:, cs])
        o_ref[:, cs] = (gate * _sigmoid(gate) * up).astype(o_ref.dtype)


def _ffn_up(u, wg, wu, *, tm=2048, tn=512):
    m, d = u.shape
    n = wg.shape[1]
    return pl.pallas_call(
        _ffn_up_kernel,
        grid=(m // tm, n // tn),
        in_specs=[
            pl.BlockSpec((tm, d), lambda i, j: (i, 0)),
            pl.BlockSpec((d, tn), lambda i, j: (0, j)),
            pl.BlockSpec((d, tn), lambda i, j: (0, j)),
        ],
        out_specs=pl.BlockSpec((tm, tn), lambda i, j: (i, j)),
        out_shape=jax.ShapeDtypeStruct((m, n), BF16),
        compiler_params=_params(("parallel", "parallel")),
        name="ffn_up",
    )(u, wg, wu)


def kernel(x, norm1_g, w_in, b_gate, q_norm_a, k_norm_a, sinks_a, q_norm_b, k_norm_b,
           w_branch_a, w_branch_b, w_o, norm2_g, w_ffn_gate, w_ffn_up, w_ffn_down):
    B, S, D = x.shape
    depth = w_in.shape[0]
    assert D == D_MODEL and w_in.shape[2] == IN_COLS
    assert S % MOBA_BLOCK == 0 and S % WINDOW == 0
    M = B * S

    slopes_b = jnp.asarray(np.exp2(-8.0 * np.arange(1, N_H_B + 1, dtype=np.float32) / N_H_B), F32)
    swa_bias = _swa_bias_table()
    bd = _block_diag(V7X_MXU_DIM, HEAD_DIM_A, 1.0 / HEAD_DIM_A)

    h = x.reshape(M, D)
    for l in range(depth):
        head_gain = jnp.concatenate([
            jnp.tile(q_norm_a[l], N_Q_A) * (HEAD_DIM_A ** -0.5 * LOG2E),
            jnp.tile(k_norm_a[l], N_KV_A),
            jnp.ones((WKV_A,), F32),
            jnp.tile(q_norm_b[l], N_H_B) * (HEAD_DIM_B ** -0.5 * LOG2E),
            jnp.tile(k_norm_b[l], N_H_B),
            jnp.ones((W_B,), F32),
        ]).reshape(1, COL_GA)
        qkv, u = _qkv_proj(h, norm1_g[l].reshape(1, D), w_in[l, :, :COL_GA].astype(BF16), head_gain, bd,
                           tm=512)
        z3 = qkv.reshape(B, S, COL_GA)
        o_a, (wg_b, wu_b) = _swa(z3, sinks_a[l], swa_bias, [w_ffn_gate[l], w_ffn_up[l]])
        o_b, (wd_b, wgate_b) = _moba(z3, slopes_b,
                                     [w_ffn_down[l], (w_in[l], COL_GA, 2 * D, SIDE_CAST_COL_TILE)])
        gates, (wo_b, wa_b, wb_b) = _gate_proj(u, wgate_b, [w_o[l], w_branch_a[l], w_branch_b[l]])
        h1, u2 = _merge_out(o_a.reshape(M, WQ_A), o_b.reshape(M, W_B), gates,
                            b_gate[l, :D].reshape(1, D), b_gate[l, D:].reshape(1, D),
                            wa_b, wb_b, wo_b, h, norm2_g[l].reshape(1, D))
        act = _ffn_up(u2, wg_b, wu_b)
        h = _mm_res(act, wd_b, h1, tm=1024, tn=512, name="ffn_down")
    return h.reshape(B, S, D)
```

```python
import functools

import numpy as np
import jax
import jax.numpy as jnp
from jax import lax
from jax.experimental import pallas as pl
from jax.experimental.pallas import tpu as pltpu

F32 = jnp.float32
BF16 = jnp.bfloat16

D_MODEL = 2048
HEAD_DIM_A = 64
N_Q_A = 16
N_KV_A = 4
WINDOW = 128
HEAD_DIM_B = 128
N_H_B = 8
MOBA_BLOCK = 256
MOBA_TOPK = 3
RMS_EPS = 1e-6

WQ_A = N_Q_A * HEAD_DIM_A
WKV_A = N_KV_A * HEAD_DIM_A
W_B = N_H_B * HEAD_DIM_B
COL_QA = 0
COL_KA = COL_QA + WQ_A
COL_VA = COL_KA + WKV_A
COL_QB = COL_VA + WKV_A
COL_KB = COL_QB + W_B
COL_VB = COL_KB + W_B
COL_GA = COL_VB + W_B
COL_GB = COL_GA + D_MODEL
IN_COLS = COL_GB + D_MODEL

V7X_LANES = 128
V7X_MXU_DIM = 256
VMEM_LIMIT_BYTES = 56 * 1024 * 1024
NORM_ROW_CHUNK = 64
NORM_UNROLL = 2
BF16_SUBLANE_TILE = 16
FFN_UP_WIDE_TILE = 1024
SIDE_CAST_COL_TILE = 512
SWA_QBLOCKS_PER_STEP = 8
SWA_SCORE_LOOKAHEAD = 8
MOBA_HEADS_PER_STEP = 2
MOBA_TASK_LOOKAHEAD = 4
GATE_ROWS = 16
NEG_INF = float("-inf")
LOG2E = 1.4426950408889634


def _params(sem):
    return pltpu.CompilerParams(dimension_semantics=sem, vmem_limit_bytes=VMEM_LIMIT_BYTES)


def _dot(a, b):
    return jnp.dot(a, b, preferred_element_type=F32)


def _dot_nt(a, b):
    return lax.dot_general(a, b, (((1,), (1,)), ((), ())), preferred_element_type=F32)


def _split_bf16(x):
    hi = x.astype(BF16)
    lo = (x - hi.astype(F32)).astype(BF16)
    return hi, lo


def _rms_rows_to_bf16(x_ref, g_ref, u_ref):
    rows = x_ref.shape[0]
    g = g_ref[...]

    def body(c, carry):
        r = pl.multiple_of(c * NORM_ROW_CHUNK, NORM_ROW_CHUNK)
        x = x_ref[pl.ds(r, NORM_ROW_CHUNK), :]
        ms = jnp.mean(x * x, axis=-1, keepdims=True)
        u_ref[pl.ds(r, NORM_ROW_CHUNK), :] = (x * lax.rsqrt(ms + RMS_EPS) * g).astype(BF16)
        return carry

    lax.fori_loop(0, rows // NORM_ROW_CHUNK, body, 0, unroll=NORM_UNROLL)


def _head_norm_kind(col):
    if col < COL_VA:
        return "norm64"
    if COL_QB <= col < COL_VB:
        return "norm128"
    return "copy"


def _qkv_proj_kernel(x_ref, g_ref, w_ref, hg_ref, bd_ref, o_ref, u_ref):
    _rms_rows_to_bf16(x_ref, g_ref, u_ref)
    CH = V7X_MXU_DIM

    def finish(z, cs, kind):
        if kind == "norm64":
            inv = lax.rsqrt(_dot((z * z).astype(BF16), bd_ref[...]) + RMS_EPS)
            z = z * inv * hg_ref[:, cs]
        elif kind == "norm128":
            halves = []
            for h0 in range(0, CH, HEAD_DIM_B):
                zh = z[:, h0:h0 + HEAD_DIM_B]
                ms = jnp.mean(zh * zh, axis=-1, keepdims=True)
                halves.append(zh * lax.rsqrt(ms + RMS_EPS))
            z = jnp.concatenate(halves, axis=1) * hg_ref[:, cs]
        o_ref[:, cs] = z.astype(o_ref.dtype)

    u = u_ref[...]
    chunks = [slice(c0, c0 + CH) for c0 in range(0, o_ref.shape[1], CH)]
    z_next = _dot(u, w_ref[:, chunks[0]])
    for idx, cs in enumerate(chunks):
        z = z_next
        if idx + 1 < len(chunks):
            z_next = _dot(u, w_ref[:, chunks[idx + 1]])
        finish(z, cs, _head_norm_kind(cs.start))


def _qkv_proj(x, g, w, head_gain, bd, *, tm):
    m, d = x.shape
    n = w.shape[1]
    assert n % V7X_MXU_DIM == 0
    resident = pl.Buffered(1)
    return pl.pallas_call(
        _qkv_proj_kernel,
        grid=(m // tm,),
        in_specs=[
            pl.BlockSpec((tm, d), lambda i: (i, 0)),
            pl.BlockSpec((1, d), lambda i: (0, 0)),
            pl.BlockSpec((d, n), lambda i: (0, 0), pipeline_mode=resident),
            pl.BlockSpec((1, n), lambda i: (0, 0)),
            pl.BlockSpec((V7X_MXU_DIM, V7X_MXU_DIM), lambda i: (0, 0)),
        ],
        out_specs=[
            pl.BlockSpec((tm, n), lambda i: (i, 0)),
            pl.BlockSpec((tm, d), lambda i: (i, 0)),
        ],
        out_shape=[jax.ShapeDtypeStruct((m, n), BF16), jax.ShapeDtypeStruct((m, d), BF16)],
        compiler_params=_params(("parallel",)),
        name="qkv_proj",
    )(x, g, w, head_gain, bd)


def _side_cast_specs(arrays, n_steps, step_index):
    in_specs, out_specs, shapes = [], [], []
    for item in arrays:
        w, col0, ncols, col_tile = item if isinstance(item, tuple) else (item, 0, item.shape[1], item.shape[1])
        rows = w.shape[0]
        n_col = ncols // col_tile
        n_row = n_steps // n_col
        chunk = rows // n_row
        assert n_col * col_tile == ncols and n_row * n_col == n_steps and col0 % col_tile == 0, (w.shape, n_steps)
        assert chunk * n_row == rows and chunk % BF16_SUBLANE_TILE == 0, (w.shape, n_steps)

        def in_map(*g, n_col=n_col, c0=col0 // col_tile):
            s = step_index(*g)
            return (s // n_col, c0 + s % n_col)

        def out_map(*g, n_col=n_col):
            s = step_index(*g)
            return (s // n_col, s % n_col)

        in_specs.append(pl.BlockSpec((chunk, col_tile), in_map))
        out_specs.append(pl.BlockSpec((chunk, col_tile), out_map))
        shapes.append(jax.ShapeDtypeStruct((rows, ncols), BF16))
    return in_specs, out_specs, shapes


def _side_cast(refs):
    n = len(refs) // 2
    for src, dst in zip(refs[:n], refs[n:]):
        dst[...] = src[...].astype(dst.dtype)


def _gate_proj_kernel(u_ref, w_ref, *refs):
    n_side = (len(refs) - 1) // 2
    o_ref = refs[n_side]
    o_ref[...] = _dot(u_ref[...], w_ref[...]).astype(o_ref.dtype)
    _side_cast(refs[:n_side] + refs[n_side + 1:])


def _gate_proj(u, w, side, *, tm=2048, tn=1024):
    m, k = u.shape
    n = w.shape[1]
    gi, gj = m // tm, n // tn
    side_in, side_out, side_shapes = _side_cast_specs(side, gi * gj, lambda i, j: i * gj + j)
    outs = pl.pallas_call(
        _gate_proj_kernel,
        grid=(gi, gj),
        in_specs=[
            pl.BlockSpec((tm, k), lambda i, j: (i, 0)),
            pl.BlockSpec((k, tn), lambda i, j: (0, j)),
        ] + side_in,
        out_specs=[pl.BlockSpec((tm, tn), lambda i, j: (i, j))] + side_out,
        out_shape=[jax.ShapeDtypeStruct((m, n), BF16)] + side_shapes,
        compiler_params=_params(("arbitrary", "arbitrary")),
        name="gate_proj",
    )(u, w, *side)
    return outs[0], outs[1:]


def _swa_kernel(sinks_ref, q_ref, kc_ref, kp_ref, vc_ref, vp_ref, bias_ref, *refs):
    n_side = (len(refs) - 1) // 2
    o_ref = refs[n_side]
    _side_cast(refs[:n_side] + refs[n_side + 1:])
    L = WINDOW
    QB = SWA_QBLOCKS_PER_STEP
    first_step = (pl.program_id(1) == 0).astype(jnp.int32)
    k = jnp.concatenate([kp_ref[0], kc_ref[0]], axis=0).astype(F32)
    v = jnp.concatenate([vp_ref[0], vc_ref[0]], axis=0).astype(F32)

    lane = lax.broadcasted_iota(jnp.int32, (1, V7X_LANES), 1)
    left = lane < HEAD_DIM_A
    lo_head = lax.broadcasted_iota(jnp.int32, (1, 2 * L), 1) < L

    n_kt = WKV_A // V7X_LANES
    kt = [k[:, t * 128:(t + 1) * 128] for t in range(n_kt)]
    kt_sw = [pltpu.roll(x, HEAD_DIM_A, axis=1).astype(BF16) for x in kt]
    kt = [x.astype(BF16) for x in kt]
    pad_row = lax.broadcasted_iota(jnp.int32, (BF16_SUBLANE_TILE, k.shape[0]), 0)
    ones_row = jnp.where(pad_row == 0, 1.0, 0.0)
    vt = [jnp.concatenate([v[:, t * 128:(t + 1) * 128].T, ones_row], axis=0).astype(BF16) for t in range(n_kt)]

    def q_masked(qb, h):
        tile = h // 2
        qt = q_ref[0, qb * L:(qb + 1) * L, tile * 128:(tile + 1) * 128]
        keep = left if h % 2 == 0 else jnp.logical_not(left)
        return jnp.where(keep, qt, jnp.zeros_like(qt))

    rep = N_Q_A // N_KV_A

    def scores(qb, g, hpar):
        t, par = g // 2, g % 2
        ha, hb = rep * g + hpar, rep * g + hpar + 2
        k_al = (kt if par == hpar else kt_sw)[t][qb * L:(qb + 2) * L]
        qm = jnp.concatenate([q_masked(qb, ha), q_masked(qb, hb)], axis=0)
        first = first_step if qb == 0 else 0
        return _dot_nt(k_al, qm) + bias_ref[first, 2 * g + hpar]

    def weighted_values(qb, g, hpar, s, out_rows):
        t, par = g // 2, g % 2
        ha, hb = rep * g + hpar, rep * g + hpar + 2
        sink = jnp.where(lo_head, sinks_ref[ha], sinks_ref[hb]) * LOG2E
        m = jnp.maximum(jnp.max(s, axis=0, keepdims=True), sink)
        e = jnp.exp2(s - m).astype(BF16)
        ot = _dot(vt[t][:, qb * L:(qb + 2) * L], e)
        denom = ot[V7X_LANES:V7X_LANES + 1, :] + jnp.exp2(sink - m)
        og = ot[par * HEAD_DIM_A:(par + 1) * HEAD_DIM_A, :] * (1.0 / denom)
        out_rows[ha] = og[:, :L]
        out_rows[hb] = og[:, L:]

    tasks = [(qb, g, hpar) for qb in range(QB) for g in range(N_KV_A) for hpar in range(2)]
    ahead = {}
    out_rows = {}
    for t in range(len(tasks) + SWA_SCORE_LOOKAHEAD):
        if t < len(tasks):
            ahead[t] = scores(*tasks[t])
        d = t - SWA_SCORE_LOOKAHEAD
        if d >= 0:
            qb, g, hpar = tasks[d]
            rows = out_rows.setdefault(qb, [None] * N_Q_A)
            weighted_values(qb, g, hpar, ahead.pop(d), rows)
            if (g, hpar) == (N_KV_A - 1, 1):
                o_t = jnp.concatenate(out_rows.pop(qb), axis=0)
                o_ref[0, qb * L:(qb + 1) * L, :] = o_t.T.astype(o_ref.dtype)


def _swa(z3, sinks, bias, side):
    B, S, _ = z3.shape
    L = WINDOW
    QB = SWA_QBLOCKS_PER_STEP
    nb = S // (QB * L)
    kblk = COL_KA // WKV_A
    vblk = COL_VA // WKV_A
    prev = lambda n: jnp.maximum(n * QB - 1, 0)
    side_in, side_out, side_shapes = _side_cast_specs(side, B * nb, lambda b, n: b * nb + n)
    outs = pl.pallas_call(
        _swa_kernel,
        grid=(B, nb),
        in_specs=[
            pl.BlockSpec(memory_space=pltpu.SMEM),
            pl.BlockSpec((1, QB * L, WQ_A), lambda b, n: (b, n, 0)),
            pl.BlockSpec((1, QB * L, WKV_A), lambda b, n: (b, n, kblk)),
            pl.BlockSpec((1, L, WKV_A), lambda b, n: (b, prev(n), kblk)),
            pl.BlockSpec((1, QB * L, WKV_A), lambda b, n: (b, n, vblk)),
            pl.BlockSpec((1, L, WKV_A), lambda b, n: (b, prev(n), vblk)),
            pl.BlockSpec((2, N_Q_A // 2, 2 * L, 2 * L), lambda b, n: (0, 0, 0, 0)),
        ] + side_in,
        out_specs=[pl.BlockSpec((1, QB * L, WQ_A), lambda b, n: (b, n, 0))] + side_out,
        out_shape=[jax.ShapeDtypeStruct((B, S, WQ_A), BF16)] + side_shapes,
        compiler_params=_params(("arbitrary", "arbitrary")),
        name="swa",
    )(sinks, z3, z3, z3, z3, z3, bias, *side)
    return outs[0], outs[1:]


def _swa_bias_table():
    L = WINDOW
    rep = N_Q_A // N_KV_A
    slopes = np.exp2(-8.0 * np.arange(1, N_Q_A + 1, dtype=np.float32) / N_Q_A).astype(np.float32)
    kj = np.arange(2 * L)[:, None]
    qi = np.arange(L)[None, :]
    dist = L + qi - kj
    window = (dist >= 0) & (dist < WINDOW)
    table = np.empty((2, N_Q_A // 2, 2 * L, 2 * L), np.float32)
    for first in range(2):
        valid = window & ((kj >= L) if first else True)
        for g in range(N_KV_A):
            for hpar in range(2):
                for a in range(2):
                    h = rep * g + hpar + 2 * a
                    table[first, 2 * g + hpar, :, a * L:(a + 1) * L] = np.where(
                        valid, -(slopes[h] * LOG2E) * dist.astype(np.float32), -np.inf)
    return jnp.asarray(table)


def _block_diag(width, block, value):
    idx = np.arange(width) // block
    return jnp.asarray((idx[:, None] == idx[None, :]).astype(np.float32) * value, dtype=BF16)


def _moba_kernel(slopes_ref, q_ref, k_ref, v_ref, *refs):
    n_side = (len(refs) - 4) // 2
    o_ref = refs[n_side]
    vt_ref, gate_ref, tab_ref = refs[2 * n_side + 1:]
    _side_cast(refs[:n_side] + refs[n_side + 1:2 * n_side + 1])
    BLK = MOBA_BLOCK
    HP = MOBA_HEADS_PER_STEP
    dh = HEAD_DIM_B
    hp = pl.program_id(1)
    S = k_ref.shape[1]
    nblk = S // BLK

    def prepare_tables():
        kc = lax.broadcasted_iota(jnp.int32, (BLK, BLK), 0)
        qr = lax.broadcasted_iota(jnp.int32, (BLK, BLK), 1)
        rel = (qr - kc).astype(F32)
        for a in range(HP):
            slope2 = slopes_ref[hp * HP + a] * LOG2E
            tab_ref[a, 0] = -slope2 * rel
            tab_ref[a, 1] = jnp.where(rel >= 0.0, -slope2 * rel, NEG_INF)

    def prepare_values():
        eye = (lax.broadcasted_iota(jnp.int32, (dh, dh), 0)
               == lax.broadcasted_iota(jnp.int32, (dh, dh), 1)).astype(BF16)
        pad_row = lax.broadcasted_iota(jnp.int32, (BF16_SUBLANE_TILE, S), 0)
        for a in range(HP):
            vt_ref[a, 0:dh, :] = _dot_nt(eye, v_ref[0, :, a * dh:(a + 1) * dh]).astype(BF16)
            vt_ref[a, dh:, :] = jnp.where(pad_row == 0, 1.0, 0.0).astype(BF16)

    def block_means():
        blk = lax.broadcasted_iota(jnp.int32, (GATE_ROWS, S), 0)
        pos = lax.broadcasted_iota(jnp.int32, (GATE_ROWS, S), 1)
        member = jnp.where((pos >= blk * BLK) & (pos < (blk + 1) * BLK), 1.0 / BLK, 0.0).astype(BF16)
        return [_split_bf16(_dot(member, k_ref[0, :, a * dh:(a + 1) * dh])) for a in range(HP)]

    def prepare_gate(kmeans):
        for a, (km_hi, km_lo) in enumerate(kmeans):
            qn = q_ref[0, :, a * dh:(a + 1) * dh]
            gate_ref[a] = _dot_nt(km_hi, qn) + _dot_nt(km_lo, qn)

    def block_max_shifts(c, a):
        slope2 = slopes_ref[hp * HP + a] * LOG2E
        shifts = [-slope2 * float((c - n) * BLK) for n in range(c)]
        if c > MOBA_TOPK:
            gate = gate_ref[a, :, c * BLK:(c + 1) * BLK]
            blk = lax.broadcasted_iota(jnp.int32, (GATE_ROWS, 1), 0)
            past = blk < c
            for n in range(c):
                g_n = gate[n:n + 1, :]
                beats = ((gate > g_n) | ((gate == g_n) & (blk < n))) & past
                rank = jnp.sum(jnp.where(beats, 1.0, 0.0), axis=0, keepdims=True)
                shifts[n] = jnp.where(rank < float(MOBA_TOPK), shifts[n], NEG_INF)
        return shifts

    def scores(c, a, ss, ms):
        for j in range(c + 1):
            s = _dot_nt(k_ref[0, j * BLK:(j + 1) * BLK, a * dh:(a + 1) * dh],
                        q_ref[0, c * BLK:(c + 1) * BLK, a * dh:(a + 1) * dh])
            s = s + tab_ref[a, 1 if j == c else 0]
            ss.append(s)
            ms.append(jnp.max(s, axis=0, keepdims=True))
            yield

    def weighted_values(c, a, ss, ms):
        shifts = block_max_shifts(c, a)
        m = functools.reduce(jnp.maximum, [m_j + sh for m_j, sh in zip(ms, shifts)] + ms[c:])
        acc = None
        for j in range(c + 1):
            p = jnp.exp2(ss[j] - (m - shifts[j] if j < c else m)).astype(BF16)
            part = _dot(vt_ref[a, :, j * BLK:(j + 1) * BLK], p)
            acc = part if acc is None else acc + part
            if j < c:
                yield
        ot = acc[0:dh] * (1.0 / acc[dh:dh + 1])
        o_ref[0, c * BLK:(c + 1) * BLK, a * dh:(a + 1) * dh] = ot.T.astype(o_ref.dtype)
        yield

    prepare_tables()
    kmeans = block_means()
    tasks = [(c, a) for c in range(nblk) for a in range(HP)]
    ahead = {}
    for t in range(len(tasks) + MOBA_TASK_LOOKAHEAD):
        running = []
        if t < len(tasks):
            ahead[t] = ([], [])
            running.append(scores(*tasks[t], *ahead[t]))
        d = t - MOBA_TASK_LOOKAHEAD
        if d >= 0:
            c, a = tasks[d]
            if d == 0:
                prepare_values()
            if (c, a) == (MOBA_TOPK, 0):
                prepare_gate(kmeans)
            running.append(weighted_values(c, a, *ahead.pop(d)))
        while running:
            running = [g for g in running if next(g, StopIteration) is not StopIteration]


def _moba(z3, slopes, side):
    B, S, _ = z3.shape
    BLK = MOBA_BLOCK
    HP = MOBA_HEADS_PER_STEP
    w = HP * HEAD_DIM_B
    qc, kc, vc = COL_QB // w, COL_KB // w, COL_VB // w
    n_hp = N_H_B // HP
    side_in, side_out, side_shapes = _side_cast_specs(side, B * n_hp, lambda b, h: b * n_hp + h)
    outs = pl.pallas_call(
        _moba_kernel,
        grid=(B, n_hp),
        in_specs=[
            pl.BlockSpec(memory_space=pltpu.SMEM),
            pl.BlockSpec((1, S, w), lambda b, h: (b, 0, qc + h)),
            pl.BlockSpec((1, S, w), lambda b, h: (b, 0, kc + h)),
            pl.BlockSpec((1, S, w), lambda b, h: (b, 0, vc + h)),
        ] + side_in,
        out_specs=[pl.BlockSpec((1, S, w), lambda b, h: (b, 0, h))] + side_out,
        out_shape=[jax.ShapeDtypeStruct((B, S, W_B), BF16)] + side_shapes,
        scratch_shapes=[
            pltpu.VMEM((HP, HEAD_DIM_B + BF16_SUBLANE_TILE, S), BF16),
            pltpu.VMEM((HP, GATE_ROWS, S), F32),
            pltpu.VMEM((HP, 2, BLK, BLK), F32),
        ],
        compiler_params=_params(("arbitrary", "arbitrary")),
        name="moba",
    )(slopes, z3, z3, z3, *[s[0] if isinstance(s, tuple) else s for s in side])
    return outs[0], outs[1:]


def _sigmoid(x):
    return 1.0 / (1.0 + jnp.exp(-x))


def _merge_out_kernel(oa_ref, ob_ref, ga_ref, gb_ref, ba_ref, bb_ref, wa_ref, wb_ref, wo_ref, x_ref,
                      g2_ref, o_ref, u2_ref, mixed_ref, *, chunk):
    oa = oa_ref[...]
    ob = ob_ref[...]
    for c0 in range(0, mixed_ref.shape[1], chunk):
        cs = slice(c0, c0 + chunk)
        a = _dot(oa, wa_ref[:, cs])
        b = _dot(ob, wb_ref[:, cs])
        ga = _sigmoid(ga_ref[:, cs].astype(F32) + ba_ref[:, cs])
        gb = _sigmoid(gb_ref[:, cs].astype(F32) + bb_ref[:, cs])
        mixed_ref[:, cs] = (ga * a + gb * b).astype(mixed_ref.dtype)
    h1 = x_ref[...] + _dot(mixed_ref[...], wo_ref[...])
    o_ref[...] = h1
    inv = lax.rsqrt(jnp.mean(h1 * h1, axis=-1, keepdims=True) + RMS_EPS)
    u2_ref[...] = (h1 * inv * g2_ref[...]).astype(u2_ref.dtype)


def _merge_out(oa, ob, gates, ba, bb, wa, wb, wo, x, g2, *, tm=512, chunk=512):
    m, ka = oa.shape
    kb = ob.shape[1]
    d = wa.shape[1]
    resident = pl.Buffered(1)
    return pl.pallas_call(
        functools.partial(_merge_out_kernel, chunk=chunk),
        grid=(m // tm,),
        in_specs=[
            pl.BlockSpec((tm, ka), lambda i: (i, 0)),
            pl.BlockSpec((tm, kb), lambda i: (i, 0)),
            pl.BlockSpec((tm, d), lambda i: (i, 0)),
            pl.BlockSpec((tm, d), lambda i: (i, 1)),
            pl.BlockSpec((1, d), lambda i: (0, 0)),
            pl.BlockSpec((1, d), lambda i: (0, 0)),
            pl.BlockSpec((ka, d), lambda i: (0, 0), pipeline_mode=resident),
            pl.BlockSpec((kb, d), lambda i: (0, 0), pipeline_mode=resident),
            pl.BlockSpec((d, d), lambda i: (0, 0), pipeline_mode=resident),
            pl.BlockSpec((tm, d), lambda i: (i, 0)),
            pl.BlockSpec((1, d), lambda i: (0, 0)),
        ],
        out_specs=[pl.BlockSpec((tm, d), lambda i: (i, 0)), pl.BlockSpec((tm, d), lambda i: (i, 0))],
        out_shape=[jax.ShapeDtypeStruct((m, d), F32), jax.ShapeDtypeStruct((m, d), BF16)],
        scratch_shapes=[pltpu.VMEM((tm, d), BF16)],
        compiler_params=_params(("parallel",)),
        name="merge_out",
    )(oa, ob, gates, gates, ba, bb, wa, wb, wo, x, g2)


def _ffn_down_kernel(a1_ref, a2_ref, w1_ref, w2_ref, r_ref, o_ref):
    o_ref[...] = r_ref[...] + _dot(a1_ref[...], w1_ref[...]) + _dot(a2_ref[...], w2_ref[...])


def _ffn_down(a1, a2, w, res, *, tm, tn):
    m, k1 = a1.shape
    k2 = a2.shape[1]
    n = w.shape[1]
    assert k1 % k2 == 0 and w.shape[0] == k1 + k2
    return pl.pallas_call(
        _ffn_down_kernel,
        grid=(m // tm, n // tn),
        in_specs=[
            pl.BlockSpec((tm, k1), lambda i, j: (i, 0)),
            pl.BlockSpec((tm, k2), lambda i, j: (i, 0)),
            pl.BlockSpec((k1, tn), lambda i, j: (0, j)),
            pl.BlockSpec((k2, tn), lambda i, j: (k1 // k2, j)),
            pl.BlockSpec((tm, tn), lambda i, j: (i, j)),
        ],
        out_specs=pl.BlockSpec((tm, tn), lambda i, j: (i, j)),
        out_shape=jax.ShapeDtypeStruct((m, n), F32),
        compiler_params=_params(("parallel", "parallel")),
        name="ffn_down",
    )(a1, a2, w, w, res)


def _ffn_up_kernel(u_ref, wg_ref, wu_ref, o_ref):
    u = u_ref[...]
    for c0 in range(0, o_ref.shape[1], V7X_MXU_DIM):
        cs = slice(c0, c0 + V7X_MXU_DIM)
        gate = _dot(u, wg_ref[:, cs])
        up = _dot(u, wu_ref[:, cs])
        o_ref[:, cs] = (gate * _sigmoid(gate) * up).astype(o_ref.dtype)


def _ffn_up(u, wg, wu, col0, n, *, tm, tn, name):
    m, d = u.shape
    assert n % tn == 0 and col0 % tn == 0
    first = col0 // tn
    return pl.pallas_call(
        _ffn_up_kernel,
        grid=(m // tm, n // tn),
        in_specs=[
            pl.BlockSpec((tm, d), lambda i, j: (i, 0)),
            pl.BlockSpec((d, tn), lambda i, j: (0, first + j)),
            pl.BlockSpec((d, tn), lambda i, j: (0, first + j)),
        ],
        out_specs=pl.BlockSpec((tm, tn), lambda i, j: (i, j)),
        out_shape=jax.ShapeDtypeStruct((m, n), BF16),
        compiler_params=_params(("parallel", "parallel")),
        name=name,
    )(u, wg, wu)


def kernel(x, norm1_g, w_in, b_gate, q_norm_a, k_norm_a, sinks_a, q_norm_b, k_norm_b,
           w_branch_a, w_branch_b, w_o, norm2_g, w_ffn_gate, w_ffn_up, w_ffn_down):
    B, S, D = x.shape
    depth = w_in.shape[0]
    assert D == D_MODEL and w_in.shape[2] == IN_COLS
    assert S % MOBA_BLOCK == 0 and S % WINDOW == 0
    M = B * S

    slopes_b = jnp.asarray(np.exp2(-8.0 * np.arange(1, N_H_B + 1, dtype=np.float32) / N_H_B), F32)
    swa_bias = _swa_bias_table()
    bd = _block_diag(V7X_MXU_DIM, HEAD_DIM_A, 1.0 / HEAD_DIM_A)

    h = x.reshape(M, D)
    for l in range(depth):
        head_gain = jnp.concatenate([
            jnp.tile(q_norm_a[l], N_Q_A) * (HEAD_DIM_A ** -0.5 * LOG2E),
            jnp.tile(k_norm_a[l], N_KV_A),
            jnp.ones((WKV_A,), F32),
            jnp.tile(q_norm_b[l], N_H_B) * (HEAD_DIM_B ** -0.5 * LOG2E),
            jnp.tile(k_norm_b[l], N_H_B),
            jnp.ones((W_B,), F32),
        ]).reshape(1, COL_GA)
        qkv, u = _qkv_proj(h, norm1_g[l].reshape(1, D), w_in[l, :, :COL_GA].astype(BF16), head_gain, bd,
                           tm=512)
        z3 = qkv.reshape(B, S, COL_GA)
        o_a, (wg_b, wu_b) = _swa(z3, sinks_a[l], swa_bias, [w_ffn_gate[l], w_ffn_up[l]])
        o_b, (wd_b, wgate_b) = _moba(z3, slopes_b,
                                     [w_ffn_down[l], (w_in[l], COL_GA, 2 * D, SIDE_CAST_COL_TILE)])
        gates, (wo_b, wa_b, wb_b) = _gate_proj(u, wgate_b, [w_o[l], w_branch_a[l], w_branch_b[l]])
        h1, u2 = _merge_out(o_a.reshape(M, WQ_A), o_b.reshape(M, W_B), gates,
                            b_gate[l, :D].reshape(1, D), b_gate[l, D:].reshape(1, D),
                            wa_b, wb_b, wo_b, h, norm2_g[l].reshape(1, D))
        d_ff = wg_b.shape[1]
        wide = (d_ff // FFN_UP_WIDE_TILE) * FFN_UP_WIDE_TILE
        act_wide = _ffn_up(u2, wg_b, wu_b, 0, wide, tm=2048, tn=FFN_UP_WIDE_TILE, name="ffn_up")
        act_rest = _ffn_up(u2, wg_b, wu_b, wide, d_ff - wide, tm=2048, tn=d_ff - wide, name="ffn_up_rest")
        h = _ffn_down(act_wide, act_rest, wd_b, h1, tm=1024, tn=512)
    return h.reshape(B, S, D)
```

```python
import functools

import numpy as np
import jax
import jax.numpy as jnp
from jax import lax
from jax.experimental import pallas as pl
from jax.experimental.pallas import tpu as pltpu

F32 = jnp.float32
BF16 = jnp.bfloat16

D_MODEL = 2048
HEAD_DIM_A = 64
N_Q_A = 16
N_KV_A = 4
WINDOW = 128
HEAD_DIM_B = 128
N_H_B = 8
MOBA_BLOCK = 256
MOBA_TOPK = 3
RMS_EPS = 1e-6

WQ_A = N_Q_A * HEAD_DIM_A
WKV_A = N_KV_A * HEAD_DIM_A
W_B = N_H_B * HEAD_DIM_B
COL_QA = 0
COL_KA = COL_QA + WQ_A
COL_VA = COL_KA + WKV_A
COL_QB = COL_VA + WKV_A
COL_KB = COL_QB + W_B
COL_VB = COL_KB + W_B
COL_GA = COL_VB + W_B
COL_GB = COL_GA + D_MODEL
IN_COLS = COL_GB + D_MODEL

V7X_LANES = 128
V7X_MXU_DIM = 256
VMEM_LIMIT_BYTES = 56 * 1024 * 1024
NORM_ROW_CHUNK = 64
NORM_UNROLL = 2
BF16_SUBLANE_TILE = 16
SIDE_CAST_COL_TILE = 512
SWA_QBLOCKS_PER_STEP = 8
SWA_SCORE_LOOKAHEAD = 8
MOBA_HEADS_PER_STEP = 2
MOBA_TASK_LOOKAHEAD = 4
GATE_ROWS = 16
NEG_INF = float("-inf")
LOG2E = 1.4426950408889634


def _params(sem):
    return pltpu.CompilerParams(dimension_semantics=sem, vmem_limit_bytes=VMEM_LIMIT_BYTES)


def _dot(a, b):
    return jnp.dot(a, b, preferred_element_type=F32)


def _dot_nt(a, b):
    return lax.dot_general(a, b, (((1,), (1,)), ((), ())), preferred_element_type=F32)


def _split_bf16(x):
    hi = x.astype(BF16)
    lo = (x - hi.astype(F32)).astype(BF16)
    return hi, lo


def _rms_rows_to_bf16(x_ref, g_ref, u_ref):
    rows = x_ref.shape[0]
    g = g_ref[...]

    def body(c, carry):
        r = pl.multiple_of(c * NORM_ROW_CHUNK, NORM_ROW_CHUNK)
        x = x_ref[pl.ds(r, NORM_ROW_CHUNK), :]
        ms = jnp.mean(x * x, axis=-1, keepdims=True)
        u_ref[pl.ds(r, NORM_ROW_CHUNK), :] = (x * lax.rsqrt(ms + RMS_EPS) * g).astype(BF16)
        return carry

    lax.fori_loop(0, rows // NORM_ROW_CHUNK, body, 0, unroll=NORM_UNROLL)


def _head_norm_kind(col):
    if col < COL_VA:
        return "norm64"
    if COL_QB <= col < COL_VB:
        return "norm128"
    return "copy"


def _qkv_proj_kernel(x0_ref, xn_ref, g_ref, w_ref, hg_ref, bd_ref, o_ref, u_ref, ubuf_ref):
    i = pl.program_id(0)
    slot = lax.rem(i, 2)

    @pl.when(i == 0)
    def _():
        _rms_rows_to_bf16(x0_ref, g_ref, ubuf_ref.at[0])

    u = ubuf_ref[slot]
    u_ref[...] = u
    xn = xn_ref[...]
    ms = jnp.mean(xn * xn, axis=-1, keepdims=True)
    ubuf_ref[1 - slot] = (xn * lax.rsqrt(ms + RMS_EPS) * g_ref[...]).astype(BF16)
    CH = V7X_MXU_DIM

    def finish(z, cs, kind):
        if kind == "norm64":
            inv = lax.rsqrt(_dot((z * z).astype(BF16), bd_ref[...]) + RMS_EPS)
            z = z * inv * hg_ref[:, cs]
        elif kind == "norm128":
            halves = []
            for h0 in range(0, CH, HEAD_DIM_B):
                zh = z[:, h0:h0 + HEAD_DIM_B]
                ms = jnp.mean(zh * zh, axis=-1, keepdims=True)
                halves.append(zh * lax.rsqrt(ms + RMS_EPS))
            z = jnp.concatenate(halves, axis=1) * hg_ref[:, cs]
        o_ref[:, cs] = z.astype(o_ref.dtype)

    chunks = [slice(c0, c0 + CH) for c0 in range(0, o_ref.shape[1], CH)]
    z_next = _dot(u, w_ref[:, chunks[0]])
    for idx, cs in enumerate(chunks):
        z = z_next
        if idx + 1 < len(chunks):
            z_next = _dot(u, w_ref[:, chunks[idx + 1]])
        finish(z, cs, _head_norm_kind(cs.start))


def _qkv_proj(x, g, w, head_gain, bd, *, tm):
    m, d = x.shape
    n = w.shape[1]
    assert n % V7X_MXU_DIM == 0
    resident = pl.Buffered(1)
    last = m // tm - 1
    return pl.pallas_call(
        _qkv_proj_kernel,
        grid=(m // tm,),
        in_specs=[
            pl.BlockSpec((tm, d), lambda i: (0, 0), pipeline_mode=resident),
            pl.BlockSpec((tm, d), lambda i: (jnp.minimum(i + 1, last), 0)),
            pl.BlockSpec((1, d), lambda i: (0, 0)),
            pl.BlockSpec((d, n), lambda i: (0, 0), pipeline_mode=resident),
            pl.BlockSpec((1, n), lambda i: (0, 0)),
            pl.BlockSpec((V7X_MXU_DIM, V7X_MXU_DIM), lambda i: (0, 0)),
        ],
        out_specs=[
            pl.BlockSpec((tm, n), lambda i: (i, 0)),
            pl.BlockSpec((tm, d), lambda i: (i, 0)),
        ],
        out_shape=[jax.ShapeDtypeStruct((m, n), BF16), jax.ShapeDtypeStruct((m, d), BF16)],
        scratch_shapes=[pltpu.VMEM((2, tm, d), BF16)],
        compiler_params=_params(("arbitrary",)),
        name="qkv_proj",
    )(x, x, g, w, head_gain, bd)


def _side_cast_specs(arrays, n_steps, step_index):
    in_specs, out_specs, shapes = [], [], []
    for item in arrays:
        w, col0, ncols, col_tile = item if isinstance(item, tuple) else (item, 0, item.shape[1], item.shape[1])
        rows = w.shape[0]
        n_col = ncols // col_tile
        n_row = n_steps // n_col
        chunk = rows // n_row
        assert n_col * col_tile == ncols and n_row * n_col == n_steps and col0 % col_tile == 0, (w.shape, n_steps)
        assert chunk * n_row == rows and chunk % BF16_SUBLANE_TILE == 0, (w.shape, n_steps)

        def in_map(*g, n_col=n_col, c0=col0 // col_tile):
            s = step_index(*g)
            return (s // n_col, c0 + s % n_col)

        def out_map(*g, n_col=n_col):
            s = step_index(*g)
            return (s // n_col, s % n_col)

        in_specs.append(pl.BlockSpec((chunk, col_tile), in_map))
        out_specs.append(pl.BlockSpec((chunk, col_tile), out_map))
        shapes.append(jax.ShapeDtypeStruct((rows, ncols), BF16))
    return in_specs, out_specs, shapes


def _side_cast(refs):
    n = len(refs) // 2
    for src, dst in zip(refs[:n], refs[n:]):
        dst[...] = src[...].astype(dst.dtype)


def _gate_proj_kernel(u_ref, w_ref, *refs):
    n_side = (len(refs) - 1) // 2
    o_ref = refs[n_side]
    o_ref[...] = _dot(u_ref[...], w_ref[...]).astype(o_ref.dtype)
    _side_cast(refs[:n_side] + refs[n_side + 1:])


def _gate_proj(u, w, side, *, tm=2048, tn=1024):
    m, k = u.shape
    n = w.shape[1]
    gi, gj = m // tm, n // tn
    side_in, side_out, side_shapes = _side_cast_specs(side, gi * gj, lambda i, j: i * gj + j)
    outs = pl.pallas_call(
        _gate_proj_kernel,
        grid=(gi, gj),
        in_specs=[
            pl.BlockSpec((tm, k), lambda i, j: (i, 0)),
            pl.BlockSpec((k, tn), lambda i, j: (0, j)),
        ] + side_in,
        out_specs=[pl.BlockSpec((tm, tn), lambda i, j: (i, j))] + side_out,
        out_shape=[jax.ShapeDtypeStruct((m, n), BF16)] + side_shapes,
        compiler_params=_params(("arbitrary", "arbitrary")),
        name="gate_proj",
    )(u, w, *side)
    return outs[0], outs[1:]


def _swa_kernel(sinks_ref, q_ref, kc_ref, kp_ref, vc_ref, vp_ref, bias_ref, *refs):
    n_side = (len(refs) - 1) // 2
    o_ref = refs[n_side]
    _side_cast(refs[:n_side] + refs[n_side + 1:])
    L = WINDOW
    QB = SWA_QBLOCKS_PER_STEP
    first_step = (pl.program_id(1) == 0).astype(jnp.int32)
    k = jnp.concatenate([kp_ref[0], kc_ref[0]], axis=0).astype(F32)
    v = jnp.concatenate([vp_ref[0], vc_ref[0]], axis=0).astype(F32)

    lane = lax.broadcasted_iota(jnp.int32, (1, V7X_LANES), 1)
    left = lane < HEAD_DIM_A
    lo_head = lax.broadcasted_iota(jnp.int32, (1, 2 * L), 1) < L

    n_kt = WKV_A // V7X_LANES
    kt = [k[:, t * 128:(t + 1) * 128] for t in range(n_kt)]
    kt_sw = [pltpu.roll(x, HEAD_DIM_A, axis=1).astype(BF16) for x in kt]
    kt = [x.astype(BF16) for x in kt]
    pad_row = lax.broadcasted_iota(jnp.int32, (BF16_SUBLANE_TILE, k.shape[0]), 0)
    ones_row = jnp.where(pad_row == 0, 1.0, 0.0)
    vt = [jnp.concatenate([v[:, t * 128:(t + 1) * 128].T, ones_row], axis=0).astype(BF16) for t in range(n_kt)]

    def q_masked(qb, h):
        tile = h // 2
        qt = q_ref[0, qb * L:(qb + 1) * L, tile * 128:(tile + 1) * 128]
        keep = left if h % 2 == 0 else jnp.logical_not(left)
        return jnp.where(keep, qt, jnp.zeros_like(qt))

    rep = N_Q_A // N_KV_A

    def scores(qb, g, hpar):
        t, par = g // 2, g % 2
        ha, hb = rep * g + hpar, rep * g + hpar + 2
        k_al = (kt if par == hpar else kt_sw)[t][qb * L:(qb + 2) * L]
        qm = jnp.concatenate([q_masked(qb, ha), q_masked(qb, hb)], axis=0)
        first = first_step if qb == 0 else 0
        return _dot_nt(k_al, qm) + bias_ref[first, 2 * g + hpar]

    def weighted_values(qb, g, hpar, s, out_rows):
        t, par = g // 2, g % 2
        ha, hb = rep * g + hpar, rep * g + hpar + 2
        sink = jnp.where(lo_head, sinks_ref[ha], sinks_ref[hb]) * LOG2E
        m = jnp.maximum(jnp.max(s, axis=0, keepdims=True), sink)
        e = jnp.exp2(s - m).astype(BF16)
        ot = _dot(vt[t][:, qb * L:(qb + 2) * L], e)
        denom = ot[V7X_LANES:V7X_LANES + 1, :] + jnp.exp2(sink - m)
        og = ot[par * HEAD_DIM_A:(par + 1) * HEAD_DIM_A, :] * (1.0 / denom)
        out_rows[ha] = og[:, :L]
        out_rows[hb] = og[:, L:]

    tasks = [(qb, g, hpar) for qb in range(QB) for g in range(N_KV_A) for hpar in range(2)]
    ahead = {}
    out_rows = {}
    for t in range(len(tasks) + SWA_SCORE_LOOKAHEAD):
        if t < len(tasks):
            ahead[t] = scores(*tasks[t])
        d = t - SWA_SCORE_LOOKAHEAD
        if d >= 0:
            qb, g, hpar = tasks[d]
            rows = out_rows.setdefault(qb, [None] * N_Q_A)
            weighted_values(qb, g, hpar, ahead.pop(d), rows)
            if (g, hpar) == (N_KV_A - 1, 1):
                o_t = jnp.concatenate(out_rows.pop(qb), axis=0)
                o_ref[0, qb * L:(qb + 1) * L, :] = o_t.T.astype(o_ref.dtype)


def _swa(z3, sinks, bias, side):
    B, S, _ = z3.shape
    L = WINDOW
    QB = SWA_QBLOCKS_PER_STEP
    nb = S // (QB * L)
    kblk = COL_KA // WKV_A
    vblk = COL_VA // WKV_A
    prev = lambda n: jnp.maximum(n * QB - 1, 0)
    side_in, side_out, side_shapes = _side_cast_specs(side, B * nb, lambda b, n: b * nb + n)
    outs = pl.pallas_call(
        _swa_kernel,
        grid=(B, nb),
        in_specs=[
            pl.BlockSpec(memory_space=pltpu.SMEM),
            pl.BlockSpec((1, QB * L, WQ_A), lambda b, n: (b, n, 0)),
            pl.BlockSpec((1, QB * L, WKV_A), lambda b, n: (b, n, kblk)),
            pl.BlockSpec((1, L, WKV_A), lambda b, n: (b, prev(n), kblk)),
            pl.BlockSpec((1, QB * L, WKV_A), lambda b, n: (b, n, vblk)),
            pl.BlockSpec((1, L, WKV_A), lambda b, n: (b, prev(n), vblk)),
            pl.BlockSpec((2, N_Q_A // 2, 2 * L, 2 * L), lambda b, n: (0, 0, 0, 0)),
        ] + side_in,
        out_specs=[pl.BlockSpec((1, QB * L, WQ_A), lambda b, n: (b, n, 0))] + side_out,
        out_shape=[jax.ShapeDtypeStruct((B, S, WQ_A), BF16)] + side_shapes,
        compiler_params=_params(("arbitrary", "arbitrary")),
        name="swa",
    )(sinks, z3, z3, z3, z3, z3, bias, *side)
    return outs[0], outs[1:]


def _swa_bias_table():
    L = WINDOW
    rep = N_Q_A // N_KV_A
    slopes = np.exp2(-8.0 * np.arange(1, N_Q_A + 1, dtype=np.float32) / N_Q_A).astype(np.float32)
    kj = np.arange(2 * L)[:, None]
    qi = np.arange(L)[None, :]
    dist = L + qi - kj
    window = (dist >= 0) & (dist < WINDOW)
    table = np.empty((2, N_Q_A // 2, 2 * L, 2 * L), np.float32)
    for first in range(2):
        valid = window & ((kj >= L) if first else True)
        for g in range(N_KV_A):
            for hpar in range(2):
                for a in range(2):
                    h = rep * g + hpar + 2 * a
                    table[first, 2 * g + hpar, :, a * L:(a + 1) * L] = np.where(
                        valid, -(slopes[h] * LOG2E) * dist.astype(np.float32), -np.inf)
    return jnp.asarray(table)


def _block_diag(width, block, value):
    idx = np.arange(width) // block
    return jnp.asarray((idx[:, None] == idx[None, :]).astype(np.float32) * value, dtype=BF16)


def _moba_kernel(slopes_ref, q_ref, k_ref, v_ref, *refs):
    n_side = (len(refs) - 4) // 2
    o_ref = refs[n_side]
    vt_ref, gate_ref, tab_ref = refs[2 * n_side + 1:]
    _side_cast(refs[:n_side] + refs[n_side + 1:2 * n_side + 1])
    BLK = MOBA_BLOCK
    HP = MOBA_HEADS_PER_STEP
    dh = HEAD_DIM_B
    hp = pl.program_id(1)
    S = k_ref.shape[1]
    nblk = S // BLK

    def prepare_tables():
        kc = lax.broadcasted_iota(jnp.int32, (BLK, BLK), 0)
        qr = lax.broadcasted_iota(jnp.int32, (BLK, BLK), 1)
        rel = (qr - kc).astype(F32)
        for a in range(HP):
            slope2 = slopes_ref[hp * HP + a] * LOG2E
            tab_ref[a, 0] = -slope2 * rel
            tab_ref[a, 1] = jnp.where(rel >= 0.0, -slope2 * rel, NEG_INF)

    def prepare_values():
        eye = (lax.broadcasted_iota(jnp.int32, (dh, dh), 0)
               == lax.broadcasted_iota(jnp.int32, (dh, dh), 1)).astype(BF16)
        pad_row = lax.broadcasted_iota(jnp.int32, (BF16_SUBLANE_TILE, S), 0)
        for a in range(HP):
            vt_ref[a, 0:dh, :] = _dot_nt(eye, v_ref[0, :, a * dh:(a + 1) * dh]).astype(BF16)
            vt_ref[a, dh:, :] = jnp.where(pad_row == 0, 1.0, 0.0).astype(BF16)

    def block_means():
        blk = lax.broadcasted_iota(jnp.int32, (GATE_ROWS, S), 0)
        pos = lax.broadcasted_iota(jnp.int32, (GATE_ROWS, S), 1)
        member = jnp.where((pos >= blk * BLK) & (pos < (blk + 1) * BLK), 1.0 / BLK, 0.0).astype(BF16)
        return [_split_bf16(_dot(member, k_ref[0, :, a * dh:(a + 1) * dh])) for a in range(HP)]

    def prepare_gate(kmeans):
        for a, (km_hi, km_lo) in enumerate(kmeans):
            qn = q_ref[0, :, a * dh:(a + 1) * dh]
            gate_ref[a] = _dot_nt(km_hi, qn) + _dot_nt(km_lo, qn)

    def block_max_shifts(c, a):
        slope2 = slopes_ref[hp * HP + a] * LOG2E
        shifts = [-slope2 * float((c - n) * BLK) for n in range(c)]
        if c > MOBA_TOPK:
            gate = gate_ref[a, :, c * BLK:(c + 1) * BLK]
            blk = lax.broadcasted_iota(jnp.int32, (GATE_ROWS, 1), 0)
            past = blk < c
            for n in range(c):
                g_n = gate[n:n + 1, :]
                beats = ((gate > g_n) | ((gate == g_n) & (blk < n))) & past
                rank = jnp.sum(jnp.where(beats, 1.0, 0.0), axis=0, keepdims=True)
                shifts[n] = jnp.where(rank < float(MOBA_TOPK), shifts[n], NEG_INF)
        return shifts

    def scores(c, a, ss, ms):
        for j in range(c + 1):
            s = _dot_nt(k_ref[0, j * BLK:(j + 1) * BLK, a * dh:(a + 1) * dh],
                        q_ref[0, c * BLK:(c + 1) * BLK, a * dh:(a + 1) * dh])
            s = s + tab_ref[a, 1 if j == c else 0]
            ss.append(s)
            ms.append(jnp.max(s, axis=0, keepdims=True))
            yield

    def weighted_values(c, a, ss, ms):
        shifts = block_max_shifts(c, a)
        m = functools.reduce(jnp.maximum, [m_j + sh for m_j, sh in zip(ms, shifts)] + ms[c:])
        acc = None
        for j in range(c + 1):
            p = jnp.exp2(ss[j] - (m - shifts[j] if j < c else m)).astype(BF16)
            part = _dot(vt_ref[a, :, j * BLK:(j + 1) * BLK], p)
            acc = part if acc is None else acc + part
            if j < c:
                yield
        ot = acc[0:dh] * (1.0 / acc[dh:dh + 1])
        o_ref[0, c * BLK:(c + 1) * BLK, a * dh:(a + 1) * dh] = ot.T.astype(o_ref.dtype)
        yield

    prepare_tables()
    kmeans = block_means()
    tasks = [(c, a) for c in range(nblk) for a in range(HP)]
    ahead = {}
    for t in range(len(tasks) + MOBA_TASK_LOOKAHEAD):
        running = []
        if t < len(tasks):
            ahead[t] = ([], [])
            running.append(scores(*tasks[t], *ahead[t]))
        d = t - MOBA_TASK_LOOKAHEAD
        if d >= 0:
            c, a = tasks[d]
            if d == 0:
                prepare_values()
            if (c, a) == (MOBA_TOPK, 0):
                prepare_gate(kmeans)
            running.append(weighted_values(c, a, *ahead.pop(d)))
        while running:
            running = [g for g in running if next(g, StopIteration) is not StopIteration]


def _moba(z3, slopes, side):
    B, S, _ = z3.shape
    BLK = MOBA_BLOCK
    HP = MOBA_HEADS_PER_STEP
    w = HP * HEAD_DIM_B
    qc, kc, vc = COL_QB // w, COL_KB // w, COL_VB // w
    n_hp = N_H_B // HP
    side_in, side_out, side_shapes = _side_cast_specs(side, B * n_hp, lambda b, h: b * n_hp + h)
    outs = pl.pallas_call(
        _moba_kernel,
        grid=(B, n_hp),
        in_specs=[
            pl.BlockSpec(memory_space=pltpu.SMEM),
            pl.BlockSpec((1, S, w), lambda b, h: (b, 0, qc + h)),
            pl.BlockSpec((1, S, w), lambda b, h: (b, 0, kc + h)),
            pl.BlockSpec((1, S, w), lambda b, h: (b, 0, vc + h)),
        ] + side_in,
        out_specs=[pl.BlockSpec((1, S, w), lambda b, h: (b, 0, h))] + side_out,
        out_shape=[jax.ShapeDtypeStruct((B, S, W_B), BF16)] + side_shapes,
        scratch_shapes=[
            pltpu.VMEM((HP, HEAD_DIM_B + BF16_SUBLANE_TILE, S), BF16),
            pltpu.VMEM((HP, GATE_ROWS, S), F32),
            pltpu.VMEM((HP, 2, BLK, BLK), F32),
        ],
        compiler_params=_params(("arbitrary", "arbitrary")),
        name="moba",
    )(slopes, z3, z3, z3, *[s[0] if isinstance(s, tuple) else s for s in side])
    return outs[0], outs[1:]


def _sigmoid(x):
    return 1.0 / (1.0 + jnp.exp(-x))


def _merge_out_kernel(oa_ref, ob_ref, ga_ref, gb_ref, ba_ref, bb_ref, wa_ref, wb_ref, wo_ref, x_ref,
                      g2_ref, o_ref, u2_ref, mixed_ref, *, chunk):
    oa = oa_ref[...]
    ob = ob_ref[...]
    for c0 in range(0, mixed_ref.shape[1], chunk):
        cs = slice(c0, c0 + chunk)
        a = _dot(oa, wa_ref[:, cs])
        b = _dot(ob, wb_ref[:, cs])
        ga = _sigmoid(ga_ref[:, cs].astype(F32) + ba_ref[:, cs])
        gb = _sigmoid(gb_ref[:, cs].astype(F32) + bb_ref[:, cs])
        mixed_ref[:, cs] = (ga * a + gb * b).astype(mixed_ref.dtype)
    h1 = x_ref[...] + _dot(mixed_ref[...], wo_ref[...])
    o_ref[...] = h1
    inv = lax.rsqrt(jnp.mean(h1 * h1, axis=-1, keepdims=True) + RMS_EPS)
    u2_ref[...] = (h1 * inv * g2_ref[...]).astype(u2_ref.dtype)


def _merge_out(oa, ob, gates, ba, bb, wa, wb, wo, x, g2, *, tm=512, chunk=512):
    m, ka = oa.shape
    kb = ob.shape[1]
    d = wa.shape[1]
    resident = pl.Buffered(1)
    return pl.pallas_call(
        functools.partial(_merge_out_kernel, chunk=chunk),
        grid=(m // tm,),
        in_specs=[
            pl.BlockSpec((tm, ka), lambda i: (i, 0)),
            pl.BlockSpec((tm, kb), lambda i: (i, 0)),
            pl.BlockSpec((tm, d), lambda i: (i, 0)),
            pl.BlockSpec((tm, d), lambda i: (i, 1)),
            pl.BlockSpec((1, d), lambda i: (0, 0)),
            pl.BlockSpec((1, d), lambda i: (0, 0)),
            pl.BlockSpec((ka, d), lambda i: (0, 0), pipeline_mode=resident),
            pl.BlockSpec((kb, d), lambda i: (0, 0), pipeline_mode=resident),
            pl.BlockSpec((d, d), lambda i: (0, 0), pipeline_mode=resident),
            pl.BlockSpec((tm, d), lambda i: (i, 0)),
            pl.BlockSpec((1, d), lambda i: (0, 0)),
        ],
        out_specs=[pl.BlockSpec((tm, d), lambda i: (i, 0)), pl.BlockSpec((tm, d), lambda i: (i, 0))],
        out_shape=[jax.ShapeDtypeStruct((m, d), F32), jax.ShapeDtypeStruct((m, d), BF16)],
        scratch_shapes=[pltpu.VMEM((tm, d), BF16)],
        compiler_params=_params(("parallel",)),
        name="merge_out",
    )(oa, ob, gates, gates, ba, bb, wa, wb, wo, x, g2)


def _mm_res_kernel(a_ref, w_ref, r_ref, o_ref):
    o_ref[...] = r_ref[...] + _dot(a_ref[...], w_ref[...])


def _mm_res(a, w, res, *, tm, tn, name):
    m, k = a.shape
    n = w.shape[1]
    return pl.pallas_call(
        _mm_res_kernel,
        grid=(m // tm, n // tn),
        in_specs=[
            pl.BlockSpec((tm, k), lambda i, j: (i, 0)),
            pl.BlockSpec((k, tn), lambda i, j: (0, j)),
            pl.BlockSpec((tm, tn), lambda i, j: (i, j)),
        ],
        out_specs=pl.BlockSpec((tm, tn), lambda i, j: (i, j)),
        out_shape=jax.ShapeDtypeStruct((m, n), F32),
        compiler_params=_params(("parallel", "parallel")),
        name=name,
    )(a, w, res)


def _ffn_up_kernel(u_ref, wg_ref, wu_ref, o_ref):
    u = u_ref[...]
    for c0 in range(0, o_ref.shape[1], V7X_MXU_DIM):
        cs = slice(c0, c0 + V7X_MXU_DIM)
        gate = _dot(u, wg_ref[:, cs])
        up = _dot(u, wu_ref[:, cs])
        o_ref[:, cs] = (gate * _sigmoid(gate) * up).astype(o_ref.dtype)


def _ffn_up(u, wg, wu, *, tm=2048, tn=512):
    m, d = u.shape
    n = wg.shape[1]
    return pl.pallas_call(
        _ffn_up_kernel,
        grid=(m // tm, n // tn),
        in_specs=[
            pl.BlockSpec((tm, d), lambda i, j: (i, 0)),
            pl.BlockSpec((d, tn), lambda i, j: (0, j)),
            pl.BlockSpec((d, tn), lambda i, j: (0, j)),
        ],
        out_specs=pl.BlockSpec((tm, tn), lambda i, j: (i, j)),
        out_shape=jax.ShapeDtypeStruct((m, n), BF16),
        compiler_params=_params(("parallel", "parallel")),
        name="ffn_up",
    )(u, wg, wu)


def kernel(x, norm1_g, w_in, b_gate, q_norm_a, k_norm_a, sinks_a, q_norm_b, k_norm_b,
           w_branch_a, w_branch_b, w_o, norm2_g, w_ffn_gate, w_ffn_up, w_ffn_down):
    B, S, D = x.shape
    depth = w_in.shape[0]
    assert D == D_MODEL and w_in.shape[2] == IN_COLS
    assert S % MOBA_BLOCK == 0 and S % WINDOW == 0
    M = B * S

    slopes_b = jnp.asarray(np.exp2(-8.0 * np.arange(1, N_H_B + 1, dtype=np.float32) / N_H_B), F32)
    swa_bias = _swa_bias_table()
    bd = _block_diag(V7X_MXU_DIM, HEAD_DIM_A, 1.0 / HEAD_DIM_A)

    h = x.reshape(M, D)
    for l in range(depth):
        head_gain = jnp.concatenate([
            jnp.tile(q_norm_a[l], N_Q_A) * (HEAD_DIM_A ** -0.5 * LOG2E),
            jnp.tile(k_norm_a[l], N_KV_A),
            jnp.ones((WKV_A,), F32),
            jnp.tile(q_norm_b[l], N_H_B) * (HEAD_DIM_B ** -0.5 * LOG2E),
            jnp.tile(k_norm_b[l], N_H_B),
            jnp.ones((W_B,), F32),
        ]).reshape(1, COL_GA)
        qkv, u = _qkv_proj(h, norm1_g[l].reshape(1, D), w_in[l, :, :COL_GA].astype(BF16), head_gain, bd,
                           tm=512)
        z3 = qkv.reshape(B, S, COL_GA)
        o_a, (wg_b, wu_b) = _swa(z3, sinks_a[l], swa_bias, [w_ffn_gate[l], w_ffn_up[l]])
        o_b, (wd_b, wgate_b) = _moba(z3, slopes_b,
                                     [w_ffn_down[l], (w_in[l], COL_GA, 2 * D, SIDE_CAST_COL_TILE)])
        gates, (wo_b, wa_b, wb_b) = _gate_proj(u, wgate_b, [w_o[l], w_branch_a[l], w_branch_b[l]])
        h1, u2 = _merge_out(o_a.reshape(M, WQ_A), o_b.reshape(M, W_B), gates,
                            b_gate[l, :D].reshape(1, D), b_gate[l, D:].reshape(1, D),
                            wa_b, wb_b, wo_b, h, norm2_g[l].reshape(1, D))
        act = _ffn_up(u2, wg_b, wu_b)
        h = _mm_res(act, wd_b, h1, tm=1024, tn=512, name="ffn_down")
    return h.reshape(B, S, D)
```

```python
import functools

import numpy as np
import jax
import jax.numpy as jnp
from jax import lax
from jax.experimental import pallas as pl
from jax.experimental.pallas import tpu as pltpu

F32 = jnp.float32
BF16 = jnp.bfloat16

D_MODEL = 2048
HEAD_DIM_A = 64
N_Q_A = 16
N_KV_A = 4
WINDOW = 128
HEAD_DIM_B = 128
N_H_B = 8
MOBA_BLOCK = 256
MOBA_TOPK = 3
RMS_EPS = 1e-6

WQ_A = N_Q_A * HEAD_DIM_A
WKV_A = N_KV_A * HEAD_DIM_A
W_B = N_H_B * HEAD_DIM_B
COL_QA = 0
COL_KA = COL_QA + WQ_A
COL_VA = COL_KA + WKV_A
COL_QB = COL_VA + WKV_A
COL_KB = COL_QB + W_B
COL_VB = COL_KB + W_B
COL_GA = COL_VB + W_B
COL_GB = COL_GA + D_MODEL
IN_COLS = COL_GB + D_MODEL

V7X_LANES = 128
V7X_MXU_DIM = 256
VMEM_LIMIT_BYTES = 56 * 1024 * 1024
NORM_ROW_CHUNK = 64
NORM_UNROLL = 2
BF16_SUBLANE_TILE = 16
SIDE_CAST_COL_TILE = 512
SWA_QBLOCKS_PER_STEP = 8
SWA_SCORE_LOOKAHEAD = 8
MOBA_HEADS_PER_STEP = 2
MOBA_TASK_LOOKAHEAD = 4
GATE_ROWS = 16
NEG_INF = float("-inf")
LOG2E = 1.4426950408889634


def _params(sem):
    return pltpu.CompilerParams(dimension_semantics=sem, vmem_limit_bytes=VMEM_LIMIT_BYTES)


def _dot(a, b):
    return jnp.dot(a, b, preferred_element_type=F32)


def _dot_nt(a, b):
    return lax.dot_general(a, b, (((1,), (1,)), ((), ())), preferred_element_type=F32)


def _split_bf16(x):
    hi = x.astype(BF16)
    lo = (x - hi.astype(F32)).astype(BF16)
    return hi, lo


def _rms_rows_to_bf16(x_ref, g_ref, u_ref):
    rows = x_ref.shape[0]
    g = g_ref[...]

    def body(c, carry):
        r = pl.multiple_of(c * NORM_ROW_CHUNK, NORM_ROW_CHUNK)
        x = x_ref[pl.ds(r, NORM_ROW_CHUNK), :]
        ms = jnp.mean(x * x, axis=-1, keepdims=True)
        u_ref[pl.ds(r, NORM_ROW_CHUNK), :] = (x * lax.rsqrt(ms + RMS_EPS) * g).astype(BF16)
        return carry

    lax.fori_loop(0, rows // NORM_ROW_CHUNK, body, 0, unroll=NORM_UNROLL)


def _head_norm_kind(col):
    if col < COL_VA:
        return "norm64"
    if COL_QB <= col < COL_VB:
        return "norm128"
    return "copy"


def _qkv_proj_kernel(x_ref, g_ref, w_ref, hg_ref, o_ref, u_ref):
    _rms_rows_to_bf16(x_ref, g_ref, u_ref)
    CH = V7X_MXU_DIM

    def finish(z, cs, kind):
        if kind == "norm64":
            left = lax.broadcasted_iota(jnp.int32, (1, V7X_LANES), 1) < HEAD_DIM_A
            tiles = []
            for h0 in range(0, CH, V7X_LANES):
                zt = z[:, h0:h0 + V7X_LANES]
                sq = zt * zt
                ms_l = jnp.sum(jnp.where(left, sq, 0.0), axis=-1, keepdims=True)
                ms_r = jnp.sum(jnp.where(left, 0.0, sq), axis=-1, keepdims=True)
                ms = jnp.where(left, ms_l, ms_r) * (1.0 / HEAD_DIM_A)
                tiles.append(zt * lax.rsqrt(ms + RMS_EPS))
            z = jnp.concatenate(tiles, axis=1) * hg_ref[:, cs]
        elif kind == "norm128":
            halves = []
            for h0 in range(0, CH, HEAD_DIM_B):
                zh = z[:, h0:h0 + HEAD_DIM_B]
                ms = jnp.mean(zh * zh, axis=-1, keepdims=True)
                halves.append(zh * lax.rsqrt(ms + RMS_EPS))
            z = jnp.concatenate(halves, axis=1) * hg_ref[:, cs]
        o_ref[:, cs] = z.astype(o_ref.dtype)

    u = u_ref[...]
    chunks = [slice(c0, c0 + CH) for c0 in range(0, o_ref.shape[1], CH)]
    z_next = _dot(u, w_ref[:, chunks[0]])
    for idx, cs in enumerate(chunks):
        z = z_next
        if idx + 1 < len(chunks):
            z_next = _dot(u, w_ref[:, chunks[idx + 1]])
        finish(z, cs, _head_norm_kind(cs.start))


def _qkv_proj(x, g, w, head_gain, *, tm):
    m, d = x.shape
    n = w.shape[1]
    assert n % V7X_MXU_DIM == 0
    resident = pl.Buffered(1)
    return pl.pallas_call(
        _qkv_proj_kernel,
        grid=(m // tm,),
        in_specs=[
            pl.BlockSpec((tm, d), lambda i: (i, 0)),
            pl.BlockSpec((1, d), lambda i: (0, 0)),
            pl.BlockSpec((d, n), lambda i: (0, 0), pipeline_mode=resident),
            pl.BlockSpec((1, n), lambda i: (0, 0)),
        ],
        out_specs=[
            pl.BlockSpec((tm, n), lambda i: (i, 0)),
            pl.BlockSpec((tm, d), lambda i: (i, 0)),
        ],
        out_shape=[jax.ShapeDtypeStruct((m, n), BF16), jax.ShapeDtypeStruct((m, d), BF16)],
        compiler_params=_params(("parallel",)),
        name="qkv_proj",
    )(x, g, w, head_gain)


def _side_cast_specs(arrays, n_steps, step_index):
    in_specs, out_specs, shapes = [], [], []
    for item in arrays:
        w, col0, ncols, col_tile = item if isinstance(item, tuple) else (item, 0, item.shape[1], item.shape[1])
        rows = w.shape[0]
        n_col = ncols // col_tile
        n_row = n_steps // n_col
        chunk = rows // n_row
        assert n_col * col_tile == ncols and n_row * n_col == n_steps and col0 % col_tile == 0, (w.shape, n_steps)
        assert chunk * n_row == rows and chunk % BF16_SUBLANE_TILE == 0, (w.shape, n_steps)

        def in_map(*g, n_col=n_col, c0=col0 // col_tile):
            s = step_index(*g)
            return (s // n_col, c0 + s % n_col)

        def out_map(*g, n_col=n_col):
            s = step_index(*g)
            return (s // n_col, s % n_col)

        in_specs.append(pl.BlockSpec((chunk, col_tile), in_map))
        out_specs.append(pl.BlockSpec((chunk, col_tile), out_map))
        shapes.append(jax.ShapeDtypeStruct((rows, ncols), BF16))
    return in_specs, out_specs, shapes


def _side_cast(refs):
    n = len(refs) // 2
    for src, dst in zip(refs[:n], refs[n:]):
        dst[...] = src[...].astype(dst.dtype)


def _gate_proj_kernel(u_ref, w_ref, *refs):
    n_side = (len(refs) - 1) // 2
    o_ref = refs[n_side]
    o_ref[...] = _dot(u_ref[...], w_ref[...]).astype(o_ref.dtype)
    _side_cast(refs[:n_side] + refs[n_side + 1:])


def _gate_proj(u, w, side, *, tm=2048, tn=1024):
    m, k = u.shape
    n = w.shape[1]
    gi, gj = m // tm, n // tn
    side_in, side_out, side_shapes = _side_cast_specs(side, gi * gj, lambda i, j: i * gj + j)
    outs = pl.pallas_call(
        _gate_proj_kernel,
        grid=(gi, gj),
        in_specs=[
            pl.BlockSpec((tm, k), lambda i, j: (i, 0)),
            pl.BlockSpec((k, tn), lambda i, j: (0, j)),
        ] + side_in,
        out_specs=[pl.BlockSpec((tm, tn), lambda i, j: (i, j))] + side_out,
        out_shape=[jax.ShapeDtypeStruct((m, n), BF16)] + side_shapes,
        compiler_params=_params(("arbitrary", "arbitrary")),
        name="gate_proj",
    )(u, w, *side)
    return outs[0], outs[1:]


def _swa_kernel(sinks_ref, q_ref, kc_ref, kp_ref, vc_ref, vp_ref, bias_ref, *refs):
    n_side = (len(refs) - 1) // 2
    o_ref = refs[n_side]
    _side_cast(refs[:n_side] + refs[n_side + 1:])
    L = WINDOW
    QB = SWA_QBLOCKS_PER_STEP
    first_step = (pl.program_id(1) == 0).astype(jnp.int32)
    k = jnp.concatenate([kp_ref[0], kc_ref[0]], axis=0).astype(F32)
    v = jnp.concatenate([vp_ref[0], vc_ref[0]], axis=0).astype(F32)

    lane = lax.broadcasted_iota(jnp.int32, (1, V7X_LANES), 1)
    left = lane < HEAD_DIM_A
    lo_head = lax.broadcasted_iota(jnp.int32, (1, 2 * L), 1) < L

    n_kt = WKV_A // V7X_LANES
    kt = [k[:, t * 128:(t + 1) * 128] for t in range(n_kt)]
    kt_sw = [pltpu.roll(x, HEAD_DIM_A, axis=1).astype(BF16) for x in kt]
    kt = [x.astype(BF16) for x in kt]
    pad_row = lax.broadcasted_iota(jnp.int32, (BF16_SUBLANE_TILE, k.shape[0]), 0)
    ones_row = jnp.where(pad_row == 0, 1.0, 0.0)
    vt = [jnp.concatenate([v[:, t * 128:(t + 1) * 128].T, ones_row], axis=0).astype(BF16) for t in range(n_kt)]

    def q_masked(qb, h):
        tile = h // 2
        qt = q_ref[0, qb * L:(qb + 1) * L, tile * 128:(tile + 1) * 128]
        keep = left if h % 2 == 0 else jnp.logical_not(left)
        return jnp.where(keep, qt, jnp.zeros_like(qt))

    rep = N_Q_A // N_KV_A

    def scores(qb, g, hpar):
        t, par = g // 2, g % 2
        ha, hb = rep * g + hpar, rep * g + hpar + 2
        k_al = (kt if par == hpar else kt_sw)[t][qb * L:(qb + 2) * L]
        qm = jnp.concatenate([q_masked(qb, ha), q_masked(qb, hb)], axis=0)
        first = first_step if qb == 0 else 0
        return _dot_nt(k_al, qm) + bias_ref[first, 2 * g + hpar]

    def weighted_values(qb, g, hpar, s, out_rows):
        t, par = g // 2, g % 2
        ha, hb = rep * g + hpar, rep * g + hpar + 2
        sink = jnp.where(lo_head, sinks_ref[ha], sinks_ref[hb]) * LOG2E
        m = jnp.maximum(jnp.max(s, axis=0, keepdims=True), sink)
        e = jnp.exp2(s - m).astype(BF16)
        ot = _dot(vt[t][:, qb * L:(qb + 2) * L], e)
        denom = ot[V7X_LANES:V7X_LANES + 1, :] + jnp.exp2(sink - m)
        og = ot[par * HEAD_DIM_A:(par + 1) * HEAD_DIM_A, :] * (1.0 / denom)
        out_rows[ha] = og[:, :L]
        out_rows[hb] = og[:, L:]

    tasks = [(qb, g, hpar) for qb in range(QB) for g in range(N_KV_A) for hpar in range(2)]
    ahead = {}
    out_rows = {}
    for t in range(len(tasks) + SWA_SCORE_LOOKAHEAD):
        if t < len(tasks):
            ahead[t] = scores(*tasks[t])
        d = t - SWA_SCORE_LOOKAHEAD
        if d >= 0:
            qb, g, hpar = tasks[d]
            rows = out_rows.setdefault(qb, [None] * N_Q_A)
            weighted_values(qb, g, hpar, ahead.pop(d), rows)
            if (g, hpar) == (N_KV_A - 1, 1):
                o_t = jnp.concatenate(out_rows.pop(qb), axis=0)
                o_ref[0, qb * L:(qb + 1) * L, :] = o_t.T.astype(o_ref.dtype)


def _swa(z3, sinks, bias, side):
    B, S, _ = z3.shape
    L = WINDOW
    QB = SWA_QBLOCKS_PER_STEP
    nb = S // (QB * L)
    kblk = COL_KA // WKV_A
    vblk = COL_VA // WKV_A
    prev = lambda n: jnp.maximum(n * QB - 1, 0)
    side_in, side_out, side_shapes = _side_cast_specs(side, B * nb, lambda b, n: b * nb + n)
    outs = pl.pallas_call(
        _swa_kernel,
        grid=(B, nb),
        in_specs=[
            pl.BlockSpec(memory_space=pltpu.SMEM),
            pl.BlockSpec((1, QB * L, WQ_A), lambda b, n: (b, n, 0)),
            pl.BlockSpec((1, QB * L, WKV_A), lambda b, n: (b, n, kblk)),
            pl.BlockSpec((1, L, WKV_A), lambda b, n: (b, prev(n), kblk)),
            pl.BlockSpec((1, QB * L, WKV_A), lambda b, n: (b, n, vblk)),
            pl.BlockSpec((1, L, WKV_A), lambda b, n: (b, prev(n), vblk)),
            pl.BlockSpec((2, N_Q_A // 2, 2 * L, 2 * L), lambda b, n: (0, 0, 0, 0)),
        ] + side_in,
        out_specs=[pl.BlockSpec((1, QB * L, WQ_A), lambda b, n: (b, n, 0))] + side_out,
        out_shape=[jax.ShapeDtypeStruct((B, S, WQ_A), BF16)] + side_shapes,
        compiler_params=_params(("arbitrary", "arbitrary")),
        name="swa",
    )(sinks, z3, z3, z3, z3, z3, bias, *side)
    return outs[0], outs[1:]


def _swa_bias_table():
    L = WINDOW
    rep = N_Q_A // N_KV_A
    slopes = np.exp2(-8.0 * np.arange(1, N_Q_A + 1, dtype=np.float32) / N_Q_A).astype(np.float32)
    kj = np.arange(2 * L)[:, None]
    qi = np.arange(L)[None, :]
    dist = L + qi - kj
    window = (dist >= 0) & (dist < WINDOW)
    table = np.empty((2, N_Q_A // 2, 2 * L, 2 * L), np.float32)
    for first in range(2):
        valid = window & ((kj >= L) if first else True)
        for g in range(N_KV_A):
            for hpar in range(2):
                for a in range(2):
                    h = rep * g + hpar + 2 * a
                    table[first, 2 * g + hpar, :, a * L:(a + 1) * L] = np.where(
                        valid, -(slopes[h] * LOG2E) * dist.astype(np.float32), -np.inf)
    return jnp.asarray(table)


def _moba_kernel(slopes_ref, q_ref, k_ref, v_ref, *refs):
    n_side = (len(refs) - 4) // 2
    o_ref = refs[n_side]
    vt_ref, gate_ref, tab_ref = refs[2 * n_side + 1:]
    _side_cast(refs[:n_side] + refs[n_side + 1:2 * n_side + 1])
    BLK = MOBA_BLOCK
    HP = MOBA_HEADS_PER_STEP
    dh = HEAD_DIM_B
    hp = pl.program_id(1)
    S = k_ref.shape[1]
    nblk = S // BLK

    def prepare_tables():
        kc = lax.broadcasted_iota(jnp.int32, (BLK, BLK), 0)
        qr = lax.broadcasted_iota(jnp.int32, (BLK, BLK), 1)
        rel = (qr - kc).astype(F32)
        for a in range(HP):
            slope2 = slopes_ref[hp * HP + a] * LOG2E
            tab_ref[a, 0] = -slope2 * rel
            tab_ref[a, 1] = jnp.where(rel >= 0.0, -slope2 * rel, NEG_INF)

    def prepare_values():
        eye = (lax.broadcasted_iota(jnp.int32, (dh, dh), 0)
               == lax.broadcasted_iota(jnp.int32, (dh, dh), 1)).astype(BF16)
        pad_row = lax.broadcasted_iota(jnp.int32, (BF16_SUBLANE_TILE, S), 0)
        for a in range(HP):
            vt_ref[a, 0:dh, :] = _dot_nt(eye, v_ref[0, :, a * dh:(a + 1) * dh]).astype(BF16)
            vt_ref[a, dh:, :] = jnp.where(pad_row == 0, 1.0, 0.0).astype(BF16)

    def block_means():
        blk = lax.broadcasted_iota(jnp.int32, (GATE_ROWS, S), 0)
        pos = lax.broadcasted_iota(jnp.int32, (GATE_ROWS, S), 1)
        member = jnp.where((pos >= blk * BLK) & (pos < (blk + 1) * BLK), 1.0 / BLK, 0.0).astype(BF16)
        return [_split_bf16(_dot(member, k_ref[0, :, a * dh:(a + 1) * dh])) for a in range(HP)]

    def prepare_gate(kmeans):
        for a, (km_hi, km_lo) in enumerate(kmeans):
            qn = q_ref[0, :, a * dh:(a + 1) * dh]
            gate_ref[a] = _dot_nt(km_hi, qn) + _dot_nt(km_lo, qn)

    def block_max_shifts(c, a):
        slope2 = slopes_ref[hp * HP + a] * LOG2E
        shifts = [-slope2 * float((c - n) * BLK) for n in range(c)]
        if c > MOBA_TOPK:
            gate = gate_ref[a, :, c * BLK:(c + 1) * BLK]
            blk = lax.broadcasted_iota(jnp.int32, (GATE_ROWS, 1), 0)
            past = blk < c
            for n in range(c):
                g_n = gate[n:n + 1, :]
                beats = ((gate > g_n) | ((gate == g_n) & (blk < n))) & past
                rank = jnp.sum(jnp.where(beats, 1.0, 0.0), axis=0, keepdims=True)
                shifts[n] = jnp.where(rank < float(MOBA_TOPK), shifts[n], NEG_INF)
        return shifts

    def scores(c, a, ss, ms):
        for j in range(c + 1):
            s = _dot_nt(k_ref[0, j * BLK:(j + 1) * BLK, a * dh:(a + 1) * dh],
                        q_ref[0, c * BLK:(c + 1) * BLK, a * dh:(a + 1) * dh])
            s = s + tab_ref[a, 1 if j == c else 0]
            ss.append(s)
            ms.append(jnp.max(s, axis=0, keepdims=True))
            yield

    def weighted_values(c, a, ss, ms):
        shifts = block_max_shifts(c, a)
        m = functools.reduce(jnp.maximum, [m_j + sh for m_j, sh in zip(ms, shifts)] + ms[c:])
        acc = None
        for j in range(c + 1):
            p = jnp.exp2(ss[j] - (m - shifts[j] if j < c else m)).astype(BF16)
            part = _dot(vt_ref[a, :, j * BLK:(j + 1) * BLK], p)
            acc = part if acc is None else acc + part
            if j < c:
                yield
        ot = acc[0:dh] * (1.0 / acc[dh:dh + 1])
        o_ref[0, c * BLK:(c + 1) * BLK, a * dh:(a + 1) * dh] = ot.T.astype(o_ref.dtype)
        yield

    prepare_tables()
    kmeans = block_means()
    tasks = [(c, a) for c in range(nblk) for a in range(HP)]
    ahead = {}
    for t in range(len(tasks) + MOBA_TASK_LOOKAHEAD):
        running = []
        if t < len(tasks):
            ahead[t] = ([], [])
            running.append(scores(*tasks[t], *ahead[t]))
        d = t - MOBA_TASK_LOOKAHEAD
        if d >= 0:
            c, a = tasks[d]
            if d == 0:
                prepare_values()
            if (c, a) == (MOBA_TOPK, 0):
                prepare_gate(kmeans)
            running.append(weighted_values(c, a, *ahead.pop(d)))
        while running:
            running = [g for g in running if next(g, StopIteration) is not StopIteration]


def _moba(z3, slopes, side):
    B, S, _ = z3.shape
    BLK = MOBA_BLOCK
    HP = MOBA_HEADS_PER_STEP
    w = HP * HEAD_DIM_B
    qc, kc, vc = COL_QB // w, COL_KB // w, COL_VB // w
    n_hp = N_H_B // HP
    side_in, side_out, side_shapes = _side_cast_specs(side, B * n_hp, lambda b, h: b * n_hp + h)
    outs = pl.pallas_call(
        _moba_kernel,
        grid=(B, n_hp),
        in_specs=[
            pl.BlockSpec(memory_space=pltpu.SMEM),
            pl.BlockSpec((1, S, w), lambda b, h: (b, 0, qc + h)),
            pl.BlockSpec((1, S, w), lambda b, h: (b, 0, kc + h)),
            pl.BlockSpec((1, S, w), lambda b, h: (b, 0, vc + h)),
        ] + side_in,
        out_specs=[pl.BlockSpec((1, S, w), lambda b, h: (b, 0, h))] + side_out,
        out_shape=[jax.ShapeDtypeStruct((B, S, W_B), BF16)] + side_shapes,
        scratch_shapes=[
            pltpu.VMEM((HP, HEAD_DIM_B + BF16_SUBLANE_TILE, S), BF16),
            pltpu.VMEM((HP, GATE_ROWS, S), F32),
            pltpu.VMEM((HP, 2, BLK, BLK), F32),
        ],
        compiler_params=_params(("arbitrary", "arbitrary")),
        name="moba",
    )(slopes, z3, z3, z3, *[s[0] if isinstance(s, tuple) else s for s in side])
    return outs[0], outs[1:]


def _sigmoid(x):
    return 1.0 / (1.0 + jnp.exp(-x))


def _merge_out_kernel(oa_ref, ob_ref, ga_ref, gb_ref, ba_ref, bb_ref, wa_ref, wb_ref, wo_ref, x_ref,
                      g2_ref, o_ref, u2_ref, mixed_ref, *, chunk):
    oa = oa_ref[...]
    ob = ob_ref[...]
    for c0 in range(0, mixed_ref.shape[1], chunk):
        cs = slice(c0, c0 + chunk)
        a = _dot(oa, wa_ref[:, cs])
        b = _dot(ob, wb_ref[:, cs])
        ga = _sigmoid(ga_ref[:, cs].astype(F32) + ba_ref[:, cs])
        gb = _sigmoid(gb_ref[:, cs].astype(F32) + bb_ref[:, cs])
        mixed_ref[:, cs] = (ga * a + gb * b).astype(mixed_ref.dtype)
    h1 = x_ref[...] + _dot(mixed_ref[...], wo_ref[...])
    o_ref[...] = h1
    inv = lax.rsqrt(jnp.mean(h1 * h1, axis=-1, keepdims=True) + RMS_EPS)
    u2_ref[...] = (h1 * inv * g2_ref[...]).astype(u2_ref.dtype)


def _merge_out(oa, ob, gates, ba, bb, wa, wb, wo, x, g2, *, tm=512, chunk=512):
    m, ka = oa.shape
    kb = ob.shape[1]
    d = wa.shape[1]
    resident = pl.Buffered(1)
    return pl.pallas_call(
        functools.partial(_merge_out_kernel, chunk=chunk),
        grid=(m // tm,),
        in_specs=[
            pl.BlockSpec((tm, ka), lambda i: (i, 0)),
            pl.BlockSpec((tm, kb), lambda i: (i, 0)),
            pl.BlockSpec((tm, d), lambda i: (i, 0)),
            pl.BlockSpec((tm, d), lambda i: (i, 1)),
            pl.BlockSpec((1, d), lambda i: (0, 0)),
            pl.BlockSpec((1, d), lambda i: (0, 0)),
            pl.BlockSpec((ka, d), lambda i: (0, 0), pipeline_mode=resident),
            pl.BlockSpec((kb, d), lambda i: (0, 0), pipeline_mode=resident),
            pl.BlockSpec((d, d), lambda i: (0, 0), pipeline_mode=resident),
            pl.BlockSpec((tm, d), lambda i: (i, 0)),
            pl.BlockSpec((1, d), lambda i: (0, 0)),
        ],
        out_specs=[pl.BlockSpec((tm, d), lambda i: (i, 0)), pl.BlockSpec((tm, d), lambda i: (i, 0))],
        out_shape=[jax.ShapeDtypeStruct((m, d), F32), jax.ShapeDtypeStruct((m, d), BF16)],
        scratch_shapes=[pltpu.VMEM((tm, d), BF16)],
        compiler_params=_params(("parallel",)),
        name="merge_out",
    )(oa, ob, gates, gates, ba, bb, wa, wb, wo, x, g2)


def _mm_res_kernel(a_ref, w_ref, r_ref, o_ref):
    o_ref[...] = r_ref[...] + _dot(a_ref[...], w_ref[...])


def _mm_res(a, w, res, *, tm, tn, name):
    m, k = a.shape
    n = w.shape[1]
    return pl.pallas_call(
        _mm_res_kernel,
        grid=(m // tm, n // tn),
        in_specs=[
            pl.BlockSpec((tm, k), lambda i, j: (i, 0)),
            pl.BlockSpec((k, tn), lambda i, j: (0, j)),
            pl.BlockSpec((tm, tn), lambda i, j: (i, j)),
        ],
        out_specs=pl.BlockSpec((tm, tn), lambda i, j: (i, j)),
        out_shape=jax.ShapeDtypeStruct((m, n), F32),
        compiler_params=_params(("parallel", "parallel")),
        name=name,
    )(a, w, res)


def _ffn_up_kernel(u_ref, wg_ref, wu_ref, o_ref):
    u = u_ref[...]
    for c0 in range(0, o_ref.shape[1], V7X_MXU_DIM):
        cs = slice(c0, c0 + V7X_MXU_DIM)
        gate = _dot(u, wg_ref[:, cs])
        up = _dot(u, wu_ref[:, cs])
        o_ref[:, cs] = (gate * _sigmoid(gate) * up).astype(o_ref.dtype)


def _ffn_up(u, wg, wu, *, tm=2048, tn=512):
    m, d = u.shape
    n = wg.shape[1]
    return pl.pallas_call(
        _ffn_up_kernel,
        grid=(m // tm, n // tn),
        in_specs=[
            pl.BlockSpec((tm, d), lambda i, j: (i, 0)),
            pl.BlockSpec((d, tn), lambda i, j: (0, j)),
            pl.BlockSpec((d, tn), lambda i, j: (0, j)),
        ],
        out_specs=pl.BlockSpec((tm, tn), lambda i, j: (i, j)),
        out_shape=jax.ShapeDtypeStruct((m, n), BF16),
        compiler_params=_params(("parallel", "parallel")),
        name="ffn_up",
    )(u, wg, wu)


def kernel(x, norm1_g, w_in, b_gate, q_norm_a, k_norm_a, sinks_a, q_norm_b, k_norm_b,
           w_branch_a, w_branch_b, w_o, norm2_g, w_ffn_gate, w_ffn_up, w_ffn_down):
    B, S, D = x.shape
    depth = w_in.shape[0]
    assert D == D_MODEL and w_in.shape[2] == IN_COLS
    assert S % MOBA_BLOCK == 0 and S % WINDOW == 0
    M = B * S

    slopes_b = jnp.asarray(np.exp2(-8.0 * np.arange(1, N_H_B + 1, dtype=np.float32) / N_H_B), F32)
    swa_bias = _swa_bias_table()

    h = x.reshape(M, D)
    for l in range(depth):
        head_gain = jnp.concatenate([
            jnp.tile(q_norm_a[l], N_Q_A) * (HEAD_DIM_A ** -0.5 * LOG2E),
            jnp.tile(k_norm_a[l], N_KV_A),
            jnp.ones((WKV_A,), F32),
            jnp.tile(q_norm_b[l], N_H_B) * (HEAD_DIM_B ** -0.5 * LOG2E),
            jnp.tile(k_norm_b[l], N_H_B),
            jnp.ones((W_B,), F32),
        ]).reshape(1, COL_GA)
        qkv, u = _qkv_proj(h, norm1_g[l].reshape(1, D), w_in[l, :, :COL_GA].astype(BF16), head_gain, tm=512)
        z3 = qkv.reshape(B, S, COL_GA)
        o_a, (wg_b, wu_b) = _swa(z3, sinks_a[l], swa_bias, [w_ffn_gate[l], w_ffn_up[l]])
        o_b, (wd_b, wgate_b) = _moba(z3, slopes_b,
                                     [w_ffn_down[l], (w_in[l], COL_GA, 2 * D, SIDE_CAST_COL_TILE)])
        gates, (wo_b, wa_b, wb_b) = _gate_proj(u, wgate_b, [w_o[l], w_branch_a[l], w_branch_b[l]])
        h1, u2 = _merge_out(o_a.reshape(M, WQ_A), o_b.reshape(M, W_B), gates,
                            b_gate[l, :D].reshape(1, D), b_gate[l, D:].reshape(1, D),
                            wa_b, wb_b, wo_b, h, norm2_g[l].reshape(1, D))
        act = _ffn_up(u2, wg_b, wu_b)
        h = _mm_res(act, wd_b, h1, tm=1024, tn=512, name="ffn_down")
    return h.reshape(B, S, D)
```

```python
import functools

import numpy as np
import jax
import jax.numpy as jnp
from jax import lax
from jax.experimental import pallas as pl
from jax.experimental.pallas import tpu as pltpu

F32 = jnp.float32
BF16 = jnp.bfloat16

D_MODEL = 2048
HEAD_DIM_A = 64
N_Q_A = 16
N_KV_A = 4
WINDOW = 128
HEAD_DIM_B = 128
N_H_B = 8
MOBA_BLOCK = 256
MOBA_TOPK = 3
RMS_EPS = 1e-6

WQ_A = N_Q_A * HEAD_DIM_A
WKV_A = N_KV_A * HEAD_DIM_A
W_B = N_H_B * HEAD_DIM_B
COL_QA = 0
COL_KA = COL_QA + WQ_A
COL_VA = COL_KA + WKV_A
COL_QB = COL_VA + WKV_A
COL_KB = COL_QB + W_B
COL_VB = COL_KB + W_B
COL_GA = COL_VB + W_B
COL_GB = COL_GA + D_MODEL
IN_COLS = COL_GB + D_MODEL

V7X_LANES = 128
V7X_MXU_DIM = 256
VMEM_LIMIT_BYTES = 56 * 1024 * 1024
NORM_ROW_CHUNK = 64
NORM_UNROLL = 2
BF16_SUBLANE_TILE = 16
SIDE_CAST_COL_TILE = 512
SWA_QBLOCKS_PER_STEP = 8
SWA_SCORE_LOOKAHEAD = 8
MOBA_HEADS_PER_STEP = 2
MOBA_TASK_LOOKAHEAD = 4
GATE_ROWS = 16
NEG_INF = float("-inf")
LOG2E = 1.4426950408889634


def _params(sem):
    return pltpu.CompilerParams(dimension_semantics=sem, vmem_limit_bytes=VMEM_LIMIT_BYTES)


def _dot(a, b):
    return jnp.dot(a, b, preferred_element_type=F32)


def _dot_nt(a, b):
    return lax.dot_general(a, b, (((1,), (1,)), ((), ())), preferred_element_type=F32)


def _split_bf16(x):
    hi = x.astype(BF16)
    lo = (x - hi.astype(F32)).astype(BF16)
    return hi, lo


def _rms_rows_to_bf16(x_ref, g_ref, u_ref):
    rows = x_ref.shape[0]
    g = g_ref[...]

    def body(c, carry):
        r = pl.multiple_of(c * NORM_ROW_CHUNK, NORM_ROW_CHUNK)
        x = x_ref[pl.ds(r, NORM_ROW_CHUNK), :]
        ms = jnp.mean(x * x, axis=-1, keepdims=True)
        u_ref[pl.ds(r, NORM_ROW_CHUNK), :] = (x * lax.rsqrt(ms + RMS_EPS) * g).astype(BF16)
        return carry

    lax.fori_loop(0, rows // NORM_ROW_CHUNK, body, 0, unroll=NORM_UNROLL)


def _head_norm_kind(col):
    if col < COL_VA:
        return "norm64"
    if COL_QB <= col < COL_VB:
        return "norm128"
    return "copy"


def _qkv_proj_kernel(x_ref, g_ref, w_ref, hg_ref, o_ref, u_ref):
    _rms_rows_to_bf16(x_ref, g_ref, u_ref)
    CH = V7X_MXU_DIM

    def finish(z, cs, kind):
        if kind == "norm64":
            left = lax.broadcasted_iota(jnp.int32, (1, V7X_LANES), 1) < HEAD_DIM_A
            tiles = []
            for h0 in range(0, CH, V7X_LANES):
                zt = z[:, h0:h0 + V7X_LANES]
                sq = zt * zt
                ms_l = jnp.sum(jnp.where(left, sq, 0.0), axis=-1, keepdims=True)
                ms_r = jnp.sum(jnp.where(left, 0.0, sq), axis=-1, keepdims=True)
                ms = jnp.where(left, ms_l, ms_r) * (1.0 / HEAD_DIM_A)
                tiles.append(zt * lax.rsqrt(ms + RMS_EPS))
            z = jnp.concatenate(tiles, axis=1) * hg_ref[:, cs]
        elif kind == "norm128":
            halves = []
            for h0 in range(0, CH, HEAD_DIM_B):
                zh = z[:, h0:h0 + HEAD_DIM_B]
                ms = jnp.mean(zh * zh, axis=-1, keepdims=True)
                halves.append(zh * lax.rsqrt(ms + RMS_EPS))
            z = jnp.concatenate(halves, axis=1) * hg_ref[:, cs]
        o_ref[:, cs] = z.astype(o_ref.dtype)

    u = u_ref[...]
    chunks = [slice(c0, c0 + CH) for c0 in range(0, o_ref.shape[1], CH)]
    z_next = _dot(u, w_ref[:, chunks[0]])
    for idx, cs in enumerate(chunks):
        z = z_next
        if idx + 1 < len(chunks):
            z_next = _dot(u, w_ref[:, chunks[idx + 1]])
        finish(z, cs, _head_norm_kind(cs.start))


def _qkv_proj(x, g, w, head_gain, *, tm):
    m, d = x.shape
    n = w.shape[1]
    assert n % V7X_MXU_DIM == 0
    resident = pl.Buffered(1)
    return pl.pallas_call(
        _qkv_proj_kernel,
        grid=(m // tm,),
        in_specs=[
            pl.BlockSpec((tm, d), lambda i: (i, 0)),
            pl.BlockSpec((1, d), lambda i: (0, 0)),
            pl.BlockSpec((d, n), lambda i: (0, 0), pipeline_mode=resident),
            pl.BlockSpec((1, n), lambda i: (0, 0)),
        ],
        out_specs=[
            pl.BlockSpec((tm, n), lambda i: (i, 0)),
            pl.BlockSpec((tm, d), lambda i: (i, 0)),
        ],
        out_shape=[jax.ShapeDtypeStruct((m, n), BF16), jax.ShapeDtypeStruct((m, d), BF16)],
        compiler_params=_params(("parallel",)),
        name="qkv_proj",
    )(x, g, w, head_gain)


def _side_cast_specs(arrays, n_steps, step_index):
    in_specs, out_specs, shapes = [], [], []
    for item in arrays:
        w, col0, ncols, col_tile = item if isinstance(item, tuple) else (item, 0, item.shape[1], item.shape[1])
        rows = w.shape[0]
        n_col = ncols // col_tile
        n_row = n_steps // n_col
        chunk = rows // n_row
        assert n_col * col_tile == ncols and n_row * n_col == n_steps and col0 % col_tile == 0, (w.shape, n_steps)
        assert chunk * n_row == rows and chunk % BF16_SUBLANE_TILE == 0, (w.shape, n_steps)

        def in_map(*g, n_col=n_col, c0=col0 // col_tile):
            s = step_index(*g)
            return (s // n_col, c0 + s % n_col)

        def out_map(*g, n_col=n_col):
            s = step_index(*g)
            return (s // n_col, s % n_col)

        in_specs.append(pl.BlockSpec((chunk, col_tile), in_map))
        out_specs.append(pl.BlockSpec((chunk, col_tile), out_map))
        shapes.append(jax.ShapeDtypeStruct((rows, ncols), BF16))
    return in_specs, out_specs, shapes


def _side_cast(refs):
    n = len(refs) // 2
    for src, dst in zip(refs[:n], refs[n:]):
        dst[...] = src[...].astype(dst.dtype)


def _gate_proj_kernel(u_ref, w_ref, *refs):
    n_side = (len(refs) - 1) // 2
    o_ref = refs[n_side]
    o_ref[...] = _dot(u_ref[...], w_ref[...]).astype(o_ref.dtype)
    _side_cast(refs[:n_side] + refs[n_side + 1:])


def _gate_proj(u, w, side, *, tm=2048, tn=1024):
    m, k = u.shape
    n = w.shape[1]
    gi, gj = m // tm, n // tn
    side_in, side_out, side_shapes = _side_cast_specs(side, gi * gj, lambda i, j: i * gj + j)
    outs = pl.pallas_call(
        _gate_proj_kernel,
        grid=(gi, gj),
        in_specs=[
            pl.BlockSpec((tm, k), lambda i, j: (i, 0)),
            pl.BlockSpec((k, tn), lambda i, j: (0, j)),
        ] + side_in,
        out_specs=[pl.BlockSpec((tm, tn), lambda i, j: (i, j))] + side_out,
        out_shape=[jax.ShapeDtypeStruct((m, n), BF16)] + side_shapes,
        compiler_params=_params(("arbitrary", "arbitrary")),
        name="gate_proj",
    )(u, w, *side)
    return outs[0], outs[1:]


def _swa_kernel(sinks_ref, q_ref, kc_ref, kp_ref, vc_ref, vp_ref, bias_ref, *refs):
    n_side = (len(refs) - 1) // 2
    o_ref = refs[n_side]
    _side_cast(refs[:n_side] + refs[n_side + 1:])
    L = WINDOW
    QB = SWA_QBLOCKS_PER_STEP
    first_step = (pl.program_id(1) == 0).astype(jnp.int32)
    k = jnp.concatenate([kp_ref[0], kc_ref[0]], axis=0).astype(F32)
    v = jnp.concatenate([vp_ref[0], vc_ref[0]], axis=0).astype(F32)

    lane = lax.broadcasted_iota(jnp.int32, (1, V7X_LANES), 1)
    left = lane < HEAD_DIM_A
    lo_head = lax.broadcasted_iota(jnp.int32, (1, 2 * L), 1) < L

    n_kt = WKV_A // V7X_LANES
    kt = [k[:, t * 128:(t + 1) * 128] for t in range(n_kt)]
    kt_sw = [pltpu.roll(x, HEAD_DIM_A, axis=1).astype(BF16) for x in kt]
    kt = [x.astype(BF16) for x in kt]
    pad_row = lax.broadcasted_iota(jnp.int32, (BF16_SUBLANE_TILE, k.shape[0]), 0)
    ones_row = jnp.where(pad_row == 0, 1.0, 0.0)
    vt = [jnp.concatenate([v[:, t * 128:(t + 1) * 128].T, ones_row], axis=0).astype(BF16) for t in range(n_kt)]

    def q_masked(qb, h):
        tile = h // 2
        qt = q_ref[0, qb * L:(qb + 1) * L, tile * 128:(tile + 1) * 128]
        keep = left if h % 2 == 0 else jnp.logical_not(left)
        return jnp.where(keep, qt, jnp.zeros_like(qt))

    rep = N_Q_A // N_KV_A

    def scores(qb, g, hpar):
        t, par = g // 2, g % 2
        ha, hb = rep * g + hpar, rep * g + hpar + 2
        k_al = (kt if par == hpar else kt_sw)[t][qb * L:(qb + 2) * L]
        qm = jnp.concatenate([q_masked(qb, ha), q_masked(qb, hb)], axis=0)
        first = first_step if qb == 0 else 0
        return _dot_nt(k_al, qm) + bias_ref[first, 2 * g + hpar]

    def weighted_values(qb, g, hpar, s, out_rows):
        t, par = g // 2, g % 2
        ha, hb = rep * g + hpar, rep * g + hpar + 2
        sink = jnp.where(lo_head, sinks_ref[ha], sinks_ref[hb]) * LOG2E
        m = jnp.maximum(jnp.max(s, axis=0, keepdims=True), sink)
        e = jnp.exp2(s - m).astype(BF16)
        ot = _dot(vt[t][:, qb * L:(qb + 2) * L], e)
        denom = ot[V7X_LANES:V7X_LANES + 1, :] + jnp.exp2(sink - m)
        og = ot[par * HEAD_DIM_A:(par + 1) * HEAD_DIM_A, :] * (1.0 / denom)
        out_rows[ha] = og[:, :L]
        out_rows[hb] = og[:, L:]

    tasks = [(qb, g, hpar) for qb in range(QB) for g in range(N_KV_A) for hpar in range(2)]
    ahead = {}
    out_rows = {}
    for t in range(len(tasks) + SWA_SCORE_LOOKAHEAD):
        if t < len(tasks):
            ahead[t] = scores(*tasks[t])
        d = t - SWA_SCORE_LOOKAHEAD
        if d >= 0:
            qb, g, hpar = tasks[d]
            rows = out_rows.setdefault(qb, [None] * N_Q_A)
            weighted_values(qb, g, hpar, ahead.pop(d), rows)
            if (g, hpar) == (N_KV_A - 1, 1):
                o_t = jnp.concatenate(out_rows.pop(qb), axis=0)
                o_ref[0, qb * L:(qb + 1) * L, :] = o_t.T.astype(o_ref.dtype)


def _swa(z3, sinks, bias, side):
    B, S, _ = z3.shape
    L = WINDOW
    QB = SWA_QBLOCKS_PER_STEP
    nb = S // (QB * L)
    kblk = COL_KA // WKV_A
    vblk = COL_VA // WKV_A
    prev = lambda n: jnp.maximum(n * QB - 1, 0)
    side_in, side_out, side_shapes = _side_cast_specs(side, B * nb, lambda b, n: b * nb + n)
    outs = pl.pallas_call(
        _swa_kernel,
        grid=(B, nb),
        in_specs=[
            pl.BlockSpec(memory_space=pltpu.SMEM),
            pl.BlockSpec((1, QB * L, WQ_A), lambda b, n: (b, n, 0)),
            pl.BlockSpec((1, QB * L, WKV_A), lambda b, n: (b, n, kblk)),
            pl.BlockSpec((1, L, WKV_A), lambda b, n: (b, prev(n), kblk)),
            pl.BlockSpec((1, QB * L, WKV_A), lambda b, n: (b, n, vblk)),
            pl.BlockSpec((1, L, WKV_A), lambda b, n: (b, prev(n), vblk)),
            pl.BlockSpec((2, N_Q_A // 2, 2 * L, 2 * L), lambda b, n: (0, 0, 0, 0)),
        ] + side_in,
        out_specs=[pl.BlockSpec((1, QB * L, WQ_A), lambda b, n: (b, n, 0))] + side_out,
        out_shape=[jax.ShapeDtypeStruct((B, S, WQ_A), BF16)] + side_shapes,
        compiler_params=_params(("arbitrary", "arbitrary")),
        name="swa",
    )(sinks, z3, z3, z3, z3, z3, bias, *side)
    return outs[0], outs[1:]


def _swa_bias_table():
    L = WINDOW
    rep = N_Q_A // N_KV_A
    slopes = np.exp2(-8.0 * np.arange(1, N_Q_A + 1, dtype=np.float32) / N_Q_A).astype(np.float32)
    kj = np.arange(2 * L)[:, None]
    qi = np.arange(L)[None, :]
    dist = L + qi - kj
    window = (dist >= 0) & (dist < WINDOW)
    table = np.empty((2, N_Q_A // 2, 2 * L, 2 * L), np.float32)
    for first in range(2):
        valid = window & ((kj >= L) if first else True)
        for g in range(N_KV_A):
            for hpar in range(2):
                for a in range(2):
                    h = rep * g + hpar + 2 * a
                    table[first, 2 * g + hpar, :, a * L:(a + 1) * L] = np.where(
                        valid, -(slopes[h] * LOG2E) * dist.astype(np.float32), -np.inf)
    return jnp.asarray(table)


def _moba_kernel(slopes_ref, q_ref, k_ref, v_ref, *refs):
    n_side = (len(refs) - 4) // 2
    o_ref = refs[n_side]
    vt_ref, gate_ref, tab_ref = refs[2 * n_side + 1:]
    _side_cast(refs[:n_side] + refs[n_side + 1:2 * n_side + 1])
    BLK = MOBA_BLOCK
    HP = MOBA_HEADS_PER_STEP
    dh = HEAD_DIM_B
    hp = pl.program_id(1)
    S = k_ref.shape[1]
    nblk = S // BLK

    def prepare_tables():
        kc = lax.broadcasted_iota(jnp.int32, (BLK, BLK), 0)
        qr = lax.broadcasted_iota(jnp.int32, (BLK, BLK), 1)
        rel = (qr - kc).astype(F32)
        for a in range(HP):
            slope2 = slopes_ref[hp * HP + a] * LOG2E
            tab_ref[a, 0] = -slope2 * rel
            tab_ref[a, 1] = jnp.where(rel >= 0.0, -slope2 * rel, NEG_INF)

    def prepare_values():
        eye = (lax.broadcasted_iota(jnp.int32, (dh, dh), 0)
               == lax.broadcasted_iota(jnp.int32, (dh, dh), 1)).astype(BF16)
        pad_row = lax.broadcasted_iota(jnp.int32, (BF16_SUBLANE_TILE, S), 0)
        for a in range(HP):
            vt_ref[a, 0:dh, :] = _dot_nt(eye, v_ref[0, :, a * dh:(a + 1) * dh]).astype(BF16)
            vt_ref[a, dh:, :] = jnp.where(pad_row == 0, 1.0, 0.0).astype(BF16)

    def block_means():
        blk = lax.broadcasted_iota(jnp.int32, (GATE_ROWS, S), 0)
        pos = lax.broadcasted_iota(jnp.int32, (GATE_ROWS, S), 1)
        member = jnp.where((pos >= blk * BLK) & (pos < (blk + 1) * BLK), 1.0 / BLK, 0.0).astype(BF16)
        return [_split_bf16(_dot(member, k_ref[0, :, a * dh:(a + 1) * dh])) for a in range(HP)]

    def prepare_gate(kmeans):
        for a, (km_hi, km_lo) in enumerate(kmeans):
            qn = q_ref[0, :, a * dh:(a + 1) * dh]
            gate_ref[a] = _dot_nt(km_hi, qn) + _dot_nt(km_lo, qn)

    def block_max_shifts(c, a):
        slope2 = slopes_ref[hp * HP + a] * LOG2E
        shifts = [-slope2 * float((c - n) * BLK) for n in range(c)]
        if c > MOBA_TOPK:
            gate = gate_ref[a, :, c * BLK:(c + 1) * BLK]
            blk = lax.broadcasted_iota(jnp.int32, (GATE_ROWS, 1), 0)
            past = blk < c
            for n in range(c):
                g_n = gate[n:n + 1, :]
                beats = ((gate > g_n) | ((gate == g_n) & (blk < n))) & past
                rank = jnp.sum(jnp.where(beats, 1.0, 0.0), axis=0, keepdims=True)
                shifts[n] = jnp.where(rank < float(MOBA_TOPK), shifts[n], NEG_INF)
        return shifts

    def scores(c, a, ss, ms):
        for j in range(c + 1):
            s = _dot_nt(k_ref[0, j * BLK:(j + 1) * BLK, a * dh:(a + 1) * dh],
                        q_ref[0, c * BLK:(c + 1) * BLK, a * dh:(a + 1) * dh])
            s = s + tab_ref[a, 1 if j == c else 0]
            ss.append(s)
            ms.append(jnp.max(s, axis=0, keepdims=True))
            yield

    def weighted_values(c, a, ss, ms):
        shifts = block_max_shifts(c, a)
        m = functools.reduce(jnp.maximum, [m_j + sh for m_j, sh in zip(ms, shifts)] + ms[c:])
        acc = None
        for j in range(c + 1):
            p = jnp.exp2(ss[j] - (m - shifts[j] if j < c else m)).astype(BF16)
            part = _dot(vt_ref[a, :, j * BLK:(j + 1) * BLK], p)
            acc = part if acc is None else acc + part
            if j < c:
                yield
        ot = acc[0:dh] * (1.0 / acc[dh:dh + 1])
        o_ref[0, c * BLK:(c + 1) * BLK, a * dh:(a + 1) * dh] = ot.T.astype(o_ref.dtype)
        yield

    prepare_tables()
    kmeans = block_means()
    tasks = [(c, a) for c in range(nblk) for a in range(HP)]
    ahead = {}
    for t in range(len(tasks) + MOBA_TASK_LOOKAHEAD):
        running = []
        if t < len(tasks):
            ahead[t] = ([], [])
            running.append(scores(*tasks[t], *ahead[t]))
        d = t - MOBA_TASK_LOOKAHEAD
        if d >= 0:
            c, a = tasks[d]
            if d == 0:
                prepare_values()
            if (c, a) == (MOBA_TOPK, 0):
                prepare_gate(kmeans)
            running.append(weighted_values(c, a, *ahead.pop(d)))
        while running:
            running = [g for g in running if next(g, StopIteration) is not StopIteration]


def _moba(z3, slopes, side):
    B, S, _ = z3.shape
    BLK = MOBA_BLOCK
    HP = MOBA_HEADS_PER_STEP
    w = HP * HEAD_DIM_B
    qc, kc, vc = COL_QB // w, COL_KB // w, COL_VB // w
    n_hp = N_H_B // HP
    side_in, side_out, side_shapes = _side_cast_specs(side, B * n_hp, lambda b, h: b * n_hp + h)
    outs = pl.pallas_call(
        _moba_kernel,
        grid=(B, n_hp),
        in_specs=[
            pl.BlockSpec(memory_space=pltpu.SMEM),
            pl.BlockSpec((1, S, w), lambda b, h: (b, 0, qc + h)),
            pl.BlockSpec((1, S, w), lambda b, h: (b, 0, kc + h)),
            pl.BlockSpec((1, S, w), lambda b, h: (b, 0, vc + h)),
        ] + side_in,
        out_specs=[pl.BlockSpec((1, S, w), lambda b, h: (b, 0, h))] + side_out,
        out_shape=[jax.ShapeDtypeStruct((B, S, W_B), BF16)] + side_shapes,
        scratch_shapes=[
            pltpu.VMEM((HP, HEAD_DIM_B + BF16_SUBLANE_TILE, S), BF16),
            pltpu.VMEM((HP, GATE_ROWS, S), F32),
            pltpu.VMEM((HP, 2, BLK, BLK), F32),
        ],
        compiler_params=_params(("arbitrary", "arbitrary")),
        name="moba",
    )(slopes, z3, z3, z3, *[s[0] if isinstance(s, tuple) else s for s in side])
    return outs[0], outs[1:]


def _sigmoid(x):
    return 1.0 / (1.0 + jnp.exp2(x * (-LOG2E)))


def _merge_out_kernel(oa_ref, ob_ref, ga_ref, gb_ref, ba_ref, bb_ref, wa_ref, wb_ref, wo_ref, x_ref,
                      g2_ref, o_ref, u2_ref, mixed_ref, *, chunk):
    oa = oa_ref[...]
    ob = ob_ref[...]
    for c0 in range(0, mixed_ref.shape[1], chunk):
        cs = slice(c0, c0 + chunk)
        a = _dot(oa, wa_ref[:, cs])
        b = _dot(ob, wb_ref[:, cs])
        ga = _sigmoid(ga_ref[:, cs].astype(F32) + ba_ref[:, cs])
        gb = _sigmoid(gb_ref[:, cs].astype(F32) + bb_ref[:, cs])
        mixed_ref[:, cs] = (ga * a + gb * b).astype(mixed_ref.dtype)
    h1 = x_ref[...] + _dot(mixed_ref[...], wo_ref[...])
    o_ref[...] = h1
    inv = lax.rsqrt(jnp.mean(h1 * h1, axis=-1, keepdims=True) + RMS_EPS)
    u2_ref[...] = (h1 * inv * g2_ref[...]).astype(u2_ref.dtype)


def _merge_out(oa, ob, gates, ba, bb, wa, wb, wo, x, g2, *, tm=512, chunk=512):
    m, ka = oa.shape
    kb = ob.shape[1]
    d = wa.shape[1]
    resident = pl.Buffered(1)
    return pl.pallas_call(
        functools.partial(_merge_out_kernel, chunk=chunk),
        grid=(m // tm,),
        in_specs=[
            pl.BlockSpec((tm, ka), lambda i: (i, 0)),
            pl.BlockSpec((tm, kb), lambda i: (i, 0)),
            pl.BlockSpec((tm, d), lambda i: (i, 0)),
            pl.BlockSpec((tm, d), lambda i: (i, 1)),
            pl.BlockSpec((1, d), lambda i: (0, 0)),
            pl.BlockSpec((1, d), lambda i: (0, 0)),
            pl.BlockSpec((ka, d), lambda i: (0, 0), pipeline_mode=resident),
            pl.BlockSpec((kb, d), lambda i: (0, 0), pipeline_mode=resident),
            pl.BlockSpec((d, d), lambda i: (0, 0), pipeline_mode=resident),
            pl.BlockSpec((tm, d), lambda i: (i, 0)),
            pl.BlockSpec((1, d), lambda i: (0, 0)),
        ],
        out_specs=[pl.BlockSpec((tm, d), lambda i: (i, 0)), pl.BlockSpec((tm, d), lambda i: (i, 0))],
        out_shape=[jax.ShapeDtypeStruct((m, d), F32), jax.ShapeDtypeStruct((m, d), BF16)],
        scratch_shapes=[pltpu.VMEM((tm, d), BF16)],
        compiler_params=_params(("parallel",)),
        name="merge_out",
    )(oa, ob, gates, gates, ba, bb, wa, wb, wo, x, g2)


def _mm_res_kernel(a_ref, w_ref, r_ref, o_ref):
    o_ref[...] = r_ref[...] + _dot(a_ref[...], w_ref[...])


def _mm_res(a, w, res, *, tm, tn, name):
    m, k = a.shape
    n = w.shape[1]
    return pl.pallas_call(
        _mm_res_kernel,
        grid=(m // tm, n // tn),
        in_specs=[
            pl.BlockSpec((tm, k), lambda i, j: (i, 0)),
            pl.BlockSpec((k, tn), lambda i, j: (0, j)),
            pl.BlockSpec((tm, tn), lambda i, j: (i, j)),
        ],
        out_specs=pl.BlockSpec((tm, tn), lambda i, j: (i, j)),
        out_shape=jax.ShapeDtypeStruct((m, n), F32),
        compiler_params=_params(("parallel", "parallel")),
        name=name,
    )(a, w, res)


def _ffn_up_kernel(u_ref, wg_ref, wu_ref, o_ref):
    u = u_ref[...]
    for c0 in range(0, o_ref.shape[1], V7X_MXU_DIM):
        cs = slice(c0, c0 + V7X_MXU_DIM)
        gate = _dot(u, wg_ref[:, cs])
        up = _dot(u, wu_ref[:, cs])
        o_ref[:, cs] = (gate * _sigmoid(gate) * up).astype(o_ref.dtype)


def _ffn_up(u, wg, wu, *, tm=2048, tn=512):
    m, d = u.shape
    n = wg.shape[1]
    return pl.pallas_call(
        _ffn_up_kernel,
        grid=(m // tm, n // tn),
        in_specs=[
            pl.BlockSpec((tm, d), lambda i, j: (i, 0)),
            pl.BlockSpec((d, tn), lambda i, j: (0, j)),
            pl.BlockSpec((d, tn), lambda i, j: (0, j)),
        ],
        out_specs=pl.BlockSpec((tm, tn), lambda i, j: (i, j)),
        out_shape=jax.ShapeDtypeStruct((m, n), BF16),
        compiler_params=_params(("parallel", "parallel")),
        name="ffn_up",
    )(u, wg, wu)


def kernel(x, norm1_g, w_in, b_gate, q_norm_a, k_norm_a, sinks_a, q_norm_b, k_norm_b,
           w_branch_a, w_branch_b, w_o, norm2_g, w_ffn_gate, w_ffn_up, w_ffn_down):
    B, S, D = x.shape
    depth = w_in.shape[0]
    assert D == D_MODEL and w_in.shape[2] == IN_COLS
    assert S % MOBA_BLOCK == 0 and S % WINDOW == 0
    M = B * S

    slopes_b = jnp.asarray(np.exp2(-8.0 * np.arange(1, N_H_B + 1, dtype=np.float32) / N_H_B), F32)
    swa_bias = _swa_bias_table()

    h = x.reshape(M, D)
    for l in range(depth):
        head_gain = jnp.concatenate([
            jnp.tile(q_norm_a[l], N_Q_A) * (HEAD_DIM_A ** -0.5 * LOG2E),
            jnp.tile(k_norm_a[l], N_KV_A),
            jnp.ones((WKV_A,), F32),
            jnp.tile(q_norm_b[l], N_H_B) * (HEAD_DIM_B ** -0.5 * LOG2E),
            jnp.tile(k_norm_b[l], N_H_B),
            jnp.ones((W_B,), F32),
        ]).reshape(1, COL_GA)
        qkv, u = _qkv_proj(h, norm1_g[l].reshape(1, D), w_in[l, :, :COL_GA].astype(BF16), head_gain, tm=512)
        z3 = qkv.reshape(B, S, COL_GA)
        o_a, (wg_b, wu_b) = _swa(z3, sinks_a[l], swa_bias, [w_ffn_gate[l], w_ffn_up[l]])
        o_b, (wd_b, wgate_b) = _moba(z3, slopes_b,
                                     [w_ffn_down[l], (w_in[l], COL_GA, 2 * D, SIDE_CAST_COL_TILE)])
        gates, (wo_b, wa_b, wb_b) = _gate_proj(u, wgate_b, [w_o[l], w_branch_a[l], w_branch_b[l]])
        h1, u2 = _merge_out(o_a.reshape(M, WQ_A), o_b.reshape(M, W_B), gates,
                            b_gate[l, :D].reshape(1, D), b_gate[l, D:].reshape(1, D),
                            wa_b, wb_b, wo_b, h, norm2_g[l].reshape(1, D))
        act = _ffn_up(u2, wg_b, wu_b)
        h = _mm_res(act, wd_b, h1, tm=1024, tn=512, name="ffn_down")
    return h.reshape(B, S, D)
```

```python
import functools

import numpy as np
import jax
import jax.numpy as jnp
from jax import lax
from jax.experimental import pallas as pl
from jax.experimental.pallas import tpu as pltpu

F32 = jnp.float32
BF16 = jnp.bfloat16

D_MODEL = 2048
HEAD_DIM_A = 64
N_Q_A = 16
N_KV_A = 4
WINDOW = 128
HEAD_DIM_B = 128
N_H_B = 8
MOBA_BLOCK = 256
MOBA_TOPK = 3
RMS_EPS = 1e-6

WQ_A = N_Q_A * HEAD_DIM_A
WKV_A = N_KV_A * HEAD_DIM_A
W_B = N_H_B * HEAD_DIM_B
COL_QA = 0
COL_KA = COL_QA + WQ_A
COL_VA = COL_KA + WKV_A
COL_QB = COL_VA + WKV_A
COL_KB = COL_QB + W_B
COL_VB = COL_KB + W_B
COL_GA = COL_VB + W_B
COL_GB = COL_GA + D_MODEL
IN_COLS = COL_GB + D_MODEL

V7X_LANES = 128
V7X_MXU_DIM = 256
VMEM_LIMIT_BYTES = 56 * 1024 * 1024
NORM_ROW_CHUNK = 64
NORM_UNROLL = 2
BF16_SUBLANE_TILE = 16
SIDE_CAST_COL_TILE = 512
SWA_QBLOCKS_PER_STEP = 8
SWA_SCORE_LOOKAHEAD = 4
MOBA_HEADS_PER_STEP = 2
MOBA_TASK_LOOKAHEAD = 4
GATE_ROWS = BF16_SUBLANE_TILE
NEG_INF = float("-inf")
LOG2E = 1.4426950408889634


def _params(sem):
    return pltpu.CompilerParams(dimension_semantics=sem, vmem_limit_bytes=VMEM_LIMIT_BYTES)


def _dot(a, b):
    return jnp.dot(a, b, preferred_element_type=F32)


def _dot_nt(a, b):
    return lax.dot_general(a, b, (((1,), (1,)), ((), ())), preferred_element_type=F32)


def _split_bf16(x):
    hi = x.astype(BF16)
    lo = (x - hi.astype(F32)).astype(BF16)
    return hi, lo


def _rms_rows_to_bf16(x_ref, g_ref, u_ref):
    rows = x_ref.shape[0]
    g = g_ref[...]

    def body(c, carry):
        r = pl.multiple_of(c * NORM_ROW_CHUNK, NORM_ROW_CHUNK)
        x = x_ref[pl.ds(r, NORM_ROW_CHUNK), :]
        ms = jnp.mean(x * x, axis=-1, keepdims=True)
        u_ref[pl.ds(r, NORM_ROW_CHUNK), :] = (x * lax.rsqrt(ms + RMS_EPS) * g).astype(BF16)
        return carry

    lax.fori_loop(0, rows // NORM_ROW_CHUNK, body, 0, unroll=NORM_UNROLL)


def _head_norm_kind(col):
    if col < COL_VA:
        return "norm64"
    if COL_QB <= col < COL_VB:
        return "norm128"
    return "copy"


def _qkv_proj_kernel(x_ref, g_ref, w_ref, hg_ref, o_ref, u_ref):
    _rms_rows_to_bf16(x_ref, g_ref, u_ref)
    CH = V7X_MXU_DIM

    def finish(z, cs, kind):
        if kind == "norm64":
            left = lax.broadcasted_iota(jnp.int32, (1, V7X_LANES), 1) < HEAD_DIM_A
            tiles = []
            for h0 in range(0, CH, V7X_LANES):
                zt = z[:, h0:h0 + V7X_LANES]
                sq = zt * zt
                ms_l = jnp.sum(jnp.where(left, sq, 0.0), axis=-1, keepdims=True)
                ms_r = jnp.sum(jnp.where(left, 0.0, sq), axis=-1, keepdims=True)
                ms = jnp.where(left, ms_l, ms_r) * (1.0 / HEAD_DIM_A)
                tiles.append(zt * lax.rsqrt(ms + RMS_EPS))
            z = jnp.concatenate(tiles, axis=1) * hg_ref[:, cs]
        elif kind == "norm128":
            halves = []
            for h0 in range(0, CH, HEAD_DIM_B):
                zh = z[:, h0:h0 + HEAD_DIM_B]
                ms = jnp.mean(zh * zh, axis=-1, keepdims=True)
                halves.append(zh * lax.rsqrt(ms + RMS_EPS))
            z = jnp.concatenate(halves, axis=1) * hg_ref[:, cs]
        o_ref[:, cs] = z.astype(o_ref.dtype)

    u = u_ref[...]
    chunks = [slice(c0, c0 + CH) for c0 in range(0, o_ref.shape[1], CH)]
    z_next = _dot(u, w_ref[:, chunks[0]])
    for idx, cs in enumerate(chunks):
        z = z_next
        if idx + 1 < len(chunks):
            z_next = _dot(u, w_ref[:, chunks[idx + 1]])
        finish(z, cs, _head_norm_kind(cs.start))


def _qkv_proj(x, g, w, head_gain, *, tm):
    m, d = x.shape
    n = w.shape[1]
    assert n % V7X_MXU_DIM == 0
    resident = pl.Buffered(1)
    return pl.pallas_call(
        _qkv_proj_kernel,
        grid=(m // tm,),
        in_specs=[
            pl.BlockSpec((tm, d), lambda i: (i, 0)),
            pl.BlockSpec((1, d), lambda i: (0, 0)),
            pl.BlockSpec((d, n), lambda i: (0, 0), pipeline_mode=resident),
            pl.BlockSpec((1, n), lambda i: (0, 0)),
        ],
        out_specs=[
            pl.BlockSpec((tm, n), lambda i: (i, 0)),
            pl.BlockSpec((tm, d), lambda i: (i, 0)),
        ],
        out_shape=[jax.ShapeDtypeStruct((m, n), BF16), jax.ShapeDtypeStruct((m, d), BF16)],
        compiler_params=_params(("parallel",)),
        name="qkv_proj",
    )(x, g, w, head_gain)


def _side_cast_specs(arrays, n_steps, step_index):
    in_specs, out_specs, shapes = [], [], []
    for item in arrays:
        w, col0, ncols, col_tile = item if isinstance(item, tuple) else (item, 0, item.shape[1], item.shape[1])
        rows = w.shape[0]
        n_col = ncols // col_tile
        n_row = n_steps // n_col
        chunk = rows // n_row
        assert n_col * col_tile == ncols and n_row * n_col == n_steps and col0 % col_tile == 0, (w.shape, n_steps)
        assert chunk * n_row == rows and chunk % BF16_SUBLANE_TILE == 0, (w.shape, n_steps)

        def in_map(*g, n_col=n_col, c0=col0 // col_tile):
            s = step_index(*g)
            return (s // n_col, c0 + s % n_col)

        def out_map(*g, n_col=n_col):
            s = step_index(*g)
            return (s // n_col, s % n_col)

        in_specs.append(pl.BlockSpec((chunk, col_tile), in_map))
        out_specs.append(pl.BlockSpec((chunk, col_tile), out_map))
        shapes.append(jax.ShapeDtypeStruct((rows, ncols), BF16))
    return in_specs, out_specs, shapes


def _side_cast(refs):
    n = len(refs) // 2
    for src, dst in zip(refs[:n], refs[n:]):
        dst[...] = src[...].astype(dst.dtype)


def _gate_proj_kernel(u_ref, w_ref, *refs):
    n_side = (len(refs) - 1) // 2
    o_ref = refs[n_side]
    o_ref[...] = _dot(u_ref[...], w_ref[...]).astype(o_ref.dtype)
    _side_cast(refs[:n_side] + refs[n_side + 1:])


def _gate_proj(u, w, side, *, tm=2048, tn=1024):
    m, k = u.shape
    n = w.shape[1]
    gi, gj = m // tm, n // tn
    side_in, side_out, side_shapes = _side_cast_specs(side, gi * gj, lambda i, j: i * gj + j)
    outs = pl.pallas_call(
        _gate_proj_kernel,
        grid=(gi, gj),
        in_specs=[
            pl.BlockSpec((tm, k), lambda i, j: (i, 0)),
            pl.BlockSpec((k, tn), lambda i, j: (0, j)),
        ] + side_in,
        out_specs=[pl.BlockSpec((tm, tn), lambda i, j: (i, j))] + side_out,
        out_shape=[jax.ShapeDtypeStruct((m, n), BF16)] + side_shapes,
        compiler_params=_params(("arbitrary", "arbitrary")),
        name="gate_proj",
    )(u, w, *side)
    return outs[0], outs[1:]


def _swa_kernel(sinks_ref, q_ref, kc_ref, kp_ref, vc_ref, vp_ref, bias_ref, *refs):
    n_side = (len(refs) - 1) // 2
    o_ref = refs[n_side]
    _side_cast(refs[:n_side] + refs[n_side + 1:])
    L = WINDOW
    QB = SWA_QBLOCKS_PER_STEP
    first_step = (pl.program_id(1) == 0).astype(jnp.int32)
    k = jnp.concatenate([kp_ref[0], kc_ref[0]], axis=0).astype(F32)
    v = jnp.concatenate([vp_ref[0], vc_ref[0]], axis=0).astype(F32)

    lane = lax.broadcasted_iota(jnp.int32, (1, V7X_LANES), 1)
    left = lane < HEAD_DIM_A
    lo_head = lax.broadcasted_iota(jnp.int32, (1, 2 * L), 1) < L

    n_kt = WKV_A // V7X_LANES
    lane_tile = lambda t: slice(t * V7X_LANES, (t + 1) * V7X_LANES)
    kt = [k[:, lane_tile(t)] for t in range(n_kt)]
    kt_sw = [pltpu.roll(x, HEAD_DIM_A, axis=1).astype(BF16) for x in kt]
    kt = [x.astype(BF16) for x in kt]
    pad_row = lax.broadcasted_iota(jnp.int32, (BF16_SUBLANE_TILE, k.shape[0]), 0)
    ones_row = jnp.where(pad_row == 0, 1.0, 0.0)
    vt = [jnp.concatenate([v[:, lane_tile(t)].T, ones_row], axis=0).astype(BF16) for t in range(n_kt)]

    def q_masked(qb, h):
        qt = q_ref[0, qb * L:(qb + 1) * L, lane_tile(h // 2)]
        keep = left if h % 2 == 0 else jnp.logical_not(left)
        return jnp.where(keep, qt, jnp.zeros_like(qt))

    rep = N_Q_A // N_KV_A

    def scores(qb, g, hpar):
        t, par = g // 2, g % 2
        ha, hb = rep * g + hpar, rep * g + hpar + 2
        k_al = (kt if par == hpar else kt_sw)[t][qb * L:(qb + 2) * L]
        qm = jnp.concatenate([q_masked(qb, ha), q_masked(qb, hb)], axis=0)
        first = first_step if qb == 0 else 0
        return _dot_nt(k_al, qm) + bias_ref[first, 2 * g + hpar]

    def weighted_values(qb, g, hpar, s, out_rows):
        t, par = g // 2, g % 2
        ha, hb = rep * g + hpar, rep * g + hpar + 2
        sink = jnp.where(lo_head, sinks_ref[ha], sinks_ref[hb]) * LOG2E
        m = jnp.maximum(jnp.max(s, axis=0, keepdims=True), sink)
        e = jnp.exp2(s - m).astype(BF16)
        ot = _dot(vt[t][:, qb * L:(qb + 2) * L], e)
        denom = ot[V7X_LANES:V7X_LANES + 1, :] + jnp.exp2(sink - m)
        og = ot[par * HEAD_DIM_A:(par + 1) * HEAD_DIM_A, :] * (1.0 / denom)
        out_rows[ha] = og[:, :L]
        out_rows[hb] = og[:, L:]

    tasks = [(qb, g, hpar) for qb in range(QB) for g in range(N_KV_A) for hpar in range(2)]
    ahead = {}
    out_rows = {}
    for t in range(len(tasks) + SWA_SCORE_LOOKAHEAD):
        if t < len(tasks):
            ahead[t] = scores(*tasks[t])
        d = t - SWA_SCORE_LOOKAHEAD
        if d >= 0:
            qb, g, hpar = tasks[d]
            rows = out_rows.setdefault(qb, [None] * N_Q_A)
            weighted_values(qb, g, hpar, ahead.pop(d), rows)
            if (g, hpar) == (N_KV_A - 1, 1):
                o_t = jnp.concatenate(out_rows.pop(qb), axis=0)
                o_ref[0, qb * L:(qb + 1) * L, :] = o_t.T.astype(o_ref.dtype)


def _swa(z3, sinks, bias, side):
    B, S, _ = z3.shape
    L = WINDOW
    QB = SWA_QBLOCKS_PER_STEP
    nb = S // (QB * L)
    kblk = COL_KA // WKV_A
    vblk = COL_VA // WKV_A
    prev = lambda n: jnp.maximum(n * QB - 1, 0)
    side_in, side_out, side_shapes = _side_cast_specs(side, B * nb, lambda b, n: b * nb + n)
    outs = pl.pallas_call(
        _swa_kernel,
        grid=(B, nb),
        in_specs=[
            pl.BlockSpec(memory_space=pltpu.SMEM),
            pl.BlockSpec((1, QB * L, WQ_A), lambda b, n: (b, n, 0)),
            pl.BlockSpec((1, QB * L, WKV_A), lambda b, n: (b, n, kblk)),
            pl.BlockSpec((1, L, WKV_A), lambda b, n: (b, prev(n), kblk)),
            pl.BlockSpec((1, QB * L, WKV_A), lambda b, n: (b, n, vblk)),
            pl.BlockSpec((1, L, WKV_A), lambda b, n: (b, prev(n), vblk)),
            pl.BlockSpec((2, N_Q_A // 2, 2 * L, 2 * L), lambda b, n: (0, 0, 0, 0)),
        ] + side_in,
        out_specs=[pl.BlockSpec((1, QB * L, WQ_A), lambda b, n: (b, n, 0))] + side_out,
        out_shape=[jax.ShapeDtypeStruct((B, S, WQ_A), BF16)] + side_shapes,
        compiler_params=_params(("arbitrary", "arbitrary")),
        name="swa",
    )(sinks, z3, z3, z3, z3, z3, bias, *side)
    return outs[0], outs[1:]


def _swa_bias_table():
    L = WINDOW
    rep = N_Q_A // N_KV_A
    slopes = np.exp2(-8.0 * np.arange(1, N_Q_A + 1, dtype=np.float32) / N_Q_A).astype(np.float32)
    kj = np.arange(2 * L)[:, None]
    qi = np.arange(L)[None, :]
    dist = L + qi - kj
    window = (dist >= 0) & (dist < WINDOW)
    table = np.empty((2, N_Q_A // 2, 2 * L, 2 * L), np.float32)
    for first in range(2):
        valid = window & ((kj >= L) if first else True)
        for g in range(N_KV_A):
            for hpar in range(2):
                for a in range(2):
                    h = rep * g + hpar + 2 * a
                    table[first, 2 * g + hpar, :, a * L:(a + 1) * L] = np.where(
                        valid, -(slopes[h] * LOG2E) * dist.astype(np.float32), -np.inf)
    return jnp.asarray(table)


def _moba_kernel(slopes_ref, q_ref, k_ref, v_ref, *refs):
    n_side = (len(refs) - 4) // 2
    o_ref = refs[n_side]
    vt_ref, gate_ref, tab_ref = refs[2 * n_side + 1:]
    _side_cast(refs[:n_side] + refs[n_side + 1:2 * n_side + 1])
    BLK = MOBA_BLOCK
    HP = MOBA_HEADS_PER_STEP
    dh = HEAD_DIM_B
    hp = pl.program_id(1)
    S = k_ref.shape[1]
    nblk = S // BLK

    def prepare_tables():
        kc = lax.broadcasted_iota(jnp.int32, (BLK, BLK), 0)
        qr = lax.broadcasted_iota(jnp.int32, (BLK, BLK), 1)
        rel = (qr - kc).astype(F32)
        for a in range(HP):
            slope2 = slopes_ref[hp * HP + a] * LOG2E
            tab_ref[a, 0] = -slope2 * rel
            tab_ref[a, 1] = jnp.where(rel >= 0.0, -slope2 * rel, NEG_INF)

    def prepare_values():
        eye = (lax.broadcasted_iota(jnp.int32, (dh, dh), 0)
               == lax.broadcasted_iota(jnp.int32, (dh, dh), 1)).astype(BF16)
        pad_row = lax.broadcasted_iota(jnp.int32, (BF16_SUBLANE_TILE, S), 0)
        for a in range(HP):
            vt_ref[a, 0:dh, :] = _dot_nt(eye, v_ref[0, :, a * dh:(a + 1) * dh]).astype(BF16)
            vt_ref[a, dh:, :] = jnp.where(pad_row == 0, 1.0, 0.0).astype(BF16)

    def block_means():
        blk = lax.broadcasted_iota(jnp.int32, (GATE_ROWS, S), 0)
        pos = lax.broadcasted_iota(jnp.int32, (GATE_ROWS, S), 1)
        member = jnp.where((pos >= blk * BLK) & (pos < (blk + 1) * BLK), 1.0 / BLK, 0.0).astype(BF16)
        return [_split_bf16(_dot(member, k_ref[0, :, a * dh:(a + 1) * dh])) for a in range(HP)]

    def prepare_gate(kmeans):
        for a, (km_hi, km_lo) in enumerate(kmeans):
            qn = q_ref[0, :, a * dh:(a + 1) * dh]
            gate_ref[a] = _dot_nt(km_hi, qn) + _dot_nt(km_lo, qn)

    def block_max_shifts(c, a):
        slope2 = slopes_ref[hp * HP + a] * LOG2E
        shifts = [-slope2 * float((c - n) * BLK) for n in range(c)]
        if c > MOBA_TOPK:
            gate = gate_ref[a, :, c * BLK:(c + 1) * BLK]
            blk = lax.broadcasted_iota(jnp.int32, (GATE_ROWS, 1), 0)
            past = blk < c
            for n in range(c):
                g_n = gate[n:n + 1, :]
                beats = ((gate > g_n) | ((gate == g_n) & (blk < n))) & past
                rank = jnp.sum(jnp.where(beats, 1.0, 0.0), axis=0, keepdims=True)
                shifts[n] = jnp.where(rank < float(MOBA_TOPK), shifts[n], NEG_INF)
        return shifts

    def scores(c, a, ss, ms):
        for j in range(c + 1):
            s = _dot_nt(k_ref[0, j * BLK:(j + 1) * BLK, a * dh:(a + 1) * dh],
                        q_ref[0, c * BLK:(c + 1) * BLK, a * dh:(a + 1) * dh])
            s = s + tab_ref[a, 1 if j == c else 0]
            ss.append(s)
            ms.append(jnp.max(s, axis=0, keepdims=True))
            yield

    def weighted_values(c, a, ss, ms):
        shifts = block_max_shifts(c, a)
        m = functools.reduce(jnp.maximum, [m_j + sh for m_j, sh in zip(ms, shifts)] + ms[c:])
        acc = None
        for j in range(c + 1):
            p = jnp.exp2(ss[j] - (m - shifts[j] if j < c else m)).astype(BF16)
            part = _dot(vt_ref[a, :, j * BLK:(j + 1) * BLK], p)
            acc = part if acc is None else acc + part
            if j < c:
                yield
        ot = acc[0:dh] * (1.0 / acc[dh:dh + 1])
        o_ref[0, c * BLK:(c + 1) * BLK, a * dh:(a + 1) * dh] = ot.T.astype(o_ref.dtype)
        yield

    prepare_tables()
    kmeans = block_means()
    tasks = [(c, a) for c in range(nblk) for a in range(HP)]
    ahead = {}
    for t in range(len(tasks) + MOBA_TASK_LOOKAHEAD):
        running = []
        if t < len(tasks):
            ahead[t] = ([], [])
            running.append(scores(*tasks[t], *ahead[t]))
        d = t - MOBA_TASK_LOOKAHEAD
        if d >= 0:
            c, a = tasks[d]
            if d == 0:
                prepare_values()
            if (c, a) == (MOBA_TOPK, 0):
                prepare_gate(kmeans)
            running.append(weighted_values(c, a, *ahead.pop(d)))
        while running:
            running = [g for g in running if next(g, StopIteration) is not StopIteration]


def _moba(z3, slopes, side):
    B, S, _ = z3.shape
    BLK = MOBA_BLOCK
    HP = MOBA_HEADS_PER_STEP
    w = HP * HEAD_DIM_B
    qc, kc, vc = COL_QB // w, COL_KB // w, COL_VB // w
    n_hp = N_H_B // HP
    side_in, side_out, side_shapes = _side_cast_specs(side, B * n_hp, lambda b, h: b * n_hp + h)
    outs = pl.pallas_call(
        _moba_kernel,
        grid=(B, n_hp),
        in_specs=[
            pl.BlockSpec(memory_space=pltpu.SMEM),
            pl.BlockSpec((1, S, w), lambda b, h: (b, 0, qc + h)),
            pl.BlockSpec((1, S, w), lambda b, h: (b, 0, kc + h)),
            pl.BlockSpec((1, S, w), lambda b, h: (b, 0, vc + h)),
        ] + side_in,
        out_specs=[pl.BlockSpec((1, S, w), lambda b, h: (b, 0, h))] + side_out,
        out_shape=[jax.ShapeDtypeStruct((B, S, W_B), BF16)] + side_shapes,
        scratch_shapes=[
            pltpu.VMEM((HP, HEAD_DIM_B + BF16_SUBLANE_TILE, S), BF16),
            pltpu.VMEM((HP, GATE_ROWS, S), F32),
            pltpu.VMEM((HP, 2, BLK, BLK), F32),
        ],
        compiler_params=_params(("arbitrary", "arbitrary")),
        name="moba",
    )(slopes, z3, z3, z3, *[s[0] if isinstance(s, tuple) else s for s in side])
    return outs[0], outs[1:]


def _sigmoid(x):
    return 1.0 / (1.0 + jnp.exp2(x * (-LOG2E)))


def _merge_out_kernel(oa_ref, ob_ref, ga_ref, gb_ref, ba_ref, bb_ref, wa_ref, wb_ref, wo_ref, x_ref,
                      g2_ref, o_ref, u2_ref, mixed_ref, *, chunk):
    oa = oa_ref[...]
    ob = ob_ref[...]
    for c0 in range(0, mixed_ref.shape[1], chunk):
        cs = slice(c0, c0 + chunk)
        a = _dot(oa, wa_ref[:, cs])
        b = _dot(ob, wb_ref[:, cs])
        ga = _sigmoid(ga_ref[:, cs].astype(F32) + ba_ref[:, cs])
        gb = _sigmoid(gb_ref[:, cs].astype(F32) + bb_ref[:, cs])
        mixed_ref[:, cs] = (ga * a + gb * b).astype(mixed_ref.dtype)
    h1 = x_ref[...] + _dot(mixed_ref[...], wo_ref[...])
    o_ref[...] = h1
    inv = lax.rsqrt(jnp.mean(h1 * h1, axis=-1, keepdims=True) + RMS_EPS)
    u2_ref[...] = (h1 * inv * g2_ref[...]).astype(u2_ref.dtype)


def _merge_out(oa, ob, gates, ba, bb, wa, wb, wo, x, g2, *, tm=512, chunk=512):
    m, ka = oa.shape
    kb = ob.shape[1]
    d = wa.shape[1]
    resident = pl.Buffered(1)
    return pl.pallas_call(
        functools.partial(_merge_out_kernel, chunk=chunk),
        grid=(m // tm,),
        in_specs=[
            pl.BlockSpec((tm, ka), lambda i: (i, 0)),
            pl.BlockSpec((tm, kb), lambda i: (i, 0)),
            pl.BlockSpec((tm, d), lambda i: (i, 0)),
            pl.BlockSpec((tm, d), lambda i: (i, 1)),
            pl.BlockSpec((1, d), lambda i: (0, 0)),
            pl.BlockSpec((1, d), lambda i: (0, 0)),
            pl.BlockSpec((ka, d), lambda i: (0, 0), pipeline_mode=resident),
            pl.BlockSpec((kb, d), lambda i: (0, 0), pipeline_mode=resident),
            pl.BlockSpec((d, d), lambda i: (0, 0), pipeline_mode=resident),
            pl.BlockSpec((tm, d), lambda i: (i, 0)),
            pl.BlockSpec((1, d), lambda i: (0, 0)),
        ],
        out_specs=[pl.BlockSpec((tm, d), lambda i: (i, 0)), pl.BlockSpec((tm, d), lambda i: (i, 0))],
        out_shape=[jax.ShapeDtypeStruct((m, d), F32), jax.ShapeDtypeStruct((m, d), BF16)],
        scratch_shapes=[pltpu.VMEM((tm, d), BF16)],
        compiler_params=_params(("parallel",)),
        name="merge_out",
    )(oa, ob, gates, gates, ba, bb, wa, wb, wo, x, g2)


def _mm_res_kernel(a_ref, w_ref, r_ref, o_ref):
    o_ref[...] = r_ref[...] + _dot(a_ref[...], w_ref[...])


def _mm_res(a, w, res, *, tm, tn, name):
    m, k = a.shape
    n = w.shape[1]
    return pl.pallas_call(
        _mm_res_kernel,
        grid=(m // tm, n // tn),
        in_specs=[
            pl.BlockSpec((tm, k), lambda i, j: (i, 0)),
            pl.BlockSpec((k, tn), lambda i, j: (0, j)),
            pl.BlockSpec((tm, tn), lambda i, j: (i, j)),
        ],
        out_specs=pl.BlockSpec((tm, tn), lambda i, j: (i, j)),
        out_shape=jax.ShapeDtypeStruct((m, n), F32),
        compiler_params=_params(("parallel", "parallel")),
        name=name,
    )(a, w, res)


def _ffn_up_kernel(u_ref, wg_ref, wu_ref, o_ref):
    u = u_ref[...]
    for c0 in range(0, o_ref.shape[1], V7X_MXU_DIM):
        cs = slice(c0, c0 + V7X_MXU_DIM)
        gate = _dot(u, wg_ref[:, cs])
        up = _dot(u, wu_ref[:, cs])
        o_ref[:, cs] = (gate * _sigmoid(gate) * up).astype(o_ref.dtype)


def _ffn_up(u, wg, wu, *, tm=2048, tn=512):
    m, d = u.shape
    n = wg.shape[1]
    return pl.pallas_call(
        _ffn_up_kernel,
        grid=(m // tm, n // tn),
        in_specs=[
            pl.BlockSpec((tm, d), lambda i, j: (i, 0)),
            pl.BlockSpec((d, tn), lambda i, j: (0, j)),
            pl.BlockSpec((d, tn), lambda i, j: (0, j)),
        ],
        out_specs=pl.BlockSpec((tm, tn), lambda i, j: (i, j)),
        out_shape=jax.ShapeDtypeStruct((m, n), BF16),
        compiler_params=_params(("parallel", "parallel")),
        name="ffn_up",
    )(u, wg, wu)


def kernel(x, norm1_g, w_in, b_gate, q_norm_a, k_norm_a, sinks_a, q_norm_b, k_norm_b,
           w_branch_a, w_branch_b, w_o, norm2_g, w_ffn_gate, w_ffn_up, w_ffn_down):
    B, S, D = x.shape
    depth = w_in.shape[0]
    assert D == D_MODEL and w_in.shape[2] == IN_COLS
    assert S % MOBA_BLOCK == 0 and S % WINDOW == 0
    M = B * S

    slopes_b = jnp.asarray(np.exp2(-8.0 * np.arange(1, N_H_B + 1, dtype=np.float32) / N_H_B), F32)
    swa_bias = _swa_bias_table()

    h = x.reshape(M, D)
    for l in range(depth):
        head_gain = jnp.concatenate([
            jnp.tile(q_norm_a[l], N_Q_A) * (HEAD_DIM_A ** -0.5 * LOG2E),
            jnp.tile(k_norm_a[l], N_KV_A),
            jnp.ones((WKV_A,), F32),
            jnp.tile(q_norm_b[l], N_H_B) * (HEAD_DIM_B ** -0.5 * LOG2E),
            jnp.tile(k_norm_b[l], N_H_B),
            jnp.ones((W_B,), F32),
        ]).reshape(1, COL_GA)
        qkv, u = _qkv_proj(h, norm1_g[l].reshape(1, D), w_in[l, :, :COL_GA].astype(BF16), head_gain, tm=512)
        z3 = qkv.reshape(B, S, COL_GA)
        o_a, (wg_b, wu_b) = _swa(z3, sinks_a[l], swa_bias, [w_ffn_gate[l], w_ffn_up[l]])
        o_b, (wd_b, wgate_b) = _moba(z3, slopes_b,
                                     [w_ffn_down[l], (w_in[l], COL_GA, 2 * D, SIDE_CAST_COL_TILE)])
        gates, (wo_b, wa_b, wb_b) = _gate_proj(u, wgate_b, [w_o[l], w_branch_a[l], w_branch_b[l]])
        h1, u2 = _merge_out(o_a.reshape(M, WQ_A), o_b.reshape(M, W_B), gates,
                            b_gate[l, :D].reshape(1, D), b_gate[l, D:].reshape(1, D),
                            wa_b, wb_b, wo_b, h, norm2_g[l].reshape(1, D))
        act = _ffn_up(u2, wg_b, wu_b)
        h = _mm_res(act, wd_b, h1, tm=1024, tn=512, name="ffn_down")
    return h.reshape(B, S, D)
```

```python
import functools

import numpy as np
import jax
import jax.numpy as jnp
from jax import lax
from jax.experimental import pallas as pl
from jax.experimental.pallas import tpu as pltpu

F32 = jnp.float32
BF16 = jnp.bfloat16

D_MODEL = 2048
HEAD_DIM_A = 64
N_Q_A = 16
N_KV_A = 4
WINDOW = 128
HEAD_DIM_B = 128
N_H_B = 8
MOBA_BLOCK = 256
MOBA_TOPK = 3
RMS_EPS = 1e-6

WQ_A = N_Q_A * HEAD_DIM_A
WKV_A = N_KV_A * HEAD_DIM_A
W_B = N_H_B * HEAD_DIM_B
COL_QA = 0
COL_KA = COL_QA + WQ_A
COL_VA = COL_KA + WKV_A
COL_QB = COL_VA + WKV_A
COL_KB = COL_QB + W_B
COL_VB = COL_KB + W_B
COL_GA = COL_VB + W_B
COL_GB = COL_GA + D_MODEL
IN_COLS = COL_GB + D_MODEL

V7X_LANES = 128
V7X_MXU_DIM = 256
VMEM_LIMIT_BYTES = 56 * 1024 * 1024
NORM_ROW_CHUNK = 64
NORM_UNROLL = 2
BF16_SUBLANE_TILE = 16
SIDE_CAST_COL_TILE = 512
SWA_QBLOCKS_PER_STEP = 8
SWA_SCORE_LOOKAHEAD = 4
MOBA_HEADS_PER_STEP = 2
MOBA_TASK_LOOKAHEAD = 4
GATE_ROWS = BF16_SUBLANE_TILE
NEG_INF = float("-inf")
LOG2E = 1.4426950408889634


def _params(sem):
    return pltpu.CompilerParams(dimension_semantics=sem, vmem_limit_bytes=VMEM_LIMIT_BYTES)


def _dot(a, b):
    return jnp.dot(a, b, preferred_element_type=F32)


def _dot_nt(a, b):
    return lax.dot_general(a, b, (((1,), (1,)), ((), ())), preferred_element_type=F32)


def _split_bf16(x):
    hi = x.astype(BF16)
    lo = (x - hi.astype(F32)).astype(BF16)
    return hi, lo


def _rms_rows_to_bf16(x_ref, g_ref, u_ref):
    rows = x_ref.shape[0]
    g = g_ref[...]

    def body(c, carry):
        r = pl.multiple_of(c * NORM_ROW_CHUNK, NORM_ROW_CHUNK)
        x = x_ref[pl.ds(r, NORM_ROW_CHUNK), :]
        ms = jnp.mean(x * x, axis=-1, keepdims=True)
        u_ref[pl.ds(r, NORM_ROW_CHUNK), :] = (x * lax.rsqrt(ms + RMS_EPS) * g).astype(BF16)
        return carry

    lax.fori_loop(0, rows // NORM_ROW_CHUNK, body, 0, unroll=NORM_UNROLL)


def _head_norm_kind(col):
    if col < COL_VA:
        return "norm64"
    if COL_QB <= col < COL_VB:
        return "norm128"
    return "copy"


def _qkv_proj_kernel(x_ref, g_ref, w_ref, hg_ref, o_ref, u_ref):
    _rms_rows_to_bf16(x_ref, g_ref, u_ref)
    CH = V7X_MXU_DIM

    def finish(z, cs, kind):
        if kind == "norm64":
            left = lax.broadcasted_iota(jnp.int32, (1, V7X_LANES), 1) < HEAD_DIM_A
            tiles = []
            for h0 in range(0, CH, V7X_LANES):
                zt = z[:, h0:h0 + V7X_LANES]
                sq = zt * zt
                ms_l = jnp.sum(jnp.where(left, sq, 0.0), axis=-1, keepdims=True)
                ms_r = jnp.sum(jnp.where(left, 0.0, sq), axis=-1, keepdims=True)
                ms = jnp.where(left, ms_l, ms_r) * (1.0 / HEAD_DIM_A)
                tiles.append(zt * lax.rsqrt(ms + RMS_EPS))
            z = jnp.concatenate(tiles, axis=1) * hg_ref[:, cs]
        elif kind == "norm128":
            halves = []
            for h0 in range(0, CH, HEAD_DIM_B):
                zh = z[:, h0:h0 + HEAD_DIM_B]
                ms = jnp.mean(zh * zh, axis=-1, keepdims=True)
                halves.append(zh * lax.rsqrt(ms + RMS_EPS))
            z = jnp.concatenate(halves, axis=1) * hg_ref[:, cs]
        o_ref[:, cs] = z.astype(o_ref.dtype)

    u = u_ref[...]
    chunks = [slice(c0, c0 + CH) for c0 in range(0, o_ref.shape[1], CH)]
    z_next = _dot(u, w_ref[:, chunks[0]])
    for idx, cs in enumerate(chunks):
        z = z_next
        if idx + 1 < len(chunks):
            z_next = _dot(u, w_ref[:, chunks[idx + 1]])
        finish(z, cs, _head_norm_kind(cs.start))


def _qkv_proj(x, g, w, head_gain, *, tm):
    m, d = x.shape
    n = w.shape[1]
    assert n % V7X_MXU_DIM == 0
    resident = pl.Buffered(1)
    return pl.pallas_call(
        _qkv_proj_kernel,
        grid=(m // tm,),
        in_specs=[
            pl.BlockSpec((tm, d), lambda i: (i, 0)),
            pl.BlockSpec((1, d), lambda i: (0, 0)),
            pl.BlockSpec((d, n), lambda i: (0, 0), pipeline_mode=resident),
            pl.BlockSpec((1, n), lambda i: (0, 0)),
        ],
        out_specs=[
            pl.BlockSpec((tm, n), lambda i: (i, 0)),
            pl.BlockSpec((tm, d), lambda i: (i, 0)),
        ],
        out_shape=[jax.ShapeDtypeStruct((m, n), BF16), jax.ShapeDtypeStruct((m, d), BF16)],
        compiler_params=_params(("parallel",)),
        name="qkv_proj",
    )(x, g, w, head_gain)


def _side_cast_specs(arrays, n_steps, step_index):
    in_specs, out_specs, shapes = [], [], []
    for item in arrays:
        w, col0, ncols, col_tile = item if isinstance(item, tuple) else (item, 0, item.shape[1], item.shape[1])
        rows = w.shape[0]
        n_col = ncols // col_tile
        n_row = n_steps // n_col
        chunk = rows // n_row
        assert n_col * col_tile == ncols and n_row * n_col == n_steps and col0 % col_tile == 0, (w.shape, n_steps)
        assert chunk * n_row == rows and chunk % BF16_SUBLANE_TILE == 0, (w.shape, n_steps)

        def in_map(*g, n_col=n_col, c0=col0 // col_tile):
            s = step_index(*g)
            return (s // n_col, c0 + s % n_col)

        def out_map(*g, n_col=n_col):
            s = step_index(*g)
            return (s // n_col, s % n_col)

        in_specs.append(pl.BlockSpec((chunk, col_tile), in_map))
        out_specs.append(pl.BlockSpec((chunk, col_tile), out_map))
        shapes.append(jax.ShapeDtypeStruct((rows, ncols), BF16))
    return in_specs, out_specs, shapes


def _side_cast(refs):
    n = len(refs) // 2
    for src, dst in zip(refs[:n], refs[n:]):
        dst[...] = src[...].astype(dst.dtype)


def _gate_proj_kernel(u_ref, w_ref, *refs):
    n_side = (len(refs) - 1) // 2
    o_ref = refs[n_side]
    o_ref[...] = _dot(u_ref[...], w_ref[...]).astype(o_ref.dtype)
    _side_cast(refs[:n_side] + refs[n_side + 1:])


def _gate_proj(u, w, side, *, tm=2048, tn=1024):
    m, k = u.shape
    n = w.shape[1]
    gi, gj = m // tm, n // tn
    side_in, side_out, side_shapes = _side_cast_specs(side, gi * gj, lambda i, j: i * gj + j)
    outs = pl.pallas_call(
        _gate_proj_kernel,
        grid=(gi, gj),
        in_specs=[
            pl.BlockSpec((tm, k), lambda i, j: (i, 0)),
            pl.BlockSpec((k, tn), lambda i, j: (0, j)),
        ] + side_in,
        out_specs=[pl.BlockSpec((tm, tn), lambda i, j: (i, j))] + side_out,
        out_shape=[jax.ShapeDtypeStruct((m, n), BF16)] + side_shapes,
        compiler_params=_params(("arbitrary", "arbitrary")),
        name="gate_proj",
    )(u, w, *side)
    return outs[0], outs[1:]


def _swa_kernel(sinks_ref, q_ref, kc_ref, kp_ref, vc_ref, vp_ref, bias_ref, *refs):
    n_side = (len(refs) - 1) // 2
    o_ref = refs[n_side]
    _side_cast(refs[:n_side] + refs[n_side + 1:])
    L = WINDOW
    QB = SWA_QBLOCKS_PER_STEP
    first_step = (pl.program_id(1) == 0).astype(jnp.int32)
    k = jnp.concatenate([kp_ref[0], kc_ref[0]], axis=0).astype(F32)
    v = jnp.concatenate([vp_ref[0], vc_ref[0]], axis=0).astype(F32)

    lane = lax.broadcasted_iota(jnp.int32, (1, V7X_LANES), 1)
    left = lane < HEAD_DIM_A
    lo_head = lax.broadcasted_iota(jnp.int32, (1, 2 * L), 1) < L

    n_kt = WKV_A // V7X_LANES
    lane_tile = lambda t: slice(t * V7X_LANES, (t + 1) * V7X_LANES)
    kt = [k[:, lane_tile(t)] for t in range(n_kt)]
    kt_sw = [pltpu.roll(x, HEAD_DIM_A, axis=1).astype(BF16) for x in kt]
    kt = [x.astype(BF16) for x in kt]
    pad_row = lax.broadcasted_iota(jnp.int32, (BF16_SUBLANE_TILE, k.shape[0]), 0)
    ones_row = jnp.where(pad_row == 0, 1.0, 0.0)
    vt = [jnp.concatenate([v[:, lane_tile(t)].T, ones_row], axis=0).astype(BF16) for t in range(n_kt)]

    def q_masked(qb, h):
        qt = q_ref[0, qb * L:(qb + 1) * L, lane_tile(h // 2)]
        keep = left if h % 2 == 0 else jnp.logical_not(left)
        return jnp.where(keep, qt, jnp.zeros_like(qt))

    rep = N_Q_A // N_KV_A

    def scores(qb, g, hpar):
        t, par = g // 2, g % 2
        ha, hb = rep * g + hpar, rep * g + hpar + 2
        k_al = (kt if par == hpar else kt_sw)[t][qb * L:(qb + 2) * L]
        qm = jnp.concatenate([q_masked(qb, ha), q_masked(qb, hb)], axis=0)
        first = first_step if qb == 0 else 0
        return _dot_nt(k_al, qm) + bias_ref[first, 2 * g + hpar]

    def weighted_values(qb, g, hpar, s, out_rows):
        t, par = g // 2, g % 2
        ha, hb = rep * g + hpar, rep * g + hpar + 2
        sink = jnp.where(lo_head, sinks_ref[ha], sinks_ref[hb]) * LOG2E
        m = jnp.maximum(jnp.max(s, axis=0, keepdims=True), sink)
        e = jnp.exp2(s - m).astype(BF16)
        ot = _dot(vt[t][:, qb * L:(qb + 2) * L], e)
        denom = ot[V7X_LANES:V7X_LANES + 1, :] + jnp.exp2(sink - m)
        og = ot[par * HEAD_DIM_A:(par + 1) * HEAD_DIM_A, :] * (1.0 / denom)
        out_rows[ha] = og[:, :L]
        out_rows[hb] = og[:, L:]

    tasks = [(qb, g, hpar) for qb in range(QB) for g in range(N_KV_A) for hpar in range(2)]
    ahead = {}
    out_rows = {}
    for t in range(len(tasks) + SWA_SCORE_LOOKAHEAD):
        if t < len(tasks):
            ahead[t] = scores(*tasks[t])
        d = t - SWA_SCORE_LOOKAHEAD
        if d >= 0:
            qb, g, hpar = tasks[d]
            rows = out_rows.setdefault(qb, [None] * N_Q_A)
            weighted_values(qb, g, hpar, ahead.pop(d), rows)
            if (g, hpar) == (N_KV_A - 1, 1):
                o_t = jnp.concatenate(out_rows.pop(qb), axis=0)
                o_ref[0, qb * L:(qb + 1) * L, :] = o_t.T.astype(o_ref.dtype)


def _swa(z3, sinks, bias, side):
    B, S, _ = z3.shape
    L = WINDOW
    QB = SWA_QBLOCKS_PER_STEP
    nb = S // (QB * L)
    kblk = COL_KA // WKV_A
    vblk = COL_VA // WKV_A
    prev = lambda n: jnp.maximum(n * QB - 1, 0)
    side_in, side_out, side_shapes = _side_cast_specs(side, B * nb, lambda b, n: b * nb + n)
    outs = pl.pallas_call(
        _swa_kernel,
        grid=(B, nb),
        in_specs=[
            pl.BlockSpec(memory_space=pltpu.SMEM),
            pl.BlockSpec((1, QB * L, WQ_A), lambda b, n: (b, n, 0)),
            pl.BlockSpec((1, QB * L, WKV_A), lambda b, n: (b, n, kblk)),
            pl.BlockSpec((1, L, WKV_A), lambda b, n: (b, prev(n), kblk)),
            pl.BlockSpec((1, QB * L, WKV_A), lambda b, n: (b, n, vblk)),
            pl.BlockSpec((1, L, WKV_A), lambda b, n: (b, prev(n), vblk)),
            pl.BlockSpec((2, N_Q_A // 2, 2 * L, 2 * L), lambda b, n: (0, 0, 0, 0)),
        ] + side_in,
        out_specs=[pl.BlockSpec((1, QB * L, WQ_A), lambda b, n: (b, n, 0))] + side_out,
        out_shape=[jax.ShapeDtypeStruct((B, S, WQ_A), BF16)] + side_shapes,
        compiler_params=_params(("arbitrary", "arbitrary")),
        name="swa",
    )(sinks, z3, z3, z3, z3, z3, bias, *side)
    return outs[0], outs[1:]


def _swa_bias_table():
    L = WINDOW
    rep = N_Q_A // N_KV_A
    slopes = np.exp2(-8.0 * np.arange(1, N_Q_A + 1, dtype=np.float32) / N_Q_A).astype(np.float32)
    kj = np.arange(2 * L)[:, None]
    qi = np.arange(L)[None, :]
    dist = L + qi - kj
    window = (dist >= 0) & (dist < WINDOW)
    table = np.empty((2, N_Q_A // 2, 2 * L, 2 * L), np.float32)
    for first in range(2):
        valid = window & ((kj >= L) if first else True)
        for g in range(N_KV_A):
            for hpar in range(2):
                for a in range(2):
                    h = rep * g + hpar + 2 * a
                    table[first, 2 * g + hpar, :, a * L:(a + 1) * L] = np.where(
                        valid, -(slopes[h] * LOG2E) * dist.astype(np.float32), -np.inf)
    return jnp.asarray(table)


def _moba_kernel(slopes_ref, q_ref, k_ref, v_ref, *refs):
    n_side = (len(refs) - 6) // 2
    o_ref = refs[n_side]
    vt_ref, gate_ref, tab_ref, kaug_ref, qaug_ref = refs[2 * n_side + 1:]
    _side_cast(refs[:n_side] + refs[n_side + 1:2 * n_side + 1])
    BLK = MOBA_BLOCK
    HP = MOBA_HEADS_PER_STEP
    dh = HEAD_DIM_B
    hp = pl.program_id(1)
    S = k_ref.shape[1]
    nblk = S // BLK

    def prepare_tables():
        kc = lax.broadcasted_iota(jnp.int32, (BLK, BLK), 0)
        qr = lax.broadcasted_iota(jnp.int32, (BLK, BLK), 1)
        tab_ref[...] = jnp.where(qr >= kc, 0.0, NEG_INF)
        lane = lax.broadcasted_iota(jnp.int32, (1, V7X_LANES), 1)
        key_pos = lax.broadcasted_iota(jnp.int32, (BLK, V7X_LANES), 0).astype(F32)
        ones_cols = jnp.where(lane < 3, 1.0, 0.0).astype(BF16)
        for a in range(HP):
            bias = (slopes_ref[hp * HP + a] * LOG2E) * key_pos
            p1 = bias.astype(BF16).astype(F32)
            p2 = (bias - p1).astype(BF16).astype(F32)
            p3 = bias - p1 - p2
            cols = jnp.where(lane == 0, p1, jnp.where(lane == 1, p2, jnp.where(lane == 2, p3, 0.0))).astype(BF16)
            kaug_ref[a, :, 0:dh] = k_ref[0, :, a * dh:(a + 1) * dh]
            qaug_ref[a, :, 0:dh] = q_ref[0, :, a * dh:(a + 1) * dh]
            qaug_ref[a, :, dh:] = jnp.broadcast_to(ones_cols, (S, V7X_LANES))
            for j in range(nblk):
                kaug_ref[a, j * BLK:(j + 1) * BLK, dh:] = cols

    def prepare_values():
        eye = (lax.broadcasted_iota(jnp.int32, (dh, dh), 0)
               == lax.broadcasted_iota(jnp.int32, (dh, dh), 1)).astype(BF16)
        pad_row = lax.broadcasted_iota(jnp.int32, (BF16_SUBLANE_TILE, S), 0)
        for a in range(HP):
            vt_ref[a, 0:dh, :] = _dot_nt(eye, v_ref[0, :, a * dh:(a + 1) * dh]).astype(BF16)
            vt_ref[a, dh:, :] = jnp.where(pad_row == 0, 1.0, 0.0).astype(BF16)

    def block_means():
        blk = lax.broadcasted_iota(jnp.int32, (GATE_ROWS, S), 0)
        pos = lax.broadcasted_iota(jnp.int32, (GATE_ROWS, S), 1)
        member = jnp.where((pos >= blk * BLK) & (pos < (blk + 1) * BLK), 1.0 / BLK, 0.0).astype(BF16)
        return [_split_bf16(_dot(member, k_ref[0, :, a * dh:(a + 1) * dh])) for a in range(HP)]

    def prepare_gate(kmeans):
        for a, (km_hi, km_lo) in enumerate(kmeans):
            qn = q_ref[0, :, a * dh:(a + 1) * dh]
            gate_ref[a] = _dot_nt(km_hi, qn) + _dot_nt(km_lo, qn)

    def block_max_shifts(c, a):
        slope2 = slopes_ref[hp * HP + a] * LOG2E
        shifts = [-slope2 * float((c - n) * BLK) for n in range(c)]
        if c > MOBA_TOPK:
            gate = gate_ref[a, :, c * BLK:(c + 1) * BLK]
            blk = lax.broadcasted_iota(jnp.int32, (GATE_ROWS, 1), 0)
            past = blk < c
            for n in range(c):
                g_n = gate[n:n + 1, :]
                beats = ((gate > g_n) | ((gate == g_n) & (blk < n))) & past
                rank = jnp.sum(jnp.where(beats, 1.0, 0.0), axis=0, keepdims=True)
                shifts[n] = jnp.where(rank < float(MOBA_TOPK), shifts[n], NEG_INF)
        return shifts

    def scores(c, a, ss, ms):
        for j in range(c + 1):
            s = _dot_nt(kaug_ref[a, j * BLK:(j + 1) * BLK, :], qaug_ref[a, c * BLK:(c + 1) * BLK, :])
            if j == c:
                s = s + tab_ref[...]
            ss.append(s)
            ms.append(jnp.max(s, axis=0, keepdims=True))
            yield

    def weighted_values(c, a, ss, ms):
        shifts = block_max_shifts(c, a)
        m = functools.reduce(jnp.maximum, [m_j + sh for m_j, sh in zip(ms, shifts)] + ms[c:])
        acc = None
        for j in range(c + 1):
            p = jnp.exp2(ss[j] - (m - shifts[j] if j < c else m)).astype(BF16)
            part = _dot(vt_ref[a, :, j * BLK:(j + 1) * BLK], p)
            acc = part if acc is None else acc + part
            if j < c:
                yield
        ot = acc[0:dh] * (1.0 / acc[dh:dh + 1])
        o_ref[0, c * BLK:(c + 1) * BLK, a * dh:(a + 1) * dh] = ot.T.astype(o_ref.dtype)
        yield

    prepare_tables()
    kmeans = block_means()
    tasks = [(c, a) for c in range(nblk) for a in range(HP)]
    ahead = {}
    for t in range(len(tasks) + MOBA_TASK_LOOKAHEAD):
        running = []
        if t < len(tasks):
            ahead[t] = ([], [])
            running.append(scores(*tasks[t], *ahead[t]))
        d = t - MOBA_TASK_LOOKAHEAD
        if d >= 0:
            c, a = tasks[d]
            if d == 0:
                prepare_values()
            if (c, a) == (MOBA_TOPK, 0):
                prepare_gate(kmeans)
            running.append(weighted_values(c, a, *ahead.pop(d)))
        while running:
            running = [g for g in running if next(g, StopIteration) is not StopIteration]


def _moba(z3, slopes, side):
    B, S, _ = z3.shape
    BLK = MOBA_BLOCK
    HP = MOBA_HEADS_PER_STEP
    w = HP * HEAD_DIM_B
    qc, kc, vc = COL_QB // w, COL_KB // w, COL_VB // w
    n_hp = N_H_B // HP
    side_in, side_out, side_shapes = _side_cast_specs(side, B * n_hp, lambda b, h: b * n_hp + h)
    outs = pl.pallas_call(
        _moba_kernel,
        grid=(B, n_hp),
        in_specs=[
            pl.BlockSpec(memory_space=pltpu.SMEM),
            pl.BlockSpec((1, S, w), lambda b, h: (b, 0, qc + h)),
            pl.BlockSpec((1, S, w), lambda b, h: (b, 0, kc + h)),
            pl.BlockSpec((1, S, w), lambda b, h: (b, 0, vc + h)),
        ] + side_in,
        out_specs=[pl.BlockSpec((1, S, w), lambda b, h: (b, 0, h))] + side_out,
        out_shape=[jax.ShapeDtypeStruct((B, S, W_B), BF16)] + side_shapes,
        scratch_shapes=[
            pltpu.VMEM((HP, HEAD_DIM_B + BF16_SUBLANE_TILE, S), BF16),
            pltpu.VMEM((HP, GATE_ROWS, S), F32),
            pltpu.VMEM((BLK, BLK), F32),
            pltpu.VMEM((HP, S, 2 * HEAD_DIM_B), BF16),
            pltpu.VMEM((HP, S, 2 * HEAD_DIM_B), BF16),
        ],
        compiler_params=_params(("arbitrary", "arbitrary")),
        name="moba",
    )(slopes, z3, z3, z3, *[s[0] if isinstance(s, tuple) else s for s in side])
    return outs[0], outs[1:]


def _sigmoid(x):
    return 1.0 / (1.0 + jnp.exp2(x * (-LOG2E)))


def _merge_out_kernel(oa_ref, ob_ref, ga_ref, gb_ref, ba_ref, bb_ref, wa_ref, wb_ref, wo_ref, x_ref,
                      g2_ref, o_ref, u2_ref, mixed_ref, *, chunk):
    oa = oa_ref[...]
    ob = ob_ref[...]
    for c0 in range(0, mixed_ref.shape[1], chunk):
        cs = slice(c0, c0 + chunk)
        a = _dot(oa, wa_ref[:, cs])
        b = _dot(ob, wb_ref[:, cs])
        ga = _sigmoid(ga_ref[:, cs].astype(F32) + ba_ref[:, cs])
        gb = _sigmoid(gb_ref[:, cs].astype(F32) + bb_ref[:, cs])
        mixed_ref[:, cs] = (ga * a + gb * b).astype(mixed_ref.dtype)
    h1 = x_ref[...] + _dot(mixed_ref[...], wo_ref[...])
    o_ref[...] = h1
    inv = lax.rsqrt(jnp.mean(h1 * h1, axis=-1, keepdims=True) + RMS_EPS)
    u2_ref[...] = (h1 * inv * g2_ref[...]).astype(u2_ref.dtype)


def _merge_out(oa, ob, gates, ba, bb, wa, wb, wo, x, g2, *, tm=512, chunk=512):
    m, ka = oa.shape
    kb = ob.shape[1]
    d = wa.shape[1]
    resident = pl.Buffered(1)
    return pl.pallas_call(
        functools.partial(_merge_out_kernel, chunk=chunk),
        grid=(m // tm,),
        in_specs=[
            pl.BlockSpec((tm, ka), lambda i: (i, 0)),
            pl.BlockSpec((tm, kb), lambda i: (i, 0)),
            pl.BlockSpec((tm, d), lambda i: (i, 0)),
            pl.BlockSpec((tm, d), lambda i: (i, 1)),
            pl.BlockSpec((1, d), lambda i: (0, 0)),
            pl.BlockSpec((1, d), lambda i: (0, 0)),
            pl.BlockSpec((ka, d), lambda i: (0, 0), pipeline_mode=resident),
            pl.BlockSpec((kb, d), lambda i: (0, 0), pipeline_mode=resident),
            pl.BlockSpec((d, d), lambda i: (0, 0), pipeline_mode=resident),
            pl.BlockSpec((tm, d), lambda i: (i, 0)),
            pl.BlockSpec((1, d), lambda i: (0, 0)),
        ],
        out_specs=[pl.BlockSpec((tm, d), lambda i: (i, 0)), pl.BlockSpec((tm, d), lambda i: (i, 0))],
        out_shape=[jax.ShapeDtypeStruct((m, d), F32), jax.ShapeDtypeStruct((m, d), BF16)],
        scratch_shapes=[pltpu.VMEM((tm, d), BF16)],
        compiler_params=_params(("parallel",)),
        name="merge_out",
    )(oa, ob, gates, gates, ba, bb, wa, wb, wo, x, g2)


def _mm_res_kernel(a_ref, w_ref, r_ref, o_ref):
    o_ref[...] = r_ref[...] + _dot(a_ref[...], w_ref[...])


def _mm_res(a, w, res, *, tm, tn, name):
    m, k = a.shape
    n = w.shape[1]
    return pl.pallas_call(
        _mm_res_kernel,
        grid=(m // tm, n // tn),
        in_specs=[
            pl.BlockSpec((tm, k), lambda i, j: (i, 0)),
            pl.BlockSpec((k, tn), lambda i, j: (0, j)),
            pl.BlockSpec((tm, tn), lambda i, j: (i, j)),
        ],
        out_specs=pl.BlockSpec((tm, tn), lambda i, j: (i, j)),
        out_shape=jax.ShapeDtypeStruct((m, n), F32),
        compiler_params=_params(("parallel", "parallel")),
        name=name,
    )(a, w, res)


def _ffn_up_kernel(u_ref, wg_ref, wu_ref, o_ref):
    u = u_ref[...]
    for c0 in range(0, o_ref.shape[1], V7X_MXU_DIM):
        cs = slice(c0, c0 + V7X_MXU_DIM)
        gate = _dot(u, wg_ref[:, cs])
        up = _dot(u, wu_ref[:, cs])
        o_ref[:, cs] = (gate * _sigmoid(gate) * up).astype(o_ref.dtype)


def _ffn_up(u, wg, wu, *, tm=2048, tn=512):
    m, d = u.shape
    n = wg.shape[1]
    return pl.pallas_call(
        _ffn_up_kernel,
        grid=(m // tm, n // tn),
        in_specs=[
            pl.BlockSpec((tm, d), lambda i, j: (i, 0)),
            pl.BlockSpec((d, tn), lambda i, j: (0, j)),
            pl.BlockSpec((d, tn), lambda i, j: (0, j)),
        ],
        out_specs=pl.BlockSpec((tm, tn), lambda i, j: (i, j)),
        out_shape=jax.ShapeDtypeStruct((m, n), BF16),
        compiler_params=_params(("parallel", "parallel")),
        name="ffn_up",
    )(u, wg, wu)


def kernel(x, norm1_g, w_in, b_gate, q_norm_a, k_norm_a, sinks_a, q_norm_b, k_norm_b,
           w_branch_a, w_branch_b, w_o, norm2_g, w_ffn_gate, w_ffn_up, w_ffn_down):
    B, S, D = x.shape
    depth = w_in.shape[0]
    assert D == D_MODEL and w_in.shape[2] == IN_COLS
    assert S % MOBA_BLOCK == 0 and S % WINDOW == 0
    M = B * S

    slopes_b = jnp.asarray(np.exp2(-8.0 * np.arange(1, N_H_B + 1, dtype=np.float32) / N_H_B), F32)
    swa_bias = _swa_bias_table()

    h = x.reshape(M, D)
    for l in range(depth):
        head_gain = jnp.concatenate([
            jnp.tile(q_norm_a[l], N_Q_A) * (HEAD_DIM_A ** -0.5 * LOG2E),
            jnp.tile(k_norm_a[l], N_KV_A),
            jnp.ones((WKV_A,), F32),
            jnp.tile(q_norm_b[l], N_H_B) * (HEAD_DIM_B ** -0.5 * LOG2E),
            jnp.tile(k_norm_b[l], N_H_B),
            jnp.ones((W_B,), F32),
        ]).reshape(1, COL_GA)
        qkv, u = _qkv_proj(h, norm1_g[l].reshape(1, D), w_in[l, :, :COL_GA].astype(BF16), head_gain, tm=512)
        z3 = qkv.reshape(B, S, COL_GA)
        o_a, (wg_b, wu_b) = _swa(z3, sinks_a[l], swa_bias, [w_ffn_gate[l], w_ffn_up[l]])
        o_b, (wd_b, wgate_b) = _moba(z3, slopes_b,
                                     [w_ffn_down[l], (w_in[l], COL_GA, 2 * D, SIDE_CAST_COL_TILE)])
        gates, (wo_b, wa_b, wb_b) = _gate_proj(u, wgate_b, [w_o[l], w_branch_a[l], w_branch_b[l]])
        h1, u2 = _merge_out(o_a.reshape(M, WQ_A), o_b.reshape(M, W_B), gates,
                            b_gate[l, :D].reshape(1, D), b_gate[l, D:].reshape(1, D),
                            wa_b, wb_b, wo_b, h, norm2_g[l].reshape(1, D))
        act = _ffn_up(u2, wg_b, wu_b)
        h = _mm_res(act, wd_b, h1, tm=1024, tn=512, name="ffn_down")
    return h.reshape(B, S, D)
```

```python
import functools

import numpy as np
import jax
import jax.numpy as jnp
from jax import lax
from jax.experimental import pallas as pl
from jax.experimental.pallas import tpu as pltpu

F32 = jnp.float32
BF16 = jnp.bfloat16

D_MODEL = 2048
HEAD_DIM_A = 64
N_Q_A = 16
N_KV_A = 4
WINDOW = 128
HEAD_DIM_B = 128
N_H_B = 8
MOBA_BLOCK = 256
MOBA_TOPK = 3
RMS_EPS = 1e-6

WQ_A = N_Q_A * HEAD_DIM_A
WKV_A = N_KV_A * HEAD_DIM_A
W_B = N_H_B * HEAD_DIM_B
COL_QA = 0
COL_KA = COL_QA + WQ_A
COL_VA = COL_KA + WKV_A
COL_QB = COL_VA + WKV_A
COL_KB = COL_QB + W_B
COL_VB = COL_KB + W_B
COL_GA = COL_VB + W_B
COL_GB = COL_GA + D_MODEL
IN_COLS = COL_GB + D_MODEL

V7X_LANES = 128
V7X_MXU_DIM = 256
VMEM_LIMIT_BYTES = 56 * 1024 * 1024
NORM_ROW_CHUNK = 64
NORM_UNROLL = 4
BF16_SUBLANE_TILE = 16
SIDE_CAST_COL_TILE = 512
SWA_QBLOCKS_PER_STEP = 8
SWA_SCORE_LOOKAHEAD = 4
MOBA_HEADS_PER_STEP = 2
MOBA_TASK_LOOKAHEAD = 4
GATE_ROWS = BF16_SUBLANE_TILE
NEG_INF = float("-inf")
LOG2E = 1.4426950408889634


def _params(sem):
    return pltpu.CompilerParams(dimension_semantics=sem, vmem_limit_bytes=VMEM_LIMIT_BYTES)


def _dot(a, b):
    return jnp.dot(a, b, preferred_element_type=F32)


def _dot_nt(a, b):
    return lax.dot_general(a, b, (((1,), (1,)), ((), ())), preferred_element_type=F32)


def _split_bf16(x):
    hi = x.astype(BF16)
    lo = (x - hi.astype(F32)).astype(BF16)
    return hi, lo


def _rms_rows_to_bf16(x_ref, g_ref, u_ref):
    rows = x_ref.shape[0]
    g = g_ref[...]

    def body(c, carry):
        r = pl.multiple_of(c * NORM_ROW_CHUNK, NORM_ROW_CHUNK)
        x = x_ref[pl.ds(r, NORM_ROW_CHUNK), :]
        ms = jnp.mean(x * x, axis=-1, keepdims=True)
        u_ref[pl.ds(r, NORM_ROW_CHUNK), :] = (x * lax.rsqrt(ms + RMS_EPS) * g).astype(BF16)
        return carry

    lax.fori_loop(0, rows // NORM_ROW_CHUNK, body, 0, unroll=NORM_UNROLL)


def _head_norm_kind(col):
    if col < COL_VA:
        return "norm64"
    if COL_QB <= col < COL_VB:
        return "norm128"
    return "copy"


def _qkv_proj_kernel(x_ref, g_ref, w_ref, hg_ref, o_ref, u_ref):
    _rms_rows_to_bf16(x_ref, g_ref, u_ref)
    CH = V7X_MXU_DIM

    def finish(z, cs, kind):
        if kind == "norm64":
            left = lax.broadcasted_iota(jnp.int32, (1, V7X_LANES), 1) < HEAD_DIM_A
            tiles = []
            for h0 in range(0, CH, V7X_LANES):
                zt = z[:, h0:h0 + V7X_LANES]
                sq = zt * zt
                ms_l = jnp.sum(jnp.where(left, sq, 0.0), axis=-1, keepdims=True)
                ms_r = jnp.sum(jnp.where(left, 0.0, sq), axis=-1, keepdims=True)
                ms = jnp.where(left, ms_l, ms_r) * (1.0 / HEAD_DIM_A)
                tiles.append(zt * lax.rsqrt(ms + RMS_EPS))
            z = jnp.concatenate(tiles, axis=1) * hg_ref[:, cs]
        elif kind == "norm128":
            halves = []
            for h0 in range(0, CH, HEAD_DIM_B):
                zh = z[:, h0:h0 + HEAD_DIM_B]
                ms = jnp.mean(zh * zh, axis=-1, keepdims=True)
                halves.append(zh * lax.rsqrt(ms + RMS_EPS))
            z = jnp.concatenate(halves, axis=1) * hg_ref[:, cs]
        o_ref[:, cs] = z.astype(o_ref.dtype)

    u = u_ref[...]
    chunks = [slice(c0, c0 + CH) for c0 in range(0, o_ref.shape[1], CH)]
    z_next = _dot(u, w_ref[:, chunks[0]])
    for idx, cs in enumerate(chunks):
        z = z_next
        if idx + 1 < len(chunks):
            z_next = _dot(u, w_ref[:, chunks[idx + 1]])
        finish(z, cs, _head_norm_kind(cs.start))


def _qkv_proj(x, g, w, head_gain, *, tm):
    m, d = x.shape
    n = w.shape[1]
    assert n % V7X_MXU_DIM == 0
    resident = pl.Buffered(1)
    return pl.pallas_call(
        _qkv_proj_kernel,
        grid=(m // tm,),
        in_specs=[
            pl.BlockSpec((tm, d), lambda i: (i, 0)),
            pl.BlockSpec((1, d), lambda i: (0, 0)),
            pl.BlockSpec((d, n), lambda i: (0, 0), pipeline_mode=resident),
            pl.BlockSpec((1, n), lambda i: (0, 0)),
        ],
        out_specs=[
            pl.BlockSpec((tm, n), lambda i: (i, 0)),
            pl.BlockSpec((tm, d), lambda i: (i, 0)),
        ],
        out_shape=[jax.ShapeDtypeStruct((m, n), BF16), jax.ShapeDtypeStruct((m, d), BF16)],
        compiler_params=_params(("parallel",)),
        name="qkv_proj",
    )(x, g, w, head_gain)


def _side_cast_specs(arrays, n_steps, step_index):
    in_specs, out_specs, shapes = [], [], []
    for item in arrays:
        w, col0, ncols, col_tile = item if isinstance(item, tuple) else (item, 0, item.shape[1], item.shape[1])
        rows = w.shape[0]
        n_col = ncols // col_tile
        n_row = n_steps // n_col
        chunk = rows // n_row
        assert n_col * col_tile == ncols and n_row * n_col == n_steps and col0 % col_tile == 0, (w.shape, n_steps)
        assert chunk * n_row == rows and chunk % BF16_SUBLANE_TILE == 0, (w.shape, n_steps)

        def in_map(*g, n_col=n_col, c0=col0 // col_tile):
            s = step_index(*g)
            return (s // n_col, c0 + s % n_col)

        def out_map(*g, n_col=n_col):
            s = step_index(*g)
            return (s // n_col, s % n_col)

        in_specs.append(pl.BlockSpec((chunk, col_tile), in_map))
        out_specs.append(pl.BlockSpec((chunk, col_tile), out_map))
        shapes.append(jax.ShapeDtypeStruct((rows, ncols), BF16))
    return in_specs, out_specs, shapes


def _side_cast(refs):
    n = len(refs) // 2
    for src, dst in zip(refs[:n], refs[n:]):
        dst[...] = src[...].astype(dst.dtype)


def _gate_proj_kernel(u_ref, w_ref, *refs):
    n_side = (len(refs) - 1) // 2
    o_ref = refs[n_side]
    o_ref[...] = _dot(u_ref[...], w_ref[...]).astype(o_ref.dtype)
    _side_cast(refs[:n_side] + refs[n_side + 1:])


def _gate_proj(u, w, side, *, tm=2048, tn=1024):
    m, k = u.shape
    n = w.shape[1]
    gi, gj = m // tm, n // tn
    side_in, side_out, side_shapes = _side_cast_specs(side, gi * gj, lambda i, j: i * gj + j)
    outs = pl.pallas_call(
        _gate_proj_kernel,
        grid=(gi, gj),
        in_specs=[
            pl.BlockSpec((tm, k), lambda i, j: (i, 0)),
            pl.BlockSpec((k, tn), lambda i, j: (0, j)),
        ] + side_in,
        out_specs=[pl.BlockSpec((tm, tn), lambda i, j: (i, j))] + side_out,
        out_shape=[jax.ShapeDtypeStruct((m, n), BF16)] + side_shapes,
        compiler_params=_params(("arbitrary", "arbitrary")),
        name="gate_proj",
    )(u, w, *side)
    return outs[0], outs[1:]


def _swa_kernel(sinks_ref, q_ref, kc_ref, kp_ref, vc_ref, vp_ref, bias_ref, *refs):
    n_side = (len(refs) - 1) // 2
    o_ref = refs[n_side]
    _side_cast(refs[:n_side] + refs[n_side + 1:])
    L = WINDOW
    QB = SWA_QBLOCKS_PER_STEP
    first_step = (pl.program_id(1) == 0).astype(jnp.int32)
    k = jnp.concatenate([kp_ref[0], kc_ref[0]], axis=0).astype(F32)
    v = jnp.concatenate([vp_ref[0], vc_ref[0]], axis=0).astype(F32)

    lane = lax.broadcasted_iota(jnp.int32, (1, V7X_LANES), 1)
    left = lane < HEAD_DIM_A
    lo_head = lax.broadcasted_iota(jnp.int32, (1, 2 * L), 1) < L

    n_kt = WKV_A // V7X_LANES
    lane_tile = lambda t: slice(t * V7X_LANES, (t + 1) * V7X_LANES)
    kt = [k[:, lane_tile(t)] for t in range(n_kt)]
    kt_sw = [pltpu.roll(x, HEAD_DIM_A, axis=1).astype(BF16) for x in kt]
    kt = [x.astype(BF16) for x in kt]
    pad_row = lax.broadcasted_iota(jnp.int32, (BF16_SUBLANE_TILE, k.shape[0]), 0)
    ones_row = jnp.where(pad_row == 0, 1.0, 0.0)
    vt = [jnp.concatenate([v[:, lane_tile(t)].T, ones_row], axis=0).astype(BF16) for t in range(n_kt)]

    def q_masked(qb, h):
        qt = q_ref[0, qb * L:(qb + 1) * L, lane_tile(h // 2)]
        keep = left if h % 2 == 0 else jnp.logical_not(left)
        return jnp.where(keep, qt, jnp.zeros_like(qt))

    rep = N_Q_A // N_KV_A

    def scores(qb, g, hpar):
        t, par = g // 2, g % 2
        ha, hb = rep * g + hpar, rep * g + hpar + 2
        k_al = (kt if par == hpar else kt_sw)[t][qb * L:(qb + 2) * L]
        qm = jnp.concatenate([q_masked(qb, ha), q_masked(qb, hb)], axis=0)
        first = first_step if qb == 0 else 0
        return _dot_nt(k_al, qm) + bias_ref[first, 2 * g + hpar]

    def weighted_values(qb, g, hpar, s, out_rows):
        t, par = g // 2, g % 2
        ha, hb = rep * g + hpar, rep * g + hpar + 2
        sink = jnp.where(lo_head, sinks_ref[ha], sinks_ref[hb]) * LOG2E
        m = jnp.maximum(jnp.max(s, axis=0, keepdims=True), sink)
        e = jnp.exp2(s - m).astype(BF16)
        ot = _dot(vt[t][:, qb * L:(qb + 2) * L], e)
        denom = ot[V7X_LANES:V7X_LANES + 1, :] + jnp.exp2(sink - m)
        og = ot[par * HEAD_DIM_A:(par + 1) * HEAD_DIM_A, :] * (1.0 / denom)
        out_rows[ha] = og[:, :L]
        out_rows[hb] = og[:, L:]

    tasks = [(qb, g, hpar) for qb in range(QB) for g in range(N_KV_A) for hpar in range(2)]
    ahead = {}
    out_rows = {}
    for t in range(len(tasks) + SWA_SCORE_LOOKAHEAD):
        if t < len(tasks):
            ahead[t] = scores(*tasks[t])
        d = t - SWA_SCORE_LOOKAHEAD
        if d >= 0:
            qb, g, hpar = tasks[d]
            rows = out_rows.setdefault(qb, [None] * N_Q_A)
            weighted_values(qb, g, hpar, ahead.pop(d), rows)
            if (g, hpar) == (N_KV_A - 1, 1):
                o_t = jnp.concatenate(out_rows.pop(qb), axis=0)
                o_ref[0, qb * L:(qb + 1) * L, :] = o_t.T.astype(o_ref.dtype)


def _swa(z3, sinks, bias, side):
    B, S, _ = z3.shape
    L = WINDOW
    QB = SWA_QBLOCKS_PER_STEP
    nb = S // (QB * L)
    kblk = COL_KA // WKV_A
    vblk = COL_VA // WKV_A
    prev = lambda n: jnp.maximum(n * QB - 1, 0)
    side_in, side_out, side_shapes = _side_cast_specs(side, B * nb, lambda b, n: b * nb + n)
    outs = pl.pallas_call(
        _swa_kernel,
        grid=(B, nb),
        in_specs=[
            pl.BlockSpec(memory_space=pltpu.SMEM),
            pl.BlockSpec((1, QB * L, WQ_A), lambda b, n: (b, n, 0)),
            pl.BlockSpec((1, QB * L, WKV_A), lambda b, n: (b, n, kblk)),
            pl.BlockSpec((1, L, WKV_A), lambda b, n: (b, prev(n), kblk)),
            pl.BlockSpec((1, QB * L, WKV_A), lambda b, n: (b, n, vblk)),
            pl.BlockSpec((1, L, WKV_A), lambda b, n: (b, prev(n), vblk)),
            pl.BlockSpec((2, N_Q_A // 2, 2 * L, 2 * L), lambda b, n: (0, 0, 0, 0)),
        ] + side_in,
        out_specs=[pl.BlockSpec((1, QB * L, WQ_A), lambda b, n: (b, n, 0))] + side_out,
        out_shape=[jax.ShapeDtypeStruct((B, S, WQ_A), BF16)] + side_shapes,
        compiler_params=_params(("arbitrary", "arbitrary")),
        name="swa",
    )(sinks, z3, z3, z3, z3, z3, bias, *side)
    return outs[0], outs[1:]


def _swa_bias_table():
    L = WINDOW
    rep = N_Q_A // N_KV_A
    slopes = np.exp2(-8.0 * np.arange(1, N_Q_A + 1, dtype=np.float32) / N_Q_A).astype(np.float32)
    kj = np.arange(2 * L)[:, None]
    qi = np.arange(L)[None, :]
    dist = L + qi - kj
    window = (dist >= 0) & (dist < WINDOW)
    table = np.empty((2, N_Q_A // 2, 2 * L, 2 * L), np.float32)
    for first in range(2):
        valid = window & ((kj >= L) if first else True)
        for g in range(N_KV_A):
            for hpar in range(2):
                for a in range(2):
                    h = rep * g + hpar + 2 * a
                    table[first, 2 * g + hpar, :, a * L:(a + 1) * L] = np.where(
                        valid, -(slopes[h] * LOG2E) * dist.astype(np.float32), -np.inf)
    return jnp.asarray(table)


def _moba_kernel(slopes_ref, q_ref, k_ref, v_ref, *refs):
    n_side = (len(refs) - 4) // 2
    o_ref = refs[n_side]
    vt_ref, gate_ref, tab_ref = refs[2 * n_side + 1:]
    _side_cast(refs[:n_side] + refs[n_side + 1:2 * n_side + 1])
    BLK = MOBA_BLOCK
    HP = MOBA_HEADS_PER_STEP
    dh = HEAD_DIM_B
    hp = pl.program_id(1)
    S = k_ref.shape[1]
    nblk = S // BLK

    def prepare_tables():
        kc = lax.broadcasted_iota(jnp.int32, (BLK, BLK), 0)
        qr = lax.broadcasted_iota(jnp.int32, (BLK, BLK), 1)
        rel = (qr - kc).astype(F32)
        for a in range(HP):
            slope2 = slopes_ref[hp * HP + a] * LOG2E
            tab_ref[a, 0] = -slope2 * rel
            tab_ref[a, 1] = jnp.where(rel >= 0.0, -slope2 * rel, NEG_INF)

    def prepare_values():
        eye = (lax.broadcasted_iota(jnp.int32, (dh, dh), 0)
               == lax.broadcasted_iota(jnp.int32, (dh, dh), 1)).astype(BF16)
        pad_row = lax.broadcasted_iota(jnp.int32, (BF16_SUBLANE_TILE, S), 0)
        for a in range(HP):
            vt_ref[a, 0:dh, :] = _dot_nt(eye, v_ref[0, :, a * dh:(a + 1) * dh]).astype(BF16)
            vt_ref[a, dh:, :] = jnp.where(pad_row == 0, 1.0, 0.0).astype(BF16)

    def block_means():
        blk = lax.broadcasted_iota(jnp.int32, (GATE_ROWS, S), 0)
        pos = lax.broadcasted_iota(jnp.int32, (GATE_ROWS, S), 1)
        member = jnp.where((pos >= blk * BLK) & (pos < (blk + 1) * BLK), 1.0 / BLK, 0.0).astype(BF16)
        return [_split_bf16(_dot(member, k_ref[0, :, a * dh:(a + 1) * dh])) for a in range(HP)]

    def prepare_gate(kmeans):
        for a, (km_hi, km_lo) in enumerate(kmeans):
            qn = q_ref[0, :, a * dh:(a + 1) * dh]
            gate_ref[a] = _dot_nt(km_hi, qn) + _dot_nt(km_lo, qn)

    def block_max_shifts(c, a):
        slope2 = slopes_ref[hp * HP + a] * LOG2E
        shifts = [-slope2 * float((c - n) * BLK) for n in range(c)]
        if c > MOBA_TOPK:
            gate = gate_ref[a, :, c * BLK:(c + 1) * BLK]
            blk = lax.broadcasted_iota(jnp.int32, (GATE_ROWS, 1), 0)
            past = blk < c
            for n in range(c):
                g_n = gate[n:n + 1, :]
                beats = ((gate > g_n) | ((gate == g_n) & (blk < n))) & past
                rank = jnp.sum(jnp.where(beats, 1.0, 0.0), axis=0, keepdims=True)
                shifts[n] = jnp.where(rank < float(MOBA_TOPK), shifts[n], NEG_INF)
        return shifts

    def scores(c, a, ss, ms):
        for j in range(c + 1):
            s = _dot_nt(k_ref[0, j * BLK:(j + 1) * BLK, a * dh:(a + 1) * dh],
                        q_ref[0, c * BLK:(c + 1) * BLK, a * dh:(a + 1) * dh])
            s = s + tab_ref[a, 1 if j == c else 0]
            ss.append(s)
            ms.append(jnp.max(s, axis=0, keepdims=True))
            yield

    def weighted_values(c, a, ss, ms):
        shifts = block_max_shifts(c, a)
        m = functools.reduce(jnp.maximum, [m_j + sh for m_j, sh in zip(ms, shifts)] + ms[c:])
        acc = None
        for j in range(c + 1):
            p = jnp.exp2(ss[j] - (m - shifts[j] if j < c else m)).astype(BF16)
            part = _dot(vt_ref[a, :, j * BLK:(j + 1) * BLK], p)
            acc = part if acc is None else acc + part
            if j < c:
                yield
        ot = acc[0:dh] * (1.0 / acc[dh:dh + 1])
        o_ref[0, c * BLK:(c + 1) * BLK, a * dh:(a + 1) * dh] = ot.T.astype(o_ref.dtype)
        yield

    prepare_tables()
    kmeans = block_means()
    tasks = [(c, a) for c in range(nblk) for a in range(HP)]
    ahead = {}
    for t in range(len(tasks) + MOBA_TASK_LOOKAHEAD):
        running = []
        if t < len(tasks):
            ahead[t] = ([], [])
            running.append(scores(*tasks[t], *ahead[t]))
        d = t - MOBA_TASK_LOOKAHEAD
        if d >= 0:
            c, a = tasks[d]
            if d == 0:
                prepare_values()
            if (c, a) == (MOBA_TOPK, 0):
                prepare_gate(kmeans)
            running.append(weighted_values(c, a, *ahead.pop(d)))
        while running:
            running = [g for g in running if next(g, StopIteration) is not StopIteration]


def _moba(z3, slopes, side):
    B, S, _ = z3.shape
    BLK = MOBA_BLOCK
    HP = MOBA_HEADS_PER_STEP
    w = HP * HEAD_DIM_B
    qc, kc, vc = COL_QB // w, COL_KB // w, COL_VB // w
    n_hp = N_H_B // HP
    side_in, side_out, side_shapes = _side_cast_specs(side, B * n_hp, lambda b, h: b * n_hp + h)
    outs = pl.pallas_call(
        _moba_kernel,
        grid=(B, n_hp),
        in_specs=[
            pl.BlockSpec(memory_space=pltpu.SMEM),
            pl.BlockSpec((1, S, w), lambda b, h: (b, 0, qc + h)),
            pl.BlockSpec((1, S, w), lambda b, h: (b, 0, kc + h)),
            pl.BlockSpec((1, S, w), lambda b, h: (b, 0, vc + h)),
        ] + side_in,
        out_specs=[pl.BlockSpec((1, S, w), lambda b, h: (b, 0, h))] + side_out,
        out_shape=[jax.ShapeDtypeStruct((B, S, W_B), BF16)] + side_shapes,
        scratch_shapes=[
            pltpu.VMEM((HP, HEAD_DIM_B + BF16_SUBLANE_TILE, S), BF16),
            pltpu.VMEM((HP, GATE_ROWS, S), F32),
            pltpu.VMEM((HP, 2, BLK, BLK), F32),
        ],
        compiler_params=_params(("arbitrary", "arbitrary")),
        name="moba",
    )(slopes, z3, z3, z3, *[s[0] if isinstance(s, tuple) else s for s in side])
    return outs[0], outs[1:]


def _sigmoid(x):
    return 1.0 / (1.0 + jnp.exp2(x * (-LOG2E)))


def _merge_out_kernel(oa_ref, ob_ref, ga_ref, gb_ref, ba_ref, bb_ref, wa_ref, wb_ref, wo_ref, x_ref,
                      g2_ref, o_ref, u2_ref, mixed_ref, *, chunk):
    oa = oa_ref[...]
    ob = ob_ref[...]
    for c0 in range(0, mixed_ref.shape[1], chunk):
        cs = slice(c0, c0 + chunk)
        a = _dot(oa, wa_ref[:, cs])
        b = _dot(ob, wb_ref[:, cs])
        ga = _sigmoid(ga_ref[:, cs].astype(F32) + ba_ref[:, cs])
        gb = _sigmoid(gb_ref[:, cs].astype(F32) + bb_ref[:, cs])
        mixed_ref[:, cs] = (ga * a + gb * b).astype(mixed_ref.dtype)
    h1 = x_ref[...] + _dot(mixed_ref[...], wo_ref[...])
    o_ref[...] = h1
    inv = lax.rsqrt(jnp.mean(h1 * h1, axis=-1, keepdims=True) + RMS_EPS)
    u2_ref[...] = (h1 * inv * g2_ref[...]).astype(u2_ref.dtype)


def _merge_out(oa, ob, gates, ba, bb, wa, wb, wo, x, g2, *, tm=512, chunk=512):
    m, ka = oa.shape
    kb = ob.shape[1]
    d = wa.shape[1]
    resident = pl.Buffered(1)
    return pl.pallas_call(
        functools.partial(_merge_out_kernel, chunk=chunk),
        grid=(m // tm,),
        in_specs=[
            pl.BlockSpec((tm, ka), lambda i: (i, 0)),
            pl.BlockSpec((tm, kb), lambda i: (i, 0)),
            pl.BlockSpec((tm, d), lambda i: (i, 0)),
            pl.BlockSpec((tm, d), lambda i: (i, 1)),
            pl.BlockSpec((1, d), lambda i: (0, 0)),
            pl.BlockSpec((1, d), lambda i: (0, 0)),
            pl.BlockSpec((ka, d), lambda i: (0, 0), pipeline_mode=resident),
            pl.BlockSpec((kb, d), lambda i: (0, 0), pipeline_mode=resident),
            pl.BlockSpec((d, d), lambda i: (0, 0), pipeline_mode=resident),
            pl.BlockSpec((tm, d), lambda i: (i, 0)),
            pl.BlockSpec((1, d), lambda i: (0, 0)),
        ],
        out_specs=[pl.BlockSpec((tm, d), lambda i: (i, 0)), pl.BlockSpec((tm, d), lambda i: (i, 0))],
        out_shape=[jax.ShapeDtypeStruct((m, d), F32), jax.ShapeDtypeStruct((m, d), BF16)],
        scratch_shapes=[pltpu.VMEM((tm, d), BF16)],
        compiler_params=_params(("parallel",)),
        name="merge_out",
    )(oa, ob, gates, gates, ba, bb, wa, wb, wo, x, g2)


def _mm_res_kernel(a_ref, w_ref, r_ref, o_ref):
    o_ref[...] = r_ref[...] + _dot(a_ref[...], w_ref[...])


def _mm_res(a, w, res, *, tm, tn, name):
    m, k = a.shape
    n = w.shape[1]
    return pl.pallas_call(
        _mm_res_kernel,
        grid=(m // tm, n // tn),
        in_specs=[
            pl.BlockSpec((tm, k), lambda i, j: (i, 0)),
            pl.BlockSpec((k, tn), lambda i, j: (0, j)),
            pl.BlockSpec((tm, tn), lambda i, j: (i, j)),
        ],
        out_specs=pl.BlockSpec((tm, tn), lambda i, j: (i, j)),
        out_shape=jax.ShapeDtypeStruct((m, n), F32),
        compiler_params=_params(("parallel", "parallel")),
        name=name,
    )(a, w, res)


def _ffn_up_kernel(u_ref, wg_ref, wu_ref, o_ref):
    u = u_ref[...]
    for c0 in range(0, o_ref.shape[1], V7X_MXU_DIM):
        cs = slice(c0, c0 + V7X_MXU_DIM)
        gate = _dot(u, wg_ref[:, cs])
        up = _dot(u, wu_ref[:, cs])
        o_ref[:, cs] = (gate * _sigmoid(gate) * up).astype(o_ref.dtype)


def _ffn_up(u, wg, wu, *, tm=2048, tn=512):
    m, d = u.shape
    n = wg.shape[1]
    return pl.pallas_call(
        _ffn_up_kernel,
        grid=(m // tm, n // tn),
        in_specs=[
            pl.BlockSpec((tm, d), lambda i, j: (i, 0)),
            pl.BlockSpec((d, tn), lambda i, j: (0, j)),
            pl.BlockSpec((d, tn), lambda i, j: (0, j)),
        ],
        out_specs=pl.BlockSpec((tm, tn), lambda i, j: (i, j)),
        out_shape=jax.ShapeDtypeStruct((m, n), BF16),
        compiler_params=_params(("parallel", "parallel")),
        name="ffn_up",
    )(u, wg, wu)


def kernel(x, norm1_g, w_in, b_gate, q_norm_a, k_norm_a, sinks_a, q_norm_b, k_norm_b,
           w_branch_a, w_branch_b, w_o, norm2_g, w_ffn_gate, w_ffn_up, w_ffn_down):
    B, S, D = x.shape
    depth = w_in.shape[0]
    assert D == D_MODEL and w_in.shape[2] == IN_COLS
    assert S % MOBA_BLOCK == 0 and S % WINDOW == 0
    M = B * S

    slopes_b = jnp.asarray(np.exp2(-8.0 * np.arange(1, N_H_B + 1, dtype=np.float32) / N_H_B), F32)
    swa_bias = _swa_bias_table()

    h = x.reshape(M, D)
    for l in range(depth):
        head_gain = jnp.concatenate([
            jnp.tile(q_norm_a[l], N_Q_A) * (HEAD_DIM_A ** -0.5 * LOG2E),
            jnp.tile(k_norm_a[l], N_KV_A),
            jnp.ones((WKV_A,), F32),
            jnp.tile(q_norm_b[l], N_H_B) * (HEAD_DIM_B ** -0.5 * LOG2E),
            jnp.tile(k_norm_b[l], N_H_B),
            jnp.ones((W_B,), F32),
        ]).reshape(1, COL_GA)
        qkv, u = _qkv_proj(h, norm1_g[l].reshape(1, D), w_in[l, :, :COL_GA].astype(BF16), head_gain, tm=512)
        z3 = qkv.reshape(B, S, COL_GA)
        o_a, (wg_b, wu_b) = _swa(z3, sinks_a[l], swa_bias, [w_ffn_gate[l], w_ffn_up[l]])
        o_b, (wd_b, wgate_b) = _moba(z3, slopes_b,
                                     [w_ffn_down[l], (w_in[l], COL_GA, 2 * D, SIDE_CAST_COL_TILE)])
        gates, (wo_b, wa_b, wb_b) = _gate_proj(u, wgate_b, [w_o[l], w_branch_a[l], w_branch_b[l]])
        h1, u2 = _merge_out(o_a.reshape(M, WQ_A), o_b.reshape(M, W_B), gates,
                            b_gate[l, :D].reshape(1, D), b_gate[l, D:].reshape(1, D),
                            wa_b, wb_b, wo_b, h, norm2_g[l].reshape(1, D))
        act = _ffn_up(u2, wg_b, wu_b)
        h = _mm_res(act, wd_b, h1, tm=1024, tn=512, name="ffn_down")
    return h.reshape(B, S, D)
```

```python
import functools

import numpy as np
import jax
import jax.numpy as jnp
from jax import lax
from jax.experimental import pallas as pl
from jax.experimental.pallas import tpu as pltpu

F32 = jnp.float32
BF16 = jnp.bfloat16

D_MODEL = 2048
HEAD_DIM_A = 64
N_Q_A = 16
N_KV_A = 4
WINDOW = 128
HEAD_DIM_B = 128
N_H_B = 8
MOBA_BLOCK = 256
MOBA_TOPK = 3
RMS_EPS = 1e-6

WQ_A = N_Q_A * HEAD_DIM_A
WKV_A = N_KV_A * HEAD_DIM_A
W_B = N_H_B * HEAD_DIM_B
COL_QA = 0
COL_KA = COL_QA + WQ_A
COL_VA = COL_KA + WKV_A
COL_QB = COL_VA + WKV_A
COL_KB = COL_QB + W_B
COL_VB = COL_KB + W_B
COL_GA = COL_VB + W_B
COL_GB = COL_GA + D_MODEL
IN_COLS = COL_GB + D_MODEL

V7X_LANES = 128
V7X_MXU_DIM = 256
VMEM_LIMIT_BYTES = 56 * 1024 * 1024
NORM_ROW_CHUNK = 64
NORM_UNROLL = 8
BF16_SUBLANE_TILE = 16
SIDE_CAST_COL_TILE = 512
SWA_QBLOCKS_PER_STEP = 8
SWA_SCORE_LOOKAHEAD = 4
MOBA_HEADS_PER_STEP = 2
MOBA_TASK_LOOKAHEAD = 4
GATE_ROWS = BF16_SUBLANE_TILE
NEG_INF = float("-inf")
LOG2E = 1.4426950408889634


def _params(sem):
    return pltpu.CompilerParams(dimension_semantics=sem, vmem_limit_bytes=VMEM_LIMIT_BYTES)


def _dot(a, b):
    return jnp.dot(a, b, preferred_element_type=F32)


def _dot_nt(a, b):
    return lax.dot_general(a, b, (((1,), (1,)), ((), ())), preferred_element_type=F32)


def _split_bf16(x):
    hi = x.astype(BF16)
    lo = (x - hi.astype(F32)).astype(BF16)
    return hi, lo


def _rms_rows_to_bf16(x_ref, g_ref, u_ref):
    rows = x_ref.shape[0]
    g = g_ref[...]

    def body(c, carry):
        r = pl.multiple_of(c * NORM_ROW_CHUNK, NORM_ROW_CHUNK)
        x = x_ref[pl.ds(r, NORM_ROW_CHUNK), :]
        ms = jnp.mean(x * x, axis=-1, keepdims=True)
        u_ref[pl.ds(r, NORM_ROW_CHUNK), :] = (x * lax.rsqrt(ms + RMS_EPS) * g).astype(BF16)
        return carry

    lax.fori_loop(0, rows // NORM_ROW_CHUNK, body, 0, unroll=NORM_UNROLL)


def _head_norm_kind(col):
    if col < COL_VA:
        return "norm64"
    if COL_QB <= col < COL_VB:
        return "norm128"
    return "copy"


def _qkv_proj_kernel(x_ref, g_ref, w_ref, hg_ref, o_ref, u_ref):
    _rms_rows_to_bf16(x_ref, g_ref, u_ref)
    CH = V7X_MXU_DIM

    def finish(z, cs, kind):
        if kind == "norm64":
            left = lax.broadcasted_iota(jnp.int32, (1, V7X_LANES), 1) < HEAD_DIM_A
            tiles = []
            for h0 in range(0, CH, V7X_LANES):
                zt = z[:, h0:h0 + V7X_LANES]
                sq = zt * zt
                ms_l = jnp.sum(jnp.where(left, sq, 0.0), axis=-1, keepdims=True)
                ms_r = jnp.sum(jnp.where(left, 0.0, sq), axis=-1, keepdims=True)
                ms = jnp.where(left, ms_l, ms_r) * (1.0 / HEAD_DIM_A)
                tiles.append(zt * lax.rsqrt(ms + RMS_EPS))
            z = jnp.concatenate(tiles, axis=1) * hg_ref[:, cs]
        elif kind == "norm128":
            halves = []
            for h0 in range(0, CH, HEAD_DIM_B):
                zh = z[:, h0:h0 + HEAD_DIM_B]
                ms = jnp.mean(zh * zh, axis=-1, keepdims=True)
                halves.append(zh * lax.rsqrt(ms + RMS_EPS))
            z = jnp.concatenate(halves, axis=1) * hg_ref[:, cs]
        o_ref[:, cs] = z.astype(o_ref.dtype)

    u = u_ref[...]
    chunks = [slice(c0, c0 + CH) for c0 in range(0, o_ref.shape[1], CH)]
    z_next = _dot(u, w_ref[:, chunks[0]])
    for idx, cs in enumerate(chunks):
        z = z_next
        if idx + 1 < len(chunks):
            z_next = _dot(u, w_ref[:, chunks[idx + 1]])
        finish(z, cs, _head_norm_kind(cs.start))


def _qkv_proj(x, g, w, head_gain, *, tm):
    m, d = x.shape
    n = w.shape[1]
    assert n % V7X_MXU_DIM == 0
    resident = pl.Buffered(1)
    return pl.pallas_call(
        _qkv_proj_kernel,
        grid=(m // tm,),
        in_specs=[
            pl.BlockSpec((tm, d), lambda i: (i, 0)),
            pl.BlockSpec((1, d), lambda i: (0, 0)),
            pl.BlockSpec((d, n), lambda i: (0, 0), pipeline_mode=resident),
            pl.BlockSpec((1, n), lambda i: (0, 0)),
        ],
        out_specs=[
            pl.BlockSpec((tm, n), lambda i: (i, 0)),
            pl.BlockSpec((tm, d), lambda i: (i, 0)),
        ],
        out_shape=[jax.ShapeDtypeStruct((m, n), BF16), jax.ShapeDtypeStruct((m, d), BF16)],
        compiler_params=_params(("parallel",)),
        name="qkv_proj",
    )(x, g, w, head_gain)


def _side_cast_specs(arrays, n_steps, step_index):
    in_specs, out_specs, shapes = [], [], []
    for item in arrays:
        w, col0, ncols, col_tile = item if isinstance(item, tuple) else (item, 0, item.shape[1], item.shape[1])
        rows = w.shape[0]
        n_col = ncols // col_tile
        n_row = n_steps // n_col
        chunk = rows // n_row
        assert n_col * col_tile == ncols and n_row * n_col == n_steps and col0 % col_tile == 0, (w.shape, n_steps)
        assert chunk * n_row == rows and chunk % BF16_SUBLANE_TILE == 0, (w.shape, n_steps)

        def in_map(*g, n_col=n_col, c0=col0 // col_tile):
            s = step_index(*g)
            return (s // n_col, c0 + s % n_col)

        def out_map(*g, n_col=n_col):
            s = step_index(*g)
            return (s // n_col, s % n_col)

        in_specs.append(pl.BlockSpec((chunk, col_tile), in_map))
        out_specs.append(pl.BlockSpec((chunk, col_tile), out_map))
        shapes.append(jax.ShapeDtypeStruct((rows, ncols), BF16))
    return in_specs, out_specs, shapes


def _side_cast(refs):
    n = len(refs) // 2
    for src, dst in zip(refs[:n], refs[n:]):
        dst[...] = src[...].astype(dst.dtype)


def _gate_proj_kernel(u_ref, w_ref, *refs):
    n_side = (len(refs) - 1) // 2
    o_ref = refs[n_side]
    o_ref[...] = _dot(u_ref[...], w_ref[...]).astype(o_ref.dtype)
    _side_cast(refs[:n_side] + refs[n_side + 1:])


def _gate_proj(u, w, side, *, tm=2048, tn=1024):
    m, k = u.shape
    n = w.shape[1]
    gi, gj = m // tm, n // tn
    side_in, side_out, side_shapes = _side_cast_specs(side, gi * gj, lambda i, j: i * gj + j)
    outs = pl.pallas_call(
        _gate_proj_kernel,
        grid=(gi, gj),
        in_specs=[
            pl.BlockSpec((tm, k), lambda i, j: (i, 0)),
            pl.BlockSpec((k, tn), lambda i, j: (0, j)),
        ] + side_in,
        out_specs=[pl.BlockSpec((tm, tn), lambda i, j: (i, j))] + side_out,
        out_shape=[jax.ShapeDtypeStruct((m, n), BF16)] + side_shapes,
        compiler_params=_params(("arbitrary", "arbitrary")),
        name="gate_proj",
    )(u, w, *side)
    return outs[0], outs[1:]


def _swa_kernel(sinks_ref, q_ref, kc_ref, kp_ref, vc_ref, vp_ref, bias_ref, *refs):
    n_side = (len(refs) - 1) // 2
    o_ref = refs[n_side]
    _side_cast(refs[:n_side] + refs[n_side + 1:])
    L = WINDOW
    QB = SWA_QBLOCKS_PER_STEP
    first_step = (pl.program_id(1) == 0).astype(jnp.int32)
    k = jnp.concatenate([kp_ref[0], kc_ref[0]], axis=0).astype(F32)
    v = jnp.concatenate([vp_ref[0], vc_ref[0]], axis=0).astype(F32)

    lane = lax.broadcasted_iota(jnp.int32, (1, V7X_LANES), 1)
    left = lane < HEAD_DIM_A
    lo_head = lax.broadcasted_iota(jnp.int32, (1, 2 * L), 1) < L

    n_kt = WKV_A // V7X_LANES
    lane_tile = lambda t: slice(t * V7X_LANES, (t + 1) * V7X_LANES)
    kt = [k[:, lane_tile(t)] for t in range(n_kt)]
    kt_sw = [pltpu.roll(x, HEAD_DIM_A, axis=1).astype(BF16) for x in kt]
    kt = [x.astype(BF16) for x in kt]
    pad_row = lax.broadcasted_iota(jnp.int32, (BF16_SUBLANE_TILE, k.shape[0]), 0)
    ones_row = jnp.where(pad_row == 0, 1.0, 0.0)
    vt = [jnp.concatenate([v[:, lane_tile(t)].T, ones_row], axis=0).astype(BF16) for t in range(n_kt)]

    def q_masked(qb, h):
        qt = q_ref[0, qb * L:(qb + 1) * L, lane_tile(h // 2)]
        keep = left if h % 2 == 0 else jnp.logical_not(left)
        return jnp.where(keep, qt, jnp.zeros_like(qt))

    rep = N_Q_A // N_KV_A

    def scores(qb, g, hpar):
        t, par = g // 2, g % 2
        ha, hb = rep * g + hpar, rep * g + hpar + 2
        k_al = (kt if par == hpar else kt_sw)[t][qb * L:(qb + 2) * L]
        qm = jnp.concatenate([q_masked(qb, ha), q_masked(qb, hb)], axis=0)
        first = first_step if qb == 0 else 0
        return _dot_nt(k_al, qm) + bias_ref[first, 2 * g + hpar]

    def weighted_values(qb, g, hpar, s, out_rows):
        t, par = g // 2, g % 2
        ha, hb = rep * g + hpar, rep * g + hpar + 2
        sink = jnp.where(lo_head, sinks_ref[ha], sinks_ref[hb]) * LOG2E
        m = jnp.maximum(jnp.max(s, axis=0, keepdims=True), sink)
        e = jnp.exp2(s - m).astype(BF16)
        ot = _dot(vt[t][:, qb * L:(qb + 2) * L], e)
        denom = ot[V7X_LANES:V7X_LANES + 1, :] + jnp.exp2(sink - m)
        og = ot[par * HEAD_DIM_A:(par + 1) * HEAD_DIM_A, :] * (1.0 / denom)
        out_rows[ha] = og[:, :L]
        out_rows[hb] = og[:, L:]

    tasks = [(qb, g, hpar) for qb in range(QB) for g in range(N_KV_A) for hpar in range(2)]
    ahead = {}
    out_rows = {}
    for t in range(len(tasks) + SWA_SCORE_LOOKAHEAD):
        if t < len(tasks):
            ahead[t] = scores(*tasks[t])
        d = t - SWA_SCORE_LOOKAHEAD
        if d >= 0:
            qb, g, hpar = tasks[d]
            rows = out_rows.setdefault(qb, [None] * N_Q_A)
            weighted_values(qb, g, hpar, ahead.pop(d), rows)
            if (g, hpar) == (N_KV_A - 1, 1):
                o_t = jnp.concatenate(out_rows.pop(qb), axis=0)
                o_ref[0, qb * L:(qb + 1) * L, :] = o_t.T.astype(o_ref.dtype)


def _swa(z3, sinks, bias, side):
    B, S, _ = z3.shape
    L = WINDOW
    QB = SWA_QBLOCKS_PER_STEP
    nb = S // (QB * L)
    kblk = COL_KA // WKV_A
    vblk = COL_VA // WKV_A
    prev = lambda n: jnp.maximum(n * QB - 1, 0)
    side_in, side_out, side_shapes = _side_cast_specs(side, B * nb, lambda b, n: b * nb + n)
    outs = pl.pallas_call(
        _swa_kernel,
        grid=(B, nb),
        in_specs=[
            pl.BlockSpec(memory_space=pltpu.SMEM),
            pl.BlockSpec((1, QB * L, WQ_A), lambda b, n: (b, n, 0)),
            pl.BlockSpec((1, QB * L, WKV_A), lambda b, n: (b, n, kblk)),
            pl.BlockSpec((1, L, WKV_A), lambda b, n: (b, prev(n), kblk)),
            pl.BlockSpec((1, QB * L, WKV_A), lambda b, n: (b, n, vblk)),
            pl.BlockSpec((1, L, WKV_A), lambda b, n: (b, prev(n), vblk)),
            pl.BlockSpec((2, N_Q_A // 2, 2 * L, 2 * L), lambda b, n: (0, 0, 0, 0)),
        ] + side_in,
        out_specs=[pl.BlockSpec((1, QB * L, WQ_A), lambda b, n: (b, n, 0))] + side_out,
        out_shape=[jax.ShapeDtypeStruct((B, S, WQ_A), BF16)] + side_shapes,
        compiler_params=_params(("arbitrary", "arbitrary")),
        name="swa",
    )(sinks, z3, z3, z3, z3, z3, bias, *side)
    return outs[0], outs[1:]


def _swa_bias_table():
    L = WINDOW
    rep = N_Q_A // N_KV_A
    slopes = np.exp2(-8.0 * np.arange(1, N_Q_A + 1, dtype=np.float32) / N_Q_A).astype(np.float32)
    kj = np.arange(2 * L)[:, None]
    qi = np.arange(L)[None, :]
    dist = L + qi - kj
    window = (dist >= 0) & (dist < WINDOW)
    table = np.empty((2, N_Q_A // 2, 2 * L, 2 * L), np.float32)
    for first in range(2):
        valid = window & ((kj >= L) if first else True)
        for g in range(N_KV_A):
            for hpar in range(2):
                for a in range(2):
                    h = rep * g + hpar + 2 * a
                    table[first, 2 * g + hpar, :, a * L:(a + 1) * L] = np.where(
                        valid, -(slopes[h] * LOG2E) * dist.astype(np.float32), -np.inf)
    return jnp.asarray(table)


def _moba_kernel(slopes_ref, q_ref, k_ref, v_ref, *refs):
    n_side = (len(refs) - 4) // 2
    o_ref = refs[n_side]
    vt_ref, gate_ref, tab_ref = refs[2 * n_side + 1:]
    _side_cast(refs[:n_side] + refs[n_side + 1:2 * n_side + 1])
    BLK = MOBA_BLOCK
    HP = MOBA_HEADS_PER_STEP
    dh = HEAD_DIM_B
    hp = pl.program_id(1)
    S = k_ref.shape[1]
    nblk = S // BLK

    def prepare_tables():
        kc = lax.broadcasted_iota(jnp.int32, (BLK, BLK), 0)
        qr = lax.broadcasted_iota(jnp.int32, (BLK, BLK), 1)
        rel = (qr - kc).astype(F32)
        for a in range(HP):
            slope2 = slopes_ref[hp * HP + a] * LOG2E
            tab_ref[a, 0] = -slope2 * rel
            tab_ref[a, 1] = jnp.where(rel >= 0.0, -slope2 * rel, NEG_INF)

    def prepare_values():
        eye = (lax.broadcasted_iota(jnp.int32, (dh, dh), 0)
               == lax.broadcasted_iota(jnp.int32, (dh, dh), 1)).astype(BF16)
        pad_row = lax.broadcasted_iota(jnp.int32, (BF16_SUBLANE_TILE, S), 0)
        for a in range(HP):
            vt_ref[a, 0:dh, :] = _dot_nt(eye, v_ref[0, :, a * dh:(a + 1) * dh]).astype(BF16)
            vt_ref[a, dh:, :] = jnp.where(pad_row == 0, 1.0, 0.0).astype(BF16)

    def block_means():
        blk = lax.broadcasted_iota(jnp.int32, (GATE_ROWS, S), 0)
        pos = lax.broadcasted_iota(jnp.int32, (GATE_ROWS, S), 1)
        member = jnp.where((pos >= blk * BLK) & (pos < (blk + 1) * BLK), 1.0 / BLK, 0.0).astype(BF16)
        return [_split_bf16(_dot(member, k_ref[0, :, a * dh:(a + 1) * dh])) for a in range(HP)]

    def prepare_gate(kmeans):
        for a, (km_hi, km_lo) in enumerate(kmeans):
            qn = q_ref[0, :, a * dh:(a + 1) * dh]
            gate_ref[a] = _dot_nt(km_hi, qn) + _dot_nt(km_lo, qn)

    def block_max_shifts(c, a):
        slope2 = slopes_ref[hp * HP + a] * LOG2E
        shifts = [-slope2 * float((c - n) * BLK) for n in range(c)]
        if c > MOBA_TOPK:
            gate = gate_ref[a, :, c * BLK:(c + 1) * BLK]
            blk = lax.broadcasted_iota(jnp.int32, (GATE_ROWS, 1), 0)
            past = blk < c
            for n in range(c):
                g_n = gate[n:n + 1, :]
                beats = ((gate > g_n) | ((gate == g_n) & (blk < n))) & past
                rank = jnp.sum(jnp.where(beats, 1.0, 0.0), axis=0, keepdims=True)
                shifts[n] = jnp.where(rank < float(MOBA_TOPK), shifts[n], NEG_INF)
        return shifts

    def scores(c, a, ss, ms):
        for j in range(c + 1):
            s = _dot_nt(k_ref[0, j * BLK:(j + 1) * BLK, a * dh:(a + 1) * dh],
                        q_ref[0, c * BLK:(c + 1) * BLK, a * dh:(a + 1) * dh])
            s = s + tab_ref[a, 1 if j == c else 0]
            ss.append(s)
            ms.append(jnp.max(s, axis=0, keepdims=True))
            yield

    def weighted_values(c, a, ss, ms):
        shifts = block_max_shifts(c, a)
        m = functools.reduce(jnp.maximum, [m_j + sh for m_j, sh in zip(ms, shifts)] + ms[c:])
        acc = None
        for j in range(c + 1):
            p = jnp.exp2(ss[j] - (m - shifts[j] if j < c else m)).astype(BF16)
            part = _dot(vt_ref[a, :, j * BLK:(j + 1) * BLK], p)
            acc = part if acc is None else acc + part
            if j < c:
                yield
        ot = acc[0:dh] * (1.0 / acc[dh:dh + 1])
        o_ref[0, c * BLK:(c + 1) * BLK, a * dh:(a + 1) * dh] = ot.T.astype(o_ref.dtype)
        yield

    prepare_tables()
    kmeans = block_means()
    tasks = [(c, a) for c in range(nblk) for a in range(HP)]
    ahead = {}
    for t in range(len(tasks) + MOBA_TASK_LOOKAHEAD):
        running = []
        if t < len(tasks):
            ahead[t] = ([], [])
            running.append(scores(*tasks[t], *ahead[t]))
        d = t - MOBA_TASK_LOOKAHEAD
        if d >= 0:
            c, a = tasks[d]
            if d == 0:
                prepare_values()
            if (c, a) == (MOBA_TOPK, 0):
                prepare_gate(kmeans)
            running.append(weighted_values(c, a, *ahead.pop(d)))
        while running:
            running = [g for g in running if next(g, StopIteration) is not StopIteration]


def _moba(z3, slopes, side):
    B, S, _ = z3.shape
    BLK = MOBA_BLOCK
    HP = MOBA_HEADS_PER_STEP
    w = HP * HEAD_DIM_B
    qc, kc, vc = COL_QB // w, COL_KB // w, COL_VB // w
    n_hp = N_H_B // HP
    side_in, side_out, side_shapes = _side_cast_specs(side, B * n_hp, lambda b, h: b * n_hp + h)
    outs = pl.pallas_call(
        _moba_kernel,
        grid=(B, n_hp),
        in_specs=[
            pl.BlockSpec(memory_space=pltpu.SMEM),
            pl.BlockSpec((1, S, w), lambda b, h: (b, 0, qc + h)),
            pl.BlockSpec((1, S, w), lambda b, h: (b, 0, kc + h)),
            pl.BlockSpec((1, S, w), lambda b, h: (b, 0, vc + h)),
        ] + side_in,
        out_specs=[pl.BlockSpec((1, S, w), lambda b, h: (b, 0, h))] + side_out,
        out_shape=[jax.ShapeDtypeStruct((B, S, W_B), BF16)] + side_shapes,
        scratch_shapes=[
            pltpu.VMEM((HP, HEAD_DIM_B + BF16_SUBLANE_TILE, S), BF16),
            pltpu.VMEM((HP, GATE_ROWS, S), F32),
            pltpu.VMEM((HP, 2, BLK, BLK), F32),
        ],
        compiler_params=_params(("arbitrary", "arbitrary")),
        name="moba",
    )(slopes, z3, z3, z3, *[s[0] if isinstance(s, tuple) else s for s in side])
    return outs[0], outs[1:]


def _sigmoid(x):
    return 1.0 / (1.0 + jnp.exp2(x * (-LOG2E)))


def _merge_out_kernel(oa_ref, ob_ref, ga_ref, gb_ref, ba_ref, bb_ref, wa_ref, wb_ref, wo_ref, x_ref,
                      g2_ref, o_ref, u2_ref, mixed_ref, *, chunk):
    oa = oa_ref[...]
    ob = ob_ref[...]
    for c0 in range(0, mixed_ref.shape[1], chunk):
        cs = slice(c0, c0 + chunk)
        a = _dot(oa, wa_ref[:, cs])
        b = _dot(ob, wb_ref[:, cs])
        ga = _sigmoid(ga_ref[:, cs].astype(F32) + ba_ref[:, cs])
        gb = _sigmoid(gb_ref[:, cs].astype(F32) + bb_ref[:, cs])
        mixed_ref[:, cs] = (ga * a + gb * b).astype(mixed_ref.dtype)
    h1 = x_ref[...] + _dot(mixed_ref[...], wo_ref[...])
    o_ref[...] = h1
    inv = lax.rsqrt(jnp.mean(h1 * h1, axis=-1, keepdims=True) + RMS_EPS)
    u2_ref[...] = (h1 * inv * g2_ref[...]).astype(u2_ref.dtype)


def _merge_out(oa, ob, gates, ba, bb, wa, wb, wo, x, g2, *, tm=512, chunk=512):
    m, ka = oa.shape
    kb = ob.shape[1]
    d = wa.shape[1]
    resident = pl.Buffered(1)
    return pl.pallas_call(
        functools.partial(_merge_out_kernel, chunk=chunk),
        grid=(m // tm,),
        in_specs=[
            pl.BlockSpec((tm, ka), lambda i: (i, 0)),
            pl.BlockSpec((tm, kb), lambda i: (i, 0)),
            pl.BlockSpec((tm, d), lambda i: (i, 0)),
            pl.BlockSpec((tm, d), lambda i: (i, 1)),
            pl.BlockSpec((1, d), lambda i: (0, 0)),
            pl.BlockSpec((1, d), lambda i: (0, 0)),
            pl.BlockSpec((ka, d), lambda i: (0, 0), pipeline_mode=resident),
            pl.BlockSpec((kb, d), lambda i: (0, 0), pipeline_mode=resident),
            pl.BlockSpec((d, d), lambda i: (0, 0), pipeline_mode=resident),
            pl.BlockSpec((tm, d), lambda i: (i, 0)),
            pl.BlockSpec((1, d), lambda i: (0, 0)),
        ],
        out_specs=[pl.BlockSpec((tm, d), lambda i: (i, 0)), pl.BlockSpec((tm, d), lambda i: (i, 0))],
        out_shape=[jax.ShapeDtypeStruct((m, d), F32), jax.ShapeDtypeStruct((m, d), BF16)],
        scratch_shapes=[pltpu.VMEM((tm, d), BF16)],
        compiler_params=_params(("parallel",)),
        name="merge_out",
    )(oa, ob, gates, gates, ba, bb, wa, wb, wo, x, g2)


def _mm_res_kernel(a_ref, w_ref, r_ref, o_ref):
    o_ref[...] = r_ref[...] + _dot(a_ref[...], w_ref[...])


def _mm_res(a, w, res, *, tm, tn, name):
    m, k = a.shape
    n = w.shape[1]
    return pl.pallas_call(
        _mm_res_kernel,
        grid=(m // tm, n // tn),
        in_specs=[
            pl.BlockSpec((tm, k), lambda i, j: (i, 0)),
            pl.BlockSpec((k, tn), lambda i, j: (0, j)),
            pl.BlockSpec((tm, tn), lambda i, j: (i, j)),
        ],
        out_specs=pl.BlockSpec((tm, tn), lambda i, j: (i, j)),
        out_shape=jax.ShapeDtypeStruct((m, n), F32),
        compiler_params=_params(("parallel", "parallel")),
        name=name,
    )(a, w, res)


def _ffn_up_kernel(u_ref, wg_ref, wu_ref, o_ref):
    u = u_ref[...]
    for c0 in range(0, o_ref.shape[1], V7X_MXU_DIM):
        cs = slice(c0, c0 + V7X_MXU_DIM)
        gate = _dot(u, wg_ref[:, cs])
        up = _dot(u, wu_ref[:, cs])
        o_ref[:, cs] = (gate * _sigmoid(gate) * up).astype(o_ref.dtype)


def _ffn_up(u, wg, wu, *, tm=2048, tn=512):
    m, d = u.shape
    n = wg.shape[1]
    return pl.pallas_call(
        _ffn_up_kernel,
        grid=(m // tm, n // tn),
        in_specs=[
            pl.BlockSpec((tm, d), lambda i, j: (i, 0)),
            pl.BlockSpec((d, tn), lambda i, j: (0, j)),
            pl.BlockSpec((d, tn), lambda i, j: (0, j)),
        ],
        out_specs=pl.BlockSpec((tm, tn), lambda i, j: (i, j)),
        out_shape=jax.ShapeDtypeStruct((m, n), BF16),
        compiler_params=_params(("parallel", "parallel")),
        name="ffn_up",
    )(u, wg, wu)


def kernel(x, norm1_g, w_in, b_gate, q_norm_a, k_norm_a, sinks_a, q_norm_b, k_norm_b,
           w_branch_a, w_branch_b, w_o, norm2_g, w_ffn_gate, w_ffn_up, w_ffn_down):
    B, S, D = x.shape
    depth = w_in.shape[0]
    assert D == D_MODEL and w_in.shape[2] == IN_COLS
    assert S % MOBA_BLOCK == 0 and S % WINDOW == 0
    M = B * S

    slopes_b = jnp.asarray(np.exp2(-8.0 * np.arange(1, N_H_B + 1, dtype=np.float32) / N_H_B), F32)
    swa_bias = _swa_bias_table()

    h = x.reshape(M, D)
    for l in range(depth):
        head_gain = jnp.concatenate([
            jnp.tile(q_norm_a[l], N_Q_A) * (HEAD_DIM_A ** -0.5 * LOG2E),
            jnp.tile(k_norm_a[l], N_KV_A),
            jnp.ones((WKV_A,), F32),
            jnp.tile(q_norm_b[l], N_H_B) * (HEAD_DIM_B ** -0.5 * LOG2E),
            jnp.tile(k_norm_b[l], N_H_B),
            jnp.ones((W_B,), F32),
        ]).reshape(1, COL_GA)
        qkv, u = _qkv_proj(h, norm1_g[l].reshape(1, D), w_in[l, :, :COL_GA].astype(BF16), head_gain, tm=512)
        z3 = qkv.reshape(B, S, COL_GA)
        o_a, (wg_b, wu_b) = _swa(z3, sinks_a[l], swa_bias, [w_ffn_gate[l], w_ffn_up[l]])
        o_b, (wd_b, wgate_b) = _moba(z3, slopes_b,
                                     [w_ffn_down[l], (w_in[l], COL_GA, 2 * D, SIDE_CAST_COL_TILE)])
        gates, (wo_b, wa_b, wb_b) = _gate_proj(u, wgate_b, [w_o[l], w_branch_a[l], w_branch_b[l]])
        h1, u2 = _merge_out(o_a.reshape(M, WQ_A), o_b.reshape(M, W_B), gates,
                            b_gate[l, :D].reshape(1, D), b_gate[l, D:].reshape(1, D),
                            wa_b, wb_b, wo_b, h, norm2_g[l].reshape(1, D))
        act = _ffn_up(u2, wg_b, wu_b)
        h = _mm_res(act, wd_b, h1, tm=1024, tn=512, name="ffn_down")
    return h.reshape(B, S, D)
```

```python
import functools

import numpy as np
import jax
import jax.numpy as jnp
from jax import lax
from jax.experimental import pallas as pl
from jax.experimental.pallas import tpu as pltpu

F32 = jnp.float32
BF16 = jnp.bfloat16

D_MODEL = 2048
HEAD_DIM_A = 64
N_Q_A = 16
N_KV_A = 4
WINDOW = 128
HEAD_DIM_B = 128
N_H_B = 8
MOBA_BLOCK = 256
MOBA_TOPK = 3
RMS_EPS = 1e-6

WQ_A = N_Q_A * HEAD_DIM_A
WKV_A = N_KV_A * HEAD_DIM_A
W_B = N_H_B * HEAD_DIM_B
COL_QA = 0
COL_KA = COL_QA + WQ_A
COL_VA = COL_KA + WKV_A
COL_QB = COL_VA + WKV_A
COL_KB = COL_QB + W_B
COL_VB = COL_KB + W_B
COL_GA = COL_VB + W_B
COL_GB = COL_GA + D_MODEL
IN_COLS = COL_GB + D_MODEL

V7X_LANES = 128
V7X_MXU_DIM = 256
VMEM_LIMIT_BYTES = 56 * 1024 * 1024
NORM_ROW_CHUNK = 64
NORM_UNROLL = 8
BF16_SUBLANE_TILE = 16
SIDE_CAST_COL_TILE = 512
SWA_QBLOCKS_PER_STEP = 8
SWA_SCORE_LOOKAHEAD = 4
MOBA_HEADS_PER_STEP = 2
MOBA_TASK_LOOKAHEAD = 4
GATE_ROWS = BF16_SUBLANE_TILE
NEG_INF = float("-inf")
LOG2E = 1.4426950408889634


def _params(sem):
    return pltpu.CompilerParams(dimension_semantics=sem, vmem_limit_bytes=VMEM_LIMIT_BYTES)


def _dot(a, b):
    return jnp.dot(a, b, preferred_element_type=F32)


def _dot_nt(a, b):
    return lax.dot_general(a, b, (((1,), (1,)), ((), ())), preferred_element_type=F32)


def _split_bf16(x):
    hi = x.astype(BF16)
    lo = (x - hi.astype(F32)).astype(BF16)
    return hi, lo


def _rms_rows_to_bf16(x_ref, g_ref, u_ref):
    rows = x_ref.shape[0]
    g = g_ref[...]

    def body(c, carry):
        r = pl.multiple_of(c * NORM_ROW_CHUNK, NORM_ROW_CHUNK)
        x = x_ref[pl.ds(r, NORM_ROW_CHUNK), :]
        ms = jnp.mean(x * x, axis=-1, keepdims=True)
        u_ref[pl.ds(r, NORM_ROW_CHUNK), :] = (x * lax.rsqrt(ms + RMS_EPS) * g).astype(BF16)
        return carry

    lax.fori_loop(0, rows // NORM_ROW_CHUNK, body, 0, unroll=NORM_UNROLL)


def _head_norm_kind(col):
    if col < COL_VA:
        return "norm64"
    if COL_QB <= col < COL_VB:
        return "norm128"
    return "copy"


def _qkv_proj_kernel(x_ref, g_ref, w_ref, hg_ref, o_ref, u_ref):
    _rms_rows_to_bf16(x_ref, g_ref, u_ref)
    CH = V7X_MXU_DIM

    def finish(z, cs, kind):
        if kind == "norm64":
            left = lax.broadcasted_iota(jnp.int32, (1, V7X_LANES), 1) < HEAD_DIM_A
            tiles = []
            for h0 in range(0, CH, V7X_LANES):
                zt = z[:, h0:h0 + V7X_LANES]
                sq = zt * zt
                ms_l = jnp.sum(jnp.where(left, sq, 0.0), axis=-1, keepdims=True)
                ms_r = jnp.sum(jnp.where(left, 0.0, sq), axis=-1, keepdims=True)
                ms = jnp.where(left, ms_l, ms_r) * (1.0 / HEAD_DIM_A)
                tiles.append(zt * lax.rsqrt(ms + RMS_EPS))
            z = jnp.concatenate(tiles, axis=1) * hg_ref[:, cs]
        elif kind == "norm128":
            halves = []
            for h0 in range(0, CH, HEAD_DIM_B):
                zh = z[:, h0:h0 + HEAD_DIM_B]
                ms = jnp.mean(zh * zh, axis=-1, keepdims=True)
                halves.append(zh * lax.rsqrt(ms + RMS_EPS))
            z = jnp.concatenate(halves, axis=1) * hg_ref[:, cs]
        o_ref[:, cs] = z.astype(o_ref.dtype)

    u = u_ref[...]
    chunks = [slice(c0, c0 + CH) for c0 in range(0, o_ref.shape[1], CH)]
    z_next = _dot(u, w_ref[:, chunks[0]])
    for idx, cs in enumerate(chunks):
        z = z_next
        if idx + 1 < len(chunks):
            z_next = _dot(u, w_ref[:, chunks[idx + 1]])
        finish(z, cs, _head_norm_kind(cs.start))


def _qkv_proj(x, g, w, head_gain, *, tm):
    m, d = x.shape
    n = w.shape[1]
    assert n % V7X_MXU_DIM == 0
    resident = pl.Buffered(1)
    return pl.pallas_call(
        _qkv_proj_kernel,
        grid=(m // tm,),
        in_specs=[
            pl.BlockSpec((tm, d), lambda i: (i, 0)),
            pl.BlockSpec((1, d), lambda i: (0, 0)),
            pl.BlockSpec((d, n), lambda i: (0, 0), pipeline_mode=resident),
            pl.BlockSpec((1, n), lambda i: (0, 0)),
        ],
        out_specs=[
            pl.BlockSpec((tm, n), lambda i: (i, 0)),
            pl.BlockSpec((tm, d), lambda i: (i, 0)),
        ],
        out_shape=[jax.ShapeDtypeStruct((m, n), BF16), jax.ShapeDtypeStruct((m, d), BF16)],
        compiler_params=_params(("parallel",)),
        name="qkv_proj",
    )(x, g, w, head_gain)


def _side_cast_specs(arrays, n_steps, step_index):
    in_specs, out_specs, shapes = [], [], []
    for item in arrays:
        w, col0, ncols, col_tile = item if isinstance(item, tuple) else (item, 0, item.shape[1], item.shape[1])
        rows = w.shape[0]
        n_col = ncols // col_tile
        n_row = n_steps // n_col
        chunk = rows // n_row
        assert n_col * col_tile == ncols and n_row * n_col == n_steps and col0 % col_tile == 0, (w.shape, n_steps)
        assert chunk * n_row == rows and chunk % BF16_SUBLANE_TILE == 0, (w.shape, n_steps)

        def in_map(*g, n_col=n_col, c0=col0 // col_tile):
            s = step_index(*g)
            return (s // n_col, c0 + s % n_col)

        def out_map(*g, n_col=n_col):
            s = step_index(*g)
            return (s // n_col, s % n_col)

        in_specs.append(pl.BlockSpec((chunk, col_tile), in_map))
        out_specs.append(pl.BlockSpec((chunk, col_tile), out_map))
        shapes.append(jax.ShapeDtypeStruct((rows, ncols), BF16))
    return in_specs, out_specs, shapes


def _side_cast(refs):
    n = len(refs) // 2
    for src, dst in zip(refs[:n], refs[n:]):
        dst[...] = src[...].astype(dst.dtype)


def _gate_proj_kernel(u_ref, w_ref, *refs):
    n_side = (len(refs) - 1) // 2
    o_ref = refs[n_side]
    o_ref[...] = _dot(u_ref[...], w_ref[...]).astype(o_ref.dtype)
    _side_cast(refs[:n_side] + refs[n_side + 1:])


def _gate_proj(u, w, side, *, tm=2048, tn=1024):
    m, k = u.shape
    n = w.shape[1]
    gi, gj = m // tm, n // tn
    side_in, side_out, side_shapes = _side_cast_specs(side, gi * gj, lambda i, j: i * gj + j)
    outs = pl.pallas_call(
        _gate_proj_kernel,
        grid=(gi, gj),
        in_specs=[
            pl.BlockSpec((tm, k), lambda i, j: (i, 0)),
            pl.BlockSpec((k, tn), lambda i, j: (0, j)),
        ] + side_in,
        out_specs=[pl.BlockSpec((tm, tn), lambda i, j: (i, j))] + side_out,
        out_shape=[jax.ShapeDtypeStruct((m, n), BF16)] + side_shapes,
        compiler_params=_params(("arbitrary", "arbitrary")),
        name="gate_proj",
    )(u, w, *side)
    return outs[0], outs[1:]


def _swa_kernel(sinks_ref, q_ref, kc_ref, kp_ref, vc_ref, vp_ref, bias_ref, *refs):
    n_side = (len(refs) - 1) // 2
    o_ref = refs[n_side]
    _side_cast(refs[:n_side] + refs[n_side + 1:])
    L = WINDOW
    QB = SWA_QBLOCKS_PER_STEP
    first_step = (pl.program_id(1) == 0).astype(jnp.int32)
    k = jnp.concatenate([kp_ref[0], kc_ref[0]], axis=0).astype(F32)
    v = jnp.concatenate([vp_ref[0], vc_ref[0]], axis=0).astype(F32)

    lane = lax.broadcasted_iota(jnp.int32, (1, V7X_LANES), 1)
    left = lane < HEAD_DIM_A
    lo_head = lax.broadcasted_iota(jnp.int32, (1, 2 * L), 1) < L

    n_kt = WKV_A // V7X_LANES
    lane_tile = lambda t: slice(t * V7X_LANES, (t + 1) * V7X_LANES)
    kt = [k[:, lane_tile(t)] for t in range(n_kt)]
    kt_sw = [pltpu.roll(x, HEAD_DIM_A, axis=1).astype(BF16) for x in kt]
    kt = [x.astype(BF16) for x in kt]
    pad_row = lax.broadcasted_iota(jnp.int32, (BF16_SUBLANE_TILE, k.shape[0]), 0)
    ones_row = jnp.where(pad_row == 0, 1.0, 0.0)
    vt = [jnp.concatenate([v[:, lane_tile(t)].T, ones_row], axis=0).astype(BF16) for t in range(n_kt)]

    def q_masked(qb, h):
        qt = q_ref[0, qb * L:(qb + 1) * L, lane_tile(h // 2)]
        keep = left if h % 2 == 0 else jnp.logical_not(left)
        return jnp.where(keep, qt, jnp.zeros_like(qt))

    rep = N_Q_A // N_KV_A

    def scores(qb, g, hpar):
        t, par = g // 2, g % 2
        ha, hb = rep * g + hpar, rep * g + hpar + 2
        k_al = (kt if par == hpar else kt_sw)[t][qb * L:(qb + 2) * L]
        qm = jnp.concatenate([q_masked(qb, ha), q_masked(qb, hb)], axis=0)
        first = first_step if qb == 0 else 0
        return _dot_nt(k_al, qm) + bias_ref[first, 2 * g + hpar]

    def weighted_values(qb, g, hpar, s, out_rows):
        t, par = g // 2, g % 2
        ha, hb = rep * g + hpar, rep * g + hpar + 2
        sink = jnp.where(lo_head, sinks_ref[ha], sinks_ref[hb]) * LOG2E
        m = jnp.maximum(jnp.max(s, axis=0, keepdims=True), sink)
        e = jnp.exp2(s - m).astype(BF16)
        ot = _dot(vt[t][:, qb * L:(qb + 2) * L], e)
        denom = ot[V7X_LANES:V7X_LANES + 1, :] + jnp.exp2(sink - m)
        og = ot[par * HEAD_DIM_A:(par + 1) * HEAD_DIM_A, :] * (1.0 / denom)
        out_rows[ha] = og[:, :L]
        out_rows[hb] = og[:, L:]

    tasks = [(qb, g, hpar) for qb in range(QB) for g in range(N_KV_A) for hpar in range(2)]
    ahead = {}
    out_rows = {}
    for t in range(len(tasks) + SWA_SCORE_LOOKAHEAD):
        if t < len(tasks):
            ahead[t] = scores(*tasks[t])
        d = t - SWA_SCORE_LOOKAHEAD
        if d >= 0:
            qb, g, hpar = tasks[d]
            rows = out_rows.setdefault(qb, [None] * N_Q_A)
            weighted_values(qb, g, hpar, ahead.pop(d), rows)
            if (g, hpar) == (N_KV_A - 1, 1):
                o_t = jnp.concatenate(out_rows.pop(qb), axis=0)
                o_ref[0, qb * L:(qb + 1) * L, :] = o_t.T.astype(o_ref.dtype)


def _swa(z3, sinks, bias, side):
    B, S, _ = z3.shape
    L = WINDOW
    QB = SWA_QBLOCKS_PER_STEP
    nb = S // (QB * L)
    kblk = COL_KA // WKV_A
    vblk = COL_VA // WKV_A
    prev = lambda n: jnp.maximum(n * QB - 1, 0)
    side_in, side_out, side_shapes = _side_cast_specs(side, B * nb, lambda b, n: b * nb + n)
    outs = pl.pallas_call(
        _swa_kernel,
        grid=(B, nb),
        in_specs=[
            pl.BlockSpec(memory_space=pltpu.SMEM),
            pl.BlockSpec((1, QB * L, WQ_A), lambda b, n: (b, n, 0)),
            pl.BlockSpec((1, QB * L, WKV_A), lambda b, n: (b, n, kblk)),
            pl.BlockSpec((1, L, WKV_A), lambda b, n: (b, prev(n), kblk)),
            pl.BlockSpec((1, QB * L, WKV_A), lambda b, n: (b, n, vblk)),
            pl.BlockSpec((1, L, WKV_A), lambda b, n: (b, prev(n), vblk)),
            pl.BlockSpec((2, N_Q_A // 2, 2 * L, 2 * L), lambda b, n: (0, 0, 0, 0)),
        ] + side_in,
        out_specs=[pl.BlockSpec((1, QB * L, WQ_A), lambda b, n: (b, n, 0))] + side_out,
        out_shape=[jax.ShapeDtypeStruct((B, S, WQ_A), BF16)] + side_shapes,
        compiler_params=_params(("arbitrary", "arbitrary")),
        name="swa",
    )(sinks, z3, z3, z3, z3, z3, bias, *side)
    return outs[0], outs[1:]


def _swa_bias_table():
    L = WINDOW
    rep = N_Q_A // N_KV_A
    slopes = np.exp2(-8.0 * np.arange(1, N_Q_A + 1, dtype=np.float32) / N_Q_A).astype(np.float32)
    kj = np.arange(2 * L)[:, None]
    qi = np.arange(L)[None, :]
    dist = L + qi - kj
    window = (dist >= 0) & (dist < WINDOW)
    table = np.empty((2, N_Q_A // 2, 2 * L, 2 * L), np.float32)
    for first in range(2):
        valid = window & ((kj >= L) if first else True)
        for g in range(N_KV_A):
            for hpar in range(2):
                for a in range(2):
                    h = rep * g + hpar + 2 * a
                    table[first, 2 * g + hpar, :, a * L:(a + 1) * L] = np.where(
                        valid, -(slopes[h] * LOG2E) * dist.astype(np.float32), -np.inf)
    return jnp.asarray(table)


def _moba_kernel(slopes_ref, q_ref, k_ref, v_ref, *refs):
    n_side = (len(refs) - 4) // 2
    o_ref = refs[n_side]
    vt_ref, gate_ref, tab_ref = refs[2 * n_side + 1:]
    _side_cast(refs[:n_side] + refs[n_side + 1:2 * n_side + 1])
    BLK = MOBA_BLOCK
    HP = MOBA_HEADS_PER_STEP
    dh = HEAD_DIM_B
    hp = pl.program_id(1)
    S = k_ref.shape[1]
    nblk = S // BLK

    def prepare_tables():
        kc = lax.broadcasted_iota(jnp.int32, (BLK, BLK), 0)
        qr = lax.broadcasted_iota(jnp.int32, (BLK, BLK), 1)
        rel = (qr - kc).astype(F32)
        for a in range(HP):
            slope2 = slopes_ref[hp * HP + a] * LOG2E
            tab_ref[a, 0] = -slope2 * rel
            tab_ref[a, 1] = jnp.where(rel >= 0.0, -slope2 * rel, NEG_INF)

    def prepare_values():
        eye = (lax.broadcasted_iota(jnp.int32, (dh, dh), 0)
               == lax.broadcasted_iota(jnp.int32, (dh, dh), 1)).astype(BF16)
        pad_row = lax.broadcasted_iota(jnp.int32, (BF16_SUBLANE_TILE, S), 0)
        for a in range(HP):
            vt_ref[a, 0:dh, :] = _dot_nt(eye, v_ref[0, :, a * dh:(a + 1) * dh]).astype(BF16)
            vt_ref[a, dh:, :] = jnp.where(pad_row == 0, 1.0, 0.0).astype(BF16)

    def block_means():
        blk = lax.broadcasted_iota(jnp.int32, (GATE_ROWS, S), 0)
        pos = lax.broadcasted_iota(jnp.int32, (GATE_ROWS, S), 1)
        member = jnp.where((pos >= blk * BLK) & (pos < (blk + 1) * BLK), 1.0 / BLK, 0.0).astype(BF16)
        return [_split_bf16(_dot(member, k_ref[0, :, a * dh:(a + 1) * dh])) for a in range(HP)]

    def prepare_gate(kmeans):
        for a, (km_hi, km_lo) in enumerate(kmeans):
            qn = q_ref[0, :, a * dh:(a + 1) * dh]
            gate_ref[a] = _dot_nt(km_hi, qn) + _dot_nt(km_lo, qn)

    def block_max_shifts(c, a):
        slope2 = slopes_ref[hp * HP + a] * LOG2E
        shifts = [-slope2 * float((c - n) * BLK) for n in range(c)]
        if c > MOBA_TOPK:
            gate = gate_ref[a, :, c * BLK:(c + 1) * BLK]
            blk = lax.broadcasted_iota(jnp.int32, (GATE_ROWS, 1), 0)
            past = blk < c
            for n in range(c):
                g_n = gate[n:n + 1, :]
                beats = ((gate > g_n) | ((gate == g_n) & (blk < n))) & past
                rank = jnp.sum(jnp.where(beats, 1.0, 0.0), axis=0, keepdims=True)
                shifts[n] = jnp.where(rank < float(MOBA_TOPK), shifts[n], NEG_INF)
        return shifts

    def scores(c, a, ss, ms):
        for j in range(c + 1):
            s = _dot_nt(k_ref[0, j * BLK:(j + 1) * BLK, a * dh:(a + 1) * dh],
                        q_ref[0, c * BLK:(c + 1) * BLK, a * dh:(a + 1) * dh])
            s = s + tab_ref[a, 1 if j == c else 0]
            ss.append(s)
            ms.append(jnp.max(s, axis=0, keepdims=True))
            yield

    def weighted_values(c, a, ss, ms):
        shifts = block_max_shifts(c, a)
        m = functools.reduce(jnp.maximum, [m_j + sh for m_j, sh in zip(ms, shifts)] + ms[c:])
        acc = None
        for j in range(c + 1):
            p = jnp.exp2(ss[j] - (m - shifts[j] if j < c else m)).astype(BF16)
            part = _dot(vt_ref[a, :, j * BLK:(j + 1) * BLK], p)
            acc = part if acc is None else acc + part
            if j < c:
                yield
        ot = acc[0:dh] * (1.0 / acc[dh:dh + 1])
        o_ref[0, c * BLK:(c + 1) * BLK, a * dh:(a + 1) * dh] = ot.T.astype(o_ref.dtype)
        yield

    prepare_tables()
    kmeans = block_means()
    tasks = [(c, a) for c in range(nblk) for a in range(HP)]
    ahead = {}
    for t in range(len(tasks) + MOBA_TASK_LOOKAHEAD):
        running = []
        if t < len(tasks):
            ahead[t] = ([], [])
            running.append(scores(*tasks[t], *ahead[t]))
        d = t - MOBA_TASK_LOOKAHEAD
        if d >= 0:
            c, a = tasks[d]
            if d == 0:
                prepare_values()
            if (c, a) == (MOBA_TOPK, 0):
                prepare_gate(kmeans)
            running.append(weighted_values(c, a, *ahead.pop(d)))
        while running:
            running = [g for g in running if next(g, StopIteration) is not StopIteration]


def _moba(z3, slopes, side):
    B, S, _ = z3.shape
    BLK = MOBA_BLOCK
    HP = MOBA_HEADS_PER_STEP
    w = HP * HEAD_DIM_B
    qc, kc, vc = COL_QB // w, COL_KB // w, COL_VB // w
    n_hp = N_H_B // HP
    side_in, side_out, side_shapes = _side_cast_specs(side, B * n_hp, lambda b, h: b * n_hp + h)
    outs = pl.pallas_call(
        _moba_kernel,
        grid=(B, n_hp),
        in_specs=[
            pl.BlockSpec(memory_space=pltpu.SMEM),
            pl.BlockSpec((1, S, w), lambda b, h: (b, 0, qc + h)),
            pl.BlockSpec((1, S, w), lambda b, h: (b, 0, kc + h)),
            pl.BlockSpec((1, S, w), lambda b, h: (b, 0, vc + h)),
        ] + side_in,
        out_specs=[pl.BlockSpec((1, S, w), lambda b, h: (b, 0, h))] + side_out,
        out_shape=[jax.ShapeDtypeStruct((B, S, W_B), BF16)] + side_shapes,
        scratch_shapes=[
            pltpu.VMEM((HP, HEAD_DIM_B + BF16_SUBLANE_TILE, S), BF16),
            pltpu.VMEM((HP, GATE_ROWS, S), F32),
            pltpu.VMEM((HP, 2, BLK, BLK), F32),
        ],
        compiler_params=_params(("arbitrary", "arbitrary")),
        name="moba",
    )(slopes, z3, z3, z3, *[s[0] if isinstance(s, tuple) else s for s in side])
    return outs[0], outs[1:]


def _sigmoid(x):
    return 1.0 / (1.0 + jnp.exp2(x * (-LOG2E)))


def _merge_out_kernel(oa_ref, ob_ref, ga_ref, gb_ref, ba_ref, bb_ref, wa_ref, wb_ref, wo_ref, x_ref,
                      g2_ref, o_ref, u2_ref, mixed_ref, *, chunk):
    oa = oa_ref[...]
    ob = ob_ref[...]
    for c0 in range(0, mixed_ref.shape[1], chunk):
        cs = slice(c0, c0 + chunk)
        a = _dot(oa, wa_ref[:, cs])
        b = _dot(ob, wb_ref[:, cs])
        ga = _sigmoid(ga_ref[:, cs].astype(F32) + ba_ref[:, cs])
        gb = _sigmoid(gb_ref[:, cs].astype(F32) + bb_ref[:, cs])
        mixed_ref[:, cs] = (ga * a + gb * b).astype(mixed_ref.dtype)
    h1 = x_ref[...] + _dot(mixed_ref[...], wo_ref[...])
    o_ref[...] = h1
    inv = lax.rsqrt(jnp.mean(h1 * h1, axis=-1, keepdims=True) + RMS_EPS)
    u2_ref[...] = (h1 * inv * g2_ref[...]).astype(u2_ref.dtype)


def _merge_out(oa, ob, gates, ba, bb, wa, wb, wo, x, g2, *, tm=512, chunk=512):
    m, ka = oa.shape
    kb = ob.shape[1]
    d = wa.shape[1]
    resident = pl.Buffered(1)
    return pl.pallas_call(
        functools.partial(_merge_out_kernel, chunk=chunk),
        grid=(m // tm,),
        in_specs=[
            pl.BlockSpec((tm, ka), lambda i: (i, 0)),
            pl.BlockSpec((tm, kb), lambda i: (i, 0)),
            pl.BlockSpec((tm, d), lambda i: (i, 0)),
            pl.BlockSpec((tm, d), lambda i: (i, 1)),
            pl.BlockSpec((1, d), lambda i: (0, 0)),
            pl.BlockSpec((1, d), lambda i: (0, 0)),
            pl.BlockSpec((ka, d), lambda i: (0, 0), pipeline_mode=resident),
            pl.BlockSpec((kb, d), lambda i: (0, 0), pipeline_mode=resident),
            pl.BlockSpec((d, d), lambda i: (0, 0), pipeline_mode=resident),
            pl.BlockSpec((tm, d), lambda i: (i, 0)),
            pl.BlockSpec((1, d), lambda i: (0, 0)),
        ],
        out_specs=[pl.BlockSpec((tm, d), lambda i: (i, 0)), pl.BlockSpec((tm, d), lambda i: (i, 0))],
        out_shape=[jax.ShapeDtypeStruct((m, d), F32), jax.ShapeDtypeStruct((m, d), BF16)],
        scratch_shapes=[pltpu.VMEM((tm, d), BF16)],
        compiler_params=_params(("parallel",)),
        name="merge_out",
    )(oa, ob, gates, gates, ba, bb, wa, wb, wo, x, g2)


def _mm_res_kernel(a_ref, w_ref, r_ref, o_ref):
    o_ref[...] = r_ref[...] + _dot(a_ref[...], w_ref[...])


def _mm_res(a, w, res, *, tm, tn, name):
    m, k = a.shape
    n = w.shape[1]
    return pl.pallas_call(
        _mm_res_kernel,
        grid=(n // tn, m // tm),
        in_specs=[
            pl.BlockSpec((tm, k), lambda j, i: (i, 0)),
            pl.BlockSpec((k, tn), lambda j, i: (0, j)),
            pl.BlockSpec((tm, tn), lambda j, i: (i, j)),
        ],
        out_specs=pl.BlockSpec((tm, tn), lambda j, i: (i, j)),
        out_shape=jax.ShapeDtypeStruct((m, n), F32),
        compiler_params=_params(("parallel", "parallel")),
        name=name,
    )(a, w, res)


def _ffn_up_kernel(u_ref, wg_ref, wu_ref, o_ref):
    u = u_ref[...]
    for c0 in range(0, o_ref.shape[1], V7X_MXU_DIM):
        cs = slice(c0, c0 + V7X_MXU_DIM)
        gate = _dot(u, wg_ref[:, cs])
        up = _dot(u, wu_ref[:, cs])
        o_ref[:, cs] = (gate * _sigmoid(gate) * up).astype(o_ref.dtype)


def _ffn_up(u, wg, wu, *, tm=2048, tn=512):
    m, d = u.shape
    n = wg.shape[1]
    return pl.pallas_call(
        _ffn_up_kernel,
        grid=(m // tm, n // tn),
        in_specs=[
            pl.BlockSpec((tm, d), lambda i, j: (i, 0)),
            pl.BlockSpec((d, tn), lambda i, j: (0, j)),
            pl.BlockSpec((d, tn), lambda i, j: (0, j)),
        ],
        out_specs=pl.BlockSpec((tm, tn), lambda i, j: (i, j)),
        out_shape=jax.ShapeDtypeStruct((m, n), BF16),
        compiler_params=_params(("parallel", "parallel")),
        name="ffn_up",
    )(u, wg, wu)


def kernel(x, norm1_g, w_in, b_gate, q_norm_a, k_norm_a, sinks_a, q_norm_b, k_norm_b,
           w_branch_a, w_branch_b, w_o, norm2_g, w_ffn_gate, w_ffn_up, w_ffn_down):
    B, S, D = x.shape
    depth = w_in.shape[0]
    assert D == D_MODEL and w_in.shape[2] == IN_COLS
    assert S % MOBA_BLOCK == 0 and S % WINDOW == 0
    M = B * S

    slopes_b = jnp.asarray(np.exp2(-8.0 * np.arange(1, N_H_B + 1, dtype=np.float32) / N_H_B), F32)
    swa_bias = _swa_bias_table()

    h = x.reshape(M, D)
    for l in range(depth):
        head_gain = jnp.concatenate([
            jnp.tile(q_norm_a[l], N_Q_A) * (HEAD_DIM_A ** -0.5 * LOG2E),
            jnp.tile(k_norm_a[l], N_KV_A),
            jnp.ones((WKV_A,), F32),
            jnp.tile(q_norm_b[l], N_H_B) * (HEAD_DIM_B ** -0.5 * LOG2E),
            jnp.tile(k_norm_b[l], N_H_B),
            jnp.ones((W_B,), F32),
        ]).reshape(1, COL_GA)
        qkv, u = _qkv_proj(h, norm1_g[l].reshape(1, D), w_in[l, :, :COL_GA].astype(BF16), head_gain, tm=512)
        z3 = qkv.reshape(B, S, COL_GA)
        o_a, (wg_b, wu_b) = _swa(z3, sinks_a[l], swa_bias, [w_ffn_gate[l], w_ffn_up[l]])
        o_b, (wd_b, wgate_b) = _moba(z3, slopes_b,
                                     [w_ffn_down[l], (w_in[l], COL_GA, 2 * D, SIDE_CAST_COL_TILE)])
        gates, (wo_b, wa_b, wb_b) = _gate_proj(u, wgate_b, [w_o[l], w_branch_a[l], w_branch_b[l]])
        h1, u2 = _merge_out(o_a.reshape(M, WQ_A), o_b.reshape(M, W_B), gates,
                            b_gate[l, :D].reshape(1, D), b_gate[l, D:].reshape(1, D),
                            wa_b, wb_b, wo_b, h, norm2_g[l].reshape(1, D))
        act = _ffn_up(u2, wg_b, wu_b)
        h = _mm_res(act, wd_b, h1, tm=1024, tn=512, name="ffn_down")
    return h.reshape(B, S, D)
```
